```python
import math
import jax, jax.numpy as jnp
from jax import lax
import numpy as np

D_MODEL = 1024
BATCH = 8
SEQ = 4096
DEPTH = 1

PLE_DIM = 256
EPS = 1e-6
NEG = -1e30
FORCE = 1e6

HG_HEADS = 4
HG_KDIM = 128
HG_VDIM = 128
HG_WIDTH = HG_HEADS * HG_VDIM
HG_CHUNK = 64

NSA_HEADS = 8
NSA_KV = 2
NSA_GROUP = NSA_HEADS // NSA_KV
HEAD_DIM = 64
NSA_WIDTH = NSA_HEADS * HEAD_DIM
KV_WIDTH = NSA_KV * HEAD_DIM
CMP_LEN = 32
CMP_STRIDE = 16
CMP_HIDDEN = 128
SEL_BLOCK = 64
SEL_TOPK = 16
WINDOW = 512
Q_BLOCK = 64

ROT_DIM = HEAD_DIM // 4
ROPE_THETA = 500000.0

MIX_WIDTH = HG_WIDTH + NSA_WIDTH

D_FF = 2816
CONV_W = 3

IN_SIZES = (HG_HEADS * HG_KDIM, HG_HEADS * HG_KDIM, HG_WIDTH, HG_WIDTH, NSA_WIDTH,
            KV_WIDTH, KV_WIDTH, KV_WIDTH, KV_WIDTH, KV_WIDTH, KV_WIDTH, 3 * NSA_HEADS)
IN_TOTAL = 2 * HG_HEADS * HG_KDIM + 2 * HG_WIDTH + NSA_WIDTH + 6 * KV_WIDTH + 3 * NSA_HEADS

kernel_name = "hymba_hgrn2_nsa_convffn_block"


def rms_norm(x, g):
    xf = x.astype(jnp.float32)
    y = xf * lax.rsqrt(jnp.mean(xf * xf, axis=-1, keepdims=True) + EPS)
    return (y * g.astype(jnp.float32)).astype(x.dtype)


def partial_rope(x, pos):
    half = ROT_DIM // 2
    inv = ROPE_THETA ** (-jnp.arange(half, dtype=jnp.float32) / half)
    ang = pos.astype(jnp.float32)[:, None] * inv[None, :]
    cos = jnp.cos(ang)[:, None, :]
    sin = jnp.sin(ang)[:, None, :]
    xr = x[..., :ROT_DIM].astype(jnp.float32)
    x1, x2 = xr[..., :half], xr[..., half:]
    rot = jnp.concatenate([x1 * cos - x2 * sin, x2 * cos + x1 * sin], axis=-1)
    return jnp.concatenate([rot.astype(x.dtype), x[..., ROT_DIM:]], axis=-1)


def hgrn2_mixer(q, f_pre, v, g, lb, norm_g):
    B, T, _ = q.shape
    dt = q.dtype
    f32 = jnp.float32
    n_ch = T // HG_CHUNK
    f = lb + (1.0 - lb) * jax.nn.sigmoid(f_pre.astype(f32))
    k = 1.0 - f
    logf = jnp.log(f)
    qf = jax.nn.silu(q.astype(f32)) * HG_KDIM ** -0.5

    def chunks(a, d):
        return a.reshape(B, n_ch, HG_CHUNK, HG_HEADS, d).transpose(1, 0, 3, 2, 4)

    qc = chunks(qf, HG_KDIM)
    kc = chunks(k, HG_KDIM)
    vc = chunks(v.astype(f32), HG_VDIM)
    bc = jnp.cumsum(chunks(logf, HG_KDIM), axis=3)
    causal = jnp.tril(jnp.ones((HG_CHUNK, HG_CHUNK), bool))[:, :, None]

    def step(S, inp):
        qt, kt, vt, bt = inp
        decay = jnp.exp(jnp.where(causal, bt[:, :, :, None, :] - bt[:, :, None, :, :], -jnp.inf))
        a = jnp.einsum('bhtk,bhsk,bhtsk->bhts', qt, kt, decay)
        o = (jnp.einsum('bhts,bhsv->bhtv', a, vt)
             + jnp.einsum('bhtk,bhkv->bhtv', qt * jnp.exp(bt), S))
        b_last = bt[:, :, -1:, :]
        S = (jnp.exp(b_last[:, :, 0, :])[..., None] * S
             + jnp.einsum('bhsk,bhsv->bhkv', kt * jnp.exp(b_last - bt), vt))
        return S, o

    S0 = jnp.zeros((B, HG_HEADS, HG_KDIM, HG_VDIM), f32)
    _, o = lax.scan(step, S0, (qc, kc, vc, bc))
    o = o.transpose(1, 0, 3, 2, 4).reshape(B, T, HG_HEADS, HG_VDIM)
    o = rms_norm(o, norm_g) * jax.nn.silu(g.astype(f32)).reshape(B, T, HG_HEADS, HG_VDIM)
    return o.reshape(B, T, HG_WIDTH).astype(dt)


def nsa_mixer(q, k_cmp, v_cmp, k_sel, v_sel, k_win, v_win, gate_pre,
              q_norm_g, k_norm_g, cmp_pe, cmp_w1, cmp_w2, out_norm_g):
    B, T, _ = q.shape
    dt = q.dtype
    f32 = jnp.float32
    pos = jnp.arange(T)
    scale = HEAD_DIM ** -0.5
    q = partial_rope(rms_norm(q.reshape(B, T, NSA_HEADS, HEAD_DIM), q_norm_g), pos)
    q = q.reshape(B, T, NSA_KV, NSA_GROUP, HEAD_DIM)

    def kv_heads(a):
        return a.reshape(B, T, NSA_KV, HEAD_DIM)

    n_cmp = (T - CMP_LEN) // CMP_STRIDE + 1
    cmp_start = jnp.arange(n_cmp) * CMP_STRIDE
    cmp_end = cmp_start + CMP_LEN - 1
    tok = cmp_start[:, None] + jnp.arange(CMP_LEN)[None, :]

    def compress(a, j):
        blk = kv_heads(a)[:, tok] + cmp_pe[j][None, None, :, None, :]
        blk = blk.transpose(0, 1, 3, 2, 4).reshape(B, n_cmp, NSA_KV, CMP_LEN * HEAD_DIM)
        return jax.nn.silu(blk @ cmp_w1[j]) @ cmp_w2[j]

    kc = partial_rope(rms_norm(compress(k_cmp, 0), k_norm_g[0]), cmp_end)
    vc = compress(v_cmp, 1)
    s = jnp.einsum('btghd,bcgd->bghtc', q, kc).astype(f32) * scale
    cmask = cmp_end[None, :] <= pos[:, None]
    p_cmp = jax.nn.softmax(jnp.where(cmask, s, NEG), axis=-1) * cmask
    o_cmp = jnp.einsum('bghtc,bcgd->btghd', p_cmp.astype(dt), vc)

    n_sel = T // SEL_BLOCK
    top = min(SEL_TOPK, n_sel)
    sel_start = jnp.arange(n_sel) * SEL_BLOCK
    overlap = jnp.clip(jnp.minimum(cmp_start[:, None] + CMP_LEN, sel_start[None, :] + SEL_BLOCK)
                       - jnp.maximum(cmp_start[:, None], sel_start[None, :]), 0, None)
    overlap = overlap.astype(f32) / CMP_LEN
    imp = jnp.einsum('bghtc,cn->btgn', p_cmp, overlap)
    cur = pos // SEL_BLOCK
    blk = jnp.arange(n_sel)
    forced = (blk[None, :] == 0) | (blk[None, :] == cur[:, None]) | (blk[None, :] == cur[:, None] - 1)
    visible = sel_start[None, :] <= pos[:, None]
    imp = jnp.where(visible[:, None, :], imp + jnp.where(forced, FORCE, 0.0)[:, None, :], NEG)
    _, sel_idx = lax.top_k(imp, top)

    def to_blocks(a):
        return a.reshape(B, n_sel, SEL_BLOCK, NSA_KV, HEAD_DIM).transpose(0, 3, 1, 2, 4)

    ks = to_blocks(partial_rope(rms_norm(kv_heads(k_sel), k_norm_g[1]), pos))
    vs = to_blocks(kv_heads(v_sel))
    pad = ((0, 0), (WINDOW, 0), (0, 0), (0, 0))
    kw = jnp.pad(partial_rope(rms_norm(kv_heads(k_win), k_norm_g[2]), pos), pad)
    vw = jnp.pad(kv_heads(v_win), pad)
    n_qb = T // Q_BLOCK
    q_blk = q.reshape(B, n_qb, Q_BLOCK, NSA_KV, NSA_GROUP, HEAD_DIM).swapaxes(0, 1)
    idx_blk = sel_idx.reshape(B, n_qb, Q_BLOCK, NSA_KV, top).swapaxes(0, 1)
    bi = jnp.arange(B)[:, None, None, None]
    gi = jnp.arange(NSA_KV)[None, None, :, None]
    span = Q_BLOCK + WINDOW

    def block_attend(args):
        j, qb, ib = args
        t = j * Q_BLOCK + jnp.arange(Q_BLOCK)
        ksb = ks[bi, gi, ib]
        vsb = vs[bi, gi, ib]
        ss = jnp.einsum('bqghd,bqgnld->bqghnl', qb, ksb).astype(f32) * scale
        kpos = ib[..., None] * SEL_BLOCK + jnp.arange(SEL_BLOCK)
        m = (kpos <= t[None, :, None, None, None])[:, :, :, None]
        ss = jnp.where(m, ss, NEG).reshape(B, Q_BLOCK, NSA_KV, NSA_GROUP, top * SEL_BLOCK)
        ps = jax.nn.softmax(ss, axis=-1).reshape(B, Q_BLOCK, NSA_KV, NSA_GROUP, top, SEL_BLOCK)
        o_s = jnp.einsum('bqghnl,bqgnld->bqghd', ps.astype(dt), vsb)
        kwb = lax.dynamic_slice_in_dim(kw, j * Q_BLOCK, span, axis=1)
        vwb = lax.dynamic_slice_in_dim(vw, j * Q_BLOCK, span, axis=1)
        spos = j * Q_BLOCK - WINDOW + jnp.arange(span)
        dist = t[:, None] - spos[None, :]
        wm = (spos[None, :] >= 0) & (dist >= 0) & (dist < WINDOW)
        sw = jnp.einsum('bqghd,bsgd->bghqs', qb, kwb).astype(f32) * scale
        pw = jax.nn.softmax(jnp.where(wm, sw, NEG), axis=-1)
        o_w = jnp.einsum('bghqs,bsgd->bqghd', pw.astype(dt), vwb)
        return o_s, o_w

    o_sel, o_win = lax.map(block_attend, (jnp.arange(n_qb), q_blk, idx_blk))
    o_sel = o_sel.swapaxes(0, 1).reshape(B, T, NSA_HEADS, HEAD_DIM)
    o_win = o_win.swapaxes(0, 1).reshape(B, T, NSA_HEADS, HEAD_DIM)
    o_cmp = o_cmp.reshape(B, T, NSA_HEADS, HEAD_DIM)

    gate = jax.nn.sigmoid(gate_pre.astype(f32)).reshape(B, T, 3, NSA_HEADS)[..., None].astype(dt)
    o = gate[:, :, 0] * o_cmp + gate[:, :, 1] * o_sel + gate[:, :, 2] * o_win
    return rms_norm(o, out_norm_g).reshape(B, T, NSA_WIDTH)


def conv_ffn(hn, w_up, conv_w, conv_b, w_down):
    u = hn @ w_up
    c = u.shape[-1]
    u = lax.conv_general_dilated(u, conv_w.reshape(CONV_W, 1, c), window_strides=(1,),
                                 padding=[(CONV_W - 1, 0)], dimension_numbers=('NWC', 'WIO', 'NWC'),
                                 feature_group_count=c) + conv_b
    gate, up = jnp.split(u, 2, axis=-1)
    return (jax.nn.silu(gate) * up) @ w_down


def setup_inputs(seed: int = 0) -> dict:
    key = jax.random.key(seed)
    ks = jax.random.split(key, 22)
    f32 = jnp.float32

    def nrm(k, shape, scale):
        return jax.random.normal(k, shape, f32) * scale

    def gain(k, shape):
        return 1.0 + 0.02 * jax.random.normal(k, shape, f32)

    return {
        "x": nrm(ks[0], (BATCH, SEQ, D_MODEL), 1.0),
        "p": nrm(ks[1], (DEPTH, BATCH, SEQ, PLE_DIM), 1.0),
        "attn_norm_g": gain(ks[2], (DEPTH, D_MODEL)),
        "w_in": nrm(ks[3], (DEPTH, D_MODEL, IN_TOTAL), D_MODEL ** -0.5),
        "hg_lb_logits": nrm(ks[4], (DEPTH + 1, HG_HEADS * HG_KDIM), 0.5),
        "hg_norm_g": gain(ks[5], (DEPTH, HG_VDIM)),
        "nsa_q_norm_g": gain(ks[6], (DEPTH, HEAD_DIM)),
        "nsa_k_norm_g": gain(ks[7], (DEPTH, 3, HEAD_DIM)),
        "cmp_pe": nrm(ks[8], (DEPTH, 2, CMP_LEN, HEAD_DIM), 0.1),
        "cmp_w1": nrm(ks[9], (DEPTH, 2, CMP_LEN * HEAD_DIM, CMP_HIDDEN), (CMP_LEN * HEAD_DIM) ** -0.5),
        "cmp_w2": nrm(ks[10], (DEPTH, 2, CMP_HIDDEN, HEAD_DIM), CMP_HIDDEN ** -0.5),
        "nsa_out_norm_g": gain(ks[11], (DEPTH, HEAD_DIM)),
        "w_out": nrm(ks[12], (DEPTH, MIX_WIDTH, D_MODEL), MIX_WIDTH ** -0.5),
        "ffn_norm_g": gain(ks[13], (DEPTH, D_MODEL)),
        "w_up": nrm(ks[14], (DEPTH, D_MODEL, 2 * D_FF), D_MODEL ** -0.5),
        "conv_w": nrm(ks[15], (DEPTH, CONV_W, 2 * D_FF), CONV_W ** -0.5),
        "conv_b": nrm(ks[16], (DEPTH, 2 * D_FF), 0.02),
        "w_down": nrm(ks[17], (DEPTH, D_FF, D_MODEL), D_FF ** -0.5),
        "ple_gate_norm_g": gain(ks[18], (DEPTH, D_MODEL)),
        "w_ple_gate": nrm(ks[19], (DEPTH, D_MODEL, D_MODEL), D_MODEL ** -0.5),
        "w_ple": nrm(ks[20], (DEPTH, PLE_DIM, D_MODEL), PLE_DIM ** -0.5),
        "ple_norm_g": gain(ks[21], (DEPTH, D_MODEL)),
    }


def reference(x, p, attn_norm_g, w_in, hg_lb_logits, hg_norm_g, nsa_q_norm_g, nsa_k_norm_g,
              cmp_pe, cmp_w1, cmp_w2, nsa_out_norm_g, w_out, ffn_norm_g, w_up, conv_w, conv_b,
              w_down, ple_gate_norm_g, w_ple_gate, w_ple, ple_norm_g):
    lb_all = jnp.cumsum(jax.nn.softmax(hg_lb_logits.astype(jnp.float32), axis=0), axis=0)
    split_points = []
    acc = 0
    for size in IN_SIZES[:-1]:
        acc += size
        split_points.append(acc)
    h = x
    for i in range(DEPTH):
        hn = rms_norm(h, attn_norm_g[i])
        proj = hn @ w_in[i]
        (hq, hf, hi, hg, nq, nkc, nvc, nks, nvs, nkw, nvw, ngate) = jnp.split(proj, split_points, axis=-1)
        o_hg = hgrn2_mixer(hq, hf, hi, hg, lb_all[i], hg_norm_g[i])
        o_nsa = nsa_mixer(nq, nkc, nvc, nks, nvs, nkw, nvw, ngate, nsa_q_norm_g[i], nsa_k_norm_g[i],
                          cmp_pe[i], cmp_w1[i], cmp_w2[i], nsa_out_norm_g[i])
        h = h + jnp.concatenate([o_hg, o_nsa], axis=-1) @ w_out[i]
        h = h + conv_ffn(rms_norm(h, ffn_norm_g[i]), w_up[i], conv_w[i], conv_b[i], w_down[i])
        e = rms_norm(p[i] @ w_ple[i], ple_norm_g[i])
        gate = jax.nn.sigmoid(rms_norm(h, ple_gate_norm_g[i]) @ w_ple_gate[i])
        h = h + gate * e
    return h
```

```python
import functools

import jax
import jax.numpy as jnp
from jax import lax
from jax.experimental import pallas as pl
from jax.experimental.pallas import tpu as pltpu

F32 = jnp.float32
BF16 = jnp.bfloat16

D_MODEL = 1024
PLE_DIM = 256
EPS = 1e-6
NEG = -1e30
FORCE = 1e6

HG_HEADS = 4
HG_KDIM = 128
HG_VDIM = 128
HG_WIDTH = HG_HEADS * HG_VDIM
HG_CHUNK = 64

NSA_HEADS = 8
NSA_KV = 2
NSA_GROUP = NSA_HEADS // NSA_KV
HEAD_DIM = 64
NSA_WIDTH = NSA_HEADS * HEAD_DIM
KV_WIDTH = NSA_KV * HEAD_DIM
CMP_LEN = 32
CMP_STRIDE = 16
SEL_BLOCK = 64
SEL_TOPK = 16
WINDOW = 512
Q_BLOCK = 64
ROT_DIM = HEAD_DIM // 4
ROPE_THETA = 500000.0
MIX_WIDTH = HG_WIDTH + NSA_WIDTH
D_FF = 2816
CONV_W = 3

IN_SIZES = (HG_HEADS * HG_KDIM, HG_HEADS * HG_KDIM, HG_WIDTH, HG_WIDTH, NSA_WIDTH,
            KV_WIDTH, KV_WIDTH, KV_WIDTH, KV_WIDTH, KV_WIDTH, KV_WIDTH, 3 * NSA_HEADS)
IN_TOTAL = sum(IN_SIZES)
LANES = 128
IN_PAD = -(-IN_TOTAL // LANES) * LANES
VMEM_LIMIT = 56 * 1024 * 1024

ROW_TILE = 512
FF_CHUNK = 256
HALO = 8


def _rms(x, g):
    return x * lax.rsqrt(jnp.mean(x * x, axis=-1, keepdims=True) + EPS) * g


def _in_proj_kernel(x_ref, g_ref, w_ref, o_ref, *, col_chunk):
    xb = _rms(x_ref[...], g_ref[...]).astype(BF16)
    for c in range(0, o_ref.shape[1], col_chunk):
        o_ref[:, c:c + col_chunk] = jnp.dot(
            xb, w_ref[:, c:c + col_chunk], preferred_element_type=F32).astype(o_ref.dtype)


def _in_proj(x2, g, w_pad, out_dtype):
    n = x2.shape[0]
    width = w_pad.shape[1]
    return pl.pallas_call(
        functools.partial(_in_proj_kernel, col_chunk=width // 3),
        grid=(n // ROW_TILE,),
        in_specs=[pl.BlockSpec((ROW_TILE, D_MODEL), lambda i: (i, 0)),
                  pl.BlockSpec((1, D_MODEL), lambda i: (0, 0)),
                  pl.BlockSpec((D_MODEL, width), lambda i: (0, 0))],
        out_specs=pl.BlockSpec((ROW_TILE, width), lambda i: (i, 0)),
        out_shape=jax.ShapeDtypeStruct((n, width), out_dtype),
        compiler_params=pltpu.CompilerParams(dimension_semantics=("parallel",),
                                             vmem_limit_bytes=VMEM_LIMIT),
        name="in_proj",
    )(x2, g, w_pad)


def _out_proj_kernel(x_ref, a_ref, b_ref, wa_ref, wb_ref, o_ref):
    acc = jnp.dot(a_ref[...], wa_ref[...], preferred_element_type=F32)
    acc += jnp.dot(b_ref[...], wb_ref[...], preferred_element_type=F32)
    o_ref[...] = x_ref[...] + acc


def _out_proj(x2, o_hg, o_nsa, w_a, w_b):
    n = x2.shape[0]
    return pl.pallas_call(
        _out_proj_kernel,
        grid=(n // ROW_TILE,),
        in_specs=[pl.BlockSpec((ROW_TILE, D_MODEL), lambda i: (i, 0)),
                  pl.BlockSpec((ROW_TILE, HG_WIDTH), lambda i: (i, 0)),
                  pl.BlockSpec((ROW_TILE, NSA_WIDTH), lambda i: (i, 0)),
                  pl.BlockSpec((HG_WIDTH, D_MODEL), lambda i: (0, 0)),
                  pl.BlockSpec((NSA_WIDTH, D_MODEL), lambda i: (0, 0))],
        out_specs=pl.BlockSpec((ROW_TILE, D_MODEL), lambda i: (i, 0)),
        out_shape=jax.ShapeDtypeStruct((n, D_MODEL), F32),
        compiler_params=pltpu.CompilerParams(dimension_semantics=("parallel",),
                                             vmem_limit_bytes=VMEM_LIMIT),
        name="out_proj",
    )(x2, o_hg, o_nsa, w_a, w_b)


def _ffn_ple_kernel(h_ref, halo_ref, p_ref, gf_ref, wup_ref, cw_ref, cb_ref, wdn_ref,
                    gg_ref, wg_ref, wp_ref, gp_ref, o_ref, ug_scr, uu_scr, *, tiles_per_seq):
    h = h_ref[...]
    hn = _rms(h, gf_ref[...]).astype(BF16)
    first = (pl.program_id(0) % tiles_per_seq) == 0
    hh = _rms(halo_ref[...], gf_ref[...])
    hh = jnp.where(first, 0.0, hh).astype(BF16)
    rows = h.shape[0]
    acc = jnp.zeros((rows, D_MODEL), F32)
    for c in range(0, D_FF, FF_CHUNK):
        conv = []
        for scr, off in ((ug_scr, c), (uu_scr, D_FF + c)):
            w = wup_ref[:, off:off + FF_CHUNK]
            scr[0:HALO, :] = jnp.dot(hh, w, preferred_element_type=F32)
            scr[HALO:HALO + rows, :] = jnp.dot(hn, w, preferred_element_type=F32)
            cw = cw_ref[:, off:off + FF_CHUNK]
            conv.append(scr[HALO - 2:HALO - 2 + rows, :] * cw[0:1, :]
                        + scr[HALO - 1:HALO - 1 + rows, :] * cw[1:2, :]
                        + scr[HALO:HALO + rows, :] * cw[2:3, :]
                        + cb_ref[:, off:off + FF_CHUNK])
        gate, up = conv
        a = (gate * jax.nn.sigmoid(gate) * up).astype(BF16)
        acc += jnp.dot(a, wdn_ref[c:c + FF_CHUNK, :], preferred_element_type=F32)
    h2 = h + acc
    e = jnp.dot(p_ref[...].astype(BF16), wp_ref[...], preferred_element_type=F32)
    e = _rms(e, gp_ref[...])
    gate = jax.nn.sigmoid(jnp.dot(_rms(h2, gg_ref[...]).astype(BF16), wg_ref[...],
                                  preferred_element_type=F32))
    o_ref[...] = h2 + gate * e


def _ffn_ple(h1, p2, gf, w_up, conv_w, conv_b, w_down, gg, w_gate, w_ple, gp, seq):
    n = h1.shape[0]
    const = lambda i: (0, 0)
    halo_blocks = ROW_TILE // HALO
    return pl.pallas_call(
        functools.partial(_ffn_ple_kernel, tiles_per_seq=seq // ROW_TILE),
        grid=(n // ROW_TILE,),
        in_specs=[pl.BlockSpec((ROW_TILE, D_MODEL), lambda i: (i, 0)),
                  pl.BlockSpec((HALO, D_MODEL), lambda i: (jnp.maximum(i * halo_blocks - 1, 0), 0)),
                  pl.BlockSpec((ROW_TILE, PLE_DIM), lambda i: (i, 0)),
                  pl.BlockSpec((1, D_MODEL), const),
                  pl.BlockSpec((D_MODEL, 2 * D_FF), const, pipeline_mode=pl.Buffered(1)),
                  pl.BlockSpec((CONV_W, 2 * D_FF), const),
                  pl.BlockSpec((1, 2 * D_FF), const),
                  pl.BlockSpec((D_FF, D_MODEL), const, pipeline_mode=pl.Buffered(1)),
                  pl.BlockSpec((1, D_MODEL), const),
                  pl.BlockSpec((D_MODEL, D_MODEL), const, pipeline_mode=pl.Buffered(1)),
                  pl.BlockSpec((PLE_DIM, D_MODEL), const, pipeline_mode=pl.Buffered(1)),
                  pl.BlockSpec((1, D_MODEL), const)],
        out_specs=pl.BlockSpec((ROW_TILE, D_MODEL), lambda i: (i, 0)),
        out_shape=jax.ShapeDtypeStruct((n, D_MODEL), F32),
        scratch_shapes=[pltpu.VMEM((HALO + ROW_TILE, FF_CHUNK), F32),
                        pltpu.VMEM((HALO + ROW_TILE, FF_CHUNK), F32)],
        compiler_params=pltpu.CompilerParams(dimension_semantics=("parallel",),
                                             vmem_limit_bytes=VMEM_LIMIT),
        name="ffn_ple",
    )(h1, h1, p2, gf, w_up, conv_w, conv_b, w_down, gg, w_gate, w_ple, gp)


def _rms_norm(x, g):
    xf = x.astype(F32)
    y = xf * lax.rsqrt(jnp.mean(xf * xf, axis=-1, keepdims=True) + EPS)
    return (y * g.astype(F32)).astype(x.dtype)


def _partial_rope(x, pos):
    half = ROT_DIM // 2
    inv = ROPE_THETA ** (-jnp.arange(half, dtype=F32) / half)
    ang = pos.astype(F32)[:, None] * inv[None, :]
    cos = jnp.cos(ang)[:, None, :]
    sin = jnp.sin(ang)[:, None, :]
    xr = x[..., :ROT_DIM].astype(F32)
    x1, x2 = xr[..., :half], xr[..., half:]
    rot = jnp.concatenate([x1 * cos - x2 * sin, x2 * cos + x1 * sin], axis=-1)
    return jnp.concatenate([rot.astype(x.dtype), x[..., ROT_DIM:]], axis=-1)


def _hgrn2_jnp(q, f_pre, v, g, lb, norm_g):
    B, T, _ = q.shape
    dt = q.dtype
    n_ch = T // HG_CHUNK
    f = lb + (1.0 - lb) * jax.nn.sigmoid(f_pre.astype(F32))
    k = 1.0 - f
    logf = jnp.log(f)
    qf = jax.nn.silu(q.astype(F32)) * HG_KDIM ** -0.5

    def chunks(a, d):
        return a.reshape(B, n_ch, HG_CHUNK, HG_HEADS, d).transpose(1, 0, 3, 2, 4)

    qc = chunks(qf, HG_KDIM)
    kc = chunks(k, HG_KDIM)
    vc = chunks(v.astype(F32), HG_VDIM)
    bc = jnp.cumsum(chunks(logf, HG_KDIM), axis=3)
    causal = jnp.tril(jnp.ones((HG_CHUNK, HG_CHUNK), bool))[:, :, None]

    def step(S, inp):
        qt, kt, vt, bt = inp
        decay = jnp.exp(jnp.where(causal, bt[:, :, :, None, :] - bt[:, :, None, :, :], -jnp.inf))
        a = jnp.einsum('bhtk,bhsk,bhtsk->bhts', qt, kt, decay)
        o = (jnp.einsum('bhts,bhsv->bhtv', a, vt)
             + jnp.einsum('bhtk,bhkv->bhtv', qt * jnp.exp(bt), S))
        b_last = bt[:, :, -1:, :]
        S = (jnp.exp(b_last[:, :, 0, :])[..., None] * S
             + jnp.einsum('bhsk,bhsv->bhkv', kt * jnp.exp(b_last - bt), vt))
        return S, o

    S0 = jnp.zeros((B, HG_HEADS, HG_KDIM, HG_VDIM), F32)
    _, o = lax.scan(step, S0, (qc, kc, vc, bc))
    o = o.transpose(1, 0, 3, 2, 4).reshape(B, T, HG_HEADS, HG_VDIM)
    o = _rms_norm(o, norm_g) * jax.nn.silu(g.astype(F32)).reshape(B, T, HG_HEADS, HG_VDIM)
    return o.reshape(B, T, HG_WIDTH).astype(dt)


def _nsa_jnp(q, k_cmp, v_cmp, k_sel, v_sel, k_win, v_win, gate_pre,
             q_norm_g, k_norm_g, cmp_pe, cmp_w1, cmp_w2, out_norm_g):
    B, T, _ = q.shape
    dt = q.dtype
    pos = jnp.arange(T)
    scale = HEAD_DIM ** -0.5
    q = _partial_rope(_rms_norm(q.reshape(B, T, NSA_HEADS, HEAD_DIM), q_norm_g), pos)
    q = q.reshape(B, T, NSA_KV, NSA_GROUP, HEAD_DIM)

    def kv_heads(a):
        return a.reshape(B, T, NSA_KV, HEAD_DIM)

    n_cmp = (T - CMP_LEN) // CMP_STRIDE + 1
    cmp_start = jnp.arange(n_cmp) * CMP_STRIDE
    cmp_end = cmp_start + CMP_LEN - 1
    tok = cmp_start[:, None] + jnp.arange(CMP_LEN)[None, :]

    def compress(a, j):
        blk = kv_heads(a)[:, tok] + cmp_pe[j][None, None, :, None, :]
        blk = blk.transpose(0, 1, 3, 2, 4).reshape(B, n_cmp, NSA_KV, CMP_LEN * HEAD_DIM)
        return jax.nn.silu(blk @ cmp_w1[j]) @ cmp_w2[j]

    kc = _partial_rope(_rms_norm(compress(k_cmp, 0), k_norm_g[0]), cmp_end)
    vc = compress(v_cmp, 1)
    s = jnp.einsum('btghd,bcgd->bghtc', q, kc).astype(F32) * scale
    cmask = cmp_end[None, :] <= pos[:, None]
    p_cmp = jax.nn.softmax(jnp.where(cmask, s, NEG), axis=-1) * cmask
    o_cmp = jnp.einsum('bghtc,bcgd->btghd', p_cmp.astype(dt), vc)

    n_sel = T // SEL_BLOCK
    top = min(SEL_TOPK, n_sel)
    sel_start = jnp.arange(n_sel) * SEL_BLOCK
    overlap = jnp.clip(jnp.minimum(cmp_start[:, None] + CMP_LEN, sel_start[None, :] + SEL_BLOCK)
                       - jnp.maximum(cmp_start[:, None], sel_start[None, :]), 0, None)
    overlap = overlap.astype(F32) / CMP_LEN
    imp = jnp.einsum('bghtc,cn->btgn', p_cmp, overlap)
    cur = pos // SEL_BLOCK
    blk = jnp.arange(n_sel)
    forced = (blk[None, :] == 0) | (blk[None, :] == cur[:, None]) | (blk[None, :] == cur[:, None] - 1)
    visible = sel_start[None, :] <= pos[:, None]
    imp = jnp.where(visible[:, None, :], imp + jnp.where(forced, FORCE, 0.0)[:, None, :], NEG)
    _, sel_idx = lax.top_k(imp, top)

    def to_blocks(a):
        return a.reshape(B, n_sel, SEL_BLOCK, NSA_KV, HEAD_DIM).transpose(0, 3, 1, 2, 4)

    ks = to_blocks(_partial_rope(_rms_norm(kv_heads(k_sel), k_norm_g[1]), pos))
    vs = to_blocks(kv_heads(v_sel))
    pad = ((0, 0), (WINDOW, 0), (0, 0), (0, 0))
    kw = jnp.pad(_partial_rope(_rms_norm(kv_heads(k_win), k_norm_g[2]), pos), pad)
    vw = jnp.pad(kv_heads(v_win), pad)
    n_qb = T // Q_BLOCK
    q_blk = q.reshape(B, n_qb, Q_BLOCK, NSA_KV, NSA_GROUP, HEAD_DIM).swapaxes(0, 1)
    idx_blk = sel_idx.reshape(B, n_qb, Q_BLOCK, NSA_KV, top).swapaxes(0, 1)
    bi = jnp.arange(B)[:, None, None, None]
    gi = jnp.arange(NSA_KV)[None, None, :, None]
    span = Q_BLOCK + WINDOW

    def block_attend(args):
        j, qb, ib = args
        t = j * Q_BLOCK + jnp.arange(Q_BLOCK)
        ksb = ks[bi, gi, ib]
        vsb = vs[bi, gi, ib]
        ss = jnp.einsum('bqghd,bqgnld->bqghnl', qb, ksb).astype(F32) * scale
        kpos = ib[..., None] * SEL_BLOCK + jnp.arange(SEL_BLOCK)
        m = (kpos <= t[None, :, None, None, None])[:, :, :, None]
        ss = jnp.where(m, ss, NEG).reshape(B, Q_BLOCK, NSA_KV, NSA_GROUP, top * SEL_BLOCK)
        ps = jax.nn.softmax(ss, axis=-1).reshape(B, Q_BLOCK, NSA_KV, NSA_GROUP, top, SEL_BLOCK)
        o_s = jnp.einsum('bqghnl,bqgnld->bqghd', ps.astype(dt), vsb)
        kwb = lax.dynamic_slice_in_dim(kw, j * Q_BLOCK, span, axis=1)
        vwb = lax.dynamic_slice_in_dim(vw, j * Q_BLOCK, span, axis=1)
        spos = j * Q_BLOCK - WINDOW + jnp.arange(span)
        dist = t[:, None] - spos[None, :]
        wm = (spos[None, :] >= 0) & (dist >= 0) & (dist < WINDOW)
        sw = jnp.einsum('bqghd,bsgd->bghqs', qb, kwb).astype(F32) * scale
        pw = jax.nn.softmax(jnp.where(wm, sw, NEG), axis=-1)
        o_w = jnp.einsum('bghqs,bsgd->bqghd', pw.astype(dt), vwb)
        return o_s, o_w

    o_sel, o_win = lax.map(block_attend, (jnp.arange(n_qb), q_blk, idx_blk))
    o_sel = o_sel.swapaxes(0, 1).reshape(B, T, NSA_HEADS, HEAD_DIM)
    o_win = o_win.swapaxes(0, 1).reshape(B, T, NSA_HEADS, HEAD_DIM)
    o_cmp = o_cmp.reshape(B, T, NSA_HEADS, HEAD_DIM)

    gate = jax.nn.sigmoid(gate_pre.astype(F32)).reshape(B, T, 3, NSA_HEADS)[..., None].astype(dt)
    o = gate[:, :, 0] * o_cmp + gate[:, :, 1] * o_sel + gate[:, :, 2] * o_win
    return _rms_norm(o, out_norm_g).reshape(B, T, NSA_WIDTH)


def kernel(x, p, attn_norm_g, w_in, hg_lb_logits, hg_norm_g, nsa_q_norm_g, nsa_k_norm_g, cmp_pe,
           cmp_w1, cmp_w2, nsa_out_norm_g, w_out, ffn_norm_g, w_up, conv_w, conv_b, w_down,
           ple_gate_norm_g, w_ple_gate, w_ple, ple_norm_g):
    B, T, _ = x.shape
    n = B * T
    depth = w_in.shape[0]
    lb_all = jnp.cumsum(jax.nn.softmax(hg_lb_logits.astype(F32), axis=0), axis=0)
    split_points = []
    acc = 0
    for size in IN_SIZES[:-1]:
        acc += size
        split_points.append(acc)
    h = x.reshape(n, D_MODEL)
    for i in range(depth):
        w_pad = jnp.pad(w_in[i], ((0, 0), (0, IN_PAD - IN_TOTAL))).astype(BF16)
        proj = _in_proj(h, attn_norm_g[i][None, :], w_pad, F32)
        proj = proj[:, :IN_TOTAL].reshape(B, T, IN_TOTAL)
        (hq, hf, hi, hg, nq, nkc, nvc, nks, nvs, nkw, nvw, ngate) = jnp.split(proj, split_points, axis=-1)
        o_hg = _hgrn2_jnp(hq, hf, hi, hg, lb_all[i], hg_norm_g[i])
        o_nsa = _nsa_jnp(nq, nkc, nvc, nks, nvs, nkw, nvw, ngate, nsa_q_norm_g[i], nsa_k_norm_g[i],
                         cmp_pe[i], cmp_w1[i], cmp_w2[i], nsa_out_norm_g[i])
        w_o = w_out[i].astype(BF16)
        h1 = _out_proj(h, o_hg.reshape(n, HG_WIDTH).astype(BF16), o_nsa.reshape(n, NSA_WIDTH).astype(BF16),
                       w_o[:HG_WIDTH], w_o[HG_WIDTH:])
        h = _ffn_ple(h1, p[i].reshape(n, PLE_DIM), ffn_norm_g[i][None, :], w_up[i].astype(BF16),
                     conv_w[i], conv_b[i][None, :], w_down[i].astype(BF16),
                     ple_gate_norm_g[i][None, :], w_ple_gate[i].astype(BF16), w_ple[i].astype(BF16),
                     ple_norm_g[i][None, :], T)
    return h.reshape(B, T, D_MODEL)
```

```python
import functools

import numpy as np
import jax
import jax.numpy as jnp
from jax import lax
from jax.experimental import pallas as pl
from jax.experimental.pallas import tpu as pltpu

F32 = jnp.float32
BF16 = jnp.bfloat16

D_MODEL = 1024
PLE_DIM = 256
EPS = 1e-6
NEG = -1e30
FORCE = 1e6

HG_HEADS = 4
HG_KDIM = 128
HG_VDIM = 128
HG_WIDTH = HG_HEADS * HG_VDIM
HG_CHUNK = 64

NSA_HEADS = 8
NSA_KV = 2
NSA_GROUP = NSA_HEADS // NSA_KV
HEAD_DIM = 64
NSA_WIDTH = NSA_HEADS * HEAD_DIM
KV_WIDTH = NSA_KV * HEAD_DIM
CMP_LEN = 32
CMP_STRIDE = 16
SEL_BLOCK = 64
SEL_TOPK = 16
WINDOW = 512
ROT_DIM = HEAD_DIM // 4
ROPE_THETA = 500000.0
D_FF = 2816
CONV_W = 3

IN_SIZES = (HG_HEADS * HG_KDIM, HG_HEADS * HG_KDIM, HG_WIDTH, HG_WIDTH, NSA_WIDTH,
            KV_WIDTH, KV_WIDTH, KV_WIDTH, KV_WIDTH, KV_WIDTH, KV_WIDTH, 3 * NSA_HEADS)
IN_TOTAL = sum(IN_SIZES)
LANES = 128
IN_PAD = -(-IN_TOTAL // LANES) * LANES
VMEM_LIMIT = 56 * 1024 * 1024

ROW_TILE = 512
FF_CHUNK = 256
HALO = 8


def _rms(x, g):
    return x * lax.rsqrt(jnp.mean(x * x, axis=-1, keepdims=True) + EPS) * g


def _dot(a, b):
    return jnp.dot(a, b, preferred_element_type=F32)


def _dot_nt(a, b):
    return lax.dot_general(a, b, (((1,), (1,)), ((), ())), preferred_element_type=F32)


def _dot_tn(a, b):
    return lax.dot_general(a, b, (((0,), (0,)), ((), ())), preferred_element_type=F32)


def _split(x):
    hi = x.astype(BF16)
    return hi, (x - hi.astype(F32)).astype(BF16)


def _silu(x):
    return x * jax.nn.sigmoid(x)


def _in_proj_kernel(x_ref, g_ref, w_ref, o_ref, *, col_chunk):
    xb = _rms(x_ref[...], g_ref[...]).astype(BF16)
    for c in range(0, o_ref.shape[1], col_chunk):
        o_ref[:, c:c + col_chunk] = _dot(xb, w_ref[:, c:c + col_chunk]).astype(o_ref.dtype)


def _in_proj(x2, g, w_pad, out_dtype):
    n = x2.shape[0]
    width = w_pad.shape[1]
    return pl.pallas_call(
        functools.partial(_in_proj_kernel, col_chunk=width // 3),
        grid=(n // ROW_TILE,),
        in_specs=[pl.BlockSpec((ROW_TILE, D_MODEL), lambda i: (i, 0)),
                  pl.BlockSpec((1, D_MODEL), lambda i: (0, 0)),
                  pl.BlockSpec((D_MODEL, width), lambda i: (0, 0))],
        out_specs=pl.BlockSpec((ROW_TILE, width), lambda i: (i, 0)),
        out_shape=jax.ShapeDtypeStruct((n, width), out_dtype),
        compiler_params=pltpu.CompilerParams(dimension_semantics=("parallel",),
                                             vmem_limit_bytes=VMEM_LIMIT),
        name="in_proj",
    )(x2, g, w_pad)


def _out_proj_kernel(x_ref, a_ref, b_ref, wa_ref, wb_ref, o_ref):
    acc = _dot(a_ref[...], wa_ref[...])
    acc += _dot(b_ref[...], wb_ref[...])
    o_ref[...] = x_ref[...] + acc


def _out_proj(x2, o_hg, o_nsa, w_a, w_b):
    n = x2.shape[0]
    return pl.pallas_call(
        _out_proj_kernel,
        grid=(n // ROW_TILE,),
        in_specs=[pl.BlockSpec((ROW_TILE, D_MODEL), lambda i: (i, 0)),
                  pl.BlockSpec((ROW_TILE, HG_WIDTH), lambda i: (i, 0)),
                  pl.BlockSpec((ROW_TILE, NSA_WIDTH), lambda i: (i, 0)),
                  pl.BlockSpec((HG_WIDTH, D_MODEL), lambda i: (0, 0)),
                  pl.BlockSpec((NSA_WIDTH, D_MODEL), lambda i: (0, 0))],
        out_specs=pl.BlockSpec((ROW_TILE, D_MODEL), lambda i: (i, 0)),
        out_shape=jax.ShapeDtypeStruct((n, D_MODEL), F32),
        compiler_params=pltpu.CompilerParams(dimension_semantics=("parallel",),
                                             vmem_limit_bytes=VMEM_LIMIT),
        name="out_proj",
    )(x2, o_hg, o_nsa, w_a, w_b)


def _ffn_ple_kernel(h_ref, halo_ref, p_ref, gf_ref, wup_ref, cw_ref, cb_ref, wdn_ref,
                    gg_ref, wg_ref, wp_ref, gp_ref, o_ref, ug_scr, uu_scr, *, tiles_per_seq):
    h = h_ref[...]
    hn = _rms(h, gf_ref[...]).astype(BF16)
    first = (pl.program_id(0) % tiles_per_seq) == 0
    hh = _rms(halo_ref[...], gf_ref[...])
    hh = jnp.where(first, 0.0, hh).astype(BF16)
    rows = h.shape[0]
    acc = jnp.zeros((rows, D_MODEL), F32)
    for c in range(0, D_FF, FF_CHUNK):
        conv = []
        for scr, off in ((ug_scr, c), (uu_scr, D_FF + c)):
            w = wup_ref[:, off:off + FF_CHUNK]
            scr[0:HALO, :] = _dot(hh, w)
            scr[HALO:HALO + rows, :] = _dot(hn, w)
            cw = cw_ref[:, off:off + FF_CHUNK]
            conv.append(scr[HALO - 2:HALO - 2 + rows, :] * cw[0:1, :]
                        + scr[HALO - 1:HALO - 1 + rows, :] * cw[1:2, :]
                        + scr[HALO:HALO + rows, :] * cw[2:3, :]
                        + cb_ref[:, off:off + FF_CHUNK])
        gate, up = conv
        a = (_silu(gate) * up).astype(BF16)
        acc += _dot(a, wdn_ref[c:c + FF_CHUNK, :])
    h2 = h + acc
    e = _rms(_dot(p_ref[...].astype(BF16), wp_ref[...]), gp_ref[...])
    gate = jax.nn.sigmoid(_dot(_rms(h2, gg_ref[...]).astype(BF16), wg_ref[...]))
    o_ref[...] = h2 + gate * e


def _ffn_ple(h1, p2, gf, w_up, conv_w, conv_b, w_down, gg, w_gate, w_ple, gp, seq):
    n = h1.shape[0]
    const = lambda i: (0, 0)
    halo_blocks = ROW_TILE // HALO
    return pl.pallas_call(
        functools.partial(_ffn_ple_kernel, tiles_per_seq=seq // ROW_TILE),
        grid=(n // ROW_TILE,),
        in_specs=[pl.BlockSpec((ROW_TILE, D_MODEL), lambda i: (i, 0)),
                  pl.BlockSpec((HALO, D_MODEL), lambda i: (jnp.maximum(i * halo_blocks - 1, 0), 0)),
                  pl.BlockSpec((ROW_TILE, PLE_DIM), lambda i: (i, 0)),
                  pl.BlockSpec((1, D_MODEL), const),
                  pl.BlockSpec((D_MODEL, 2 * D_FF), const, pipeline_mode=pl.Buffered(1)),
                  pl.BlockSpec((CONV_W, 2 * D_FF), const),
                  pl.BlockSpec((1, 2 * D_FF), const),
                  pl.BlockSpec((D_FF, D_MODEL), const, pipeline_mode=pl.Buffered(1)),
                  pl.BlockSpec((1, D_MODEL), const),
                  pl.BlockSpec((D_MODEL, D_MODEL), const, pipeline_mode=pl.Buffered(1)),
                  pl.BlockSpec((PLE_DIM, D_MODEL), const, pipeline_mode=pl.Buffered(1)),
                  pl.BlockSpec((1, D_MODEL), const)],
        out_specs=pl.BlockSpec((ROW_TILE, D_MODEL), lambda i: (i, 0)),
        out_shape=jax.ShapeDtypeStruct((n, D_MODEL), F32),
        scratch_shapes=[pltpu.VMEM((HALO + ROW_TILE, FF_CHUNK), F32),
                        pltpu.VMEM((HALO + ROW_TILE, FF_CHUNK), F32)],
        compiler_params=pltpu.CompilerParams(dimension_semantics=("parallel",),
                                             vmem_limit_bytes=VMEM_LIMIT),
        name="ffn_ple",
    )(h1, h1, p2, gf, w_up, conv_w, conv_b, w_down, gg, w_gate, w_ple, gp)


HG_LEVELS = (32, 16, 8, 4, 2, 1)
HG_TB = 256


def _hgrn_level_matrices():
    c = HG_CHUNK
    r = np.arange(c)[:, None]
    u = np.arange(c)[None, :]
    rows = []
    for m in HG_LEVELS:
        r0 = (r // m) * m
        upper = (r & m) != 0
        rows.append(np.where(upper, (u >= r0) & (u <= r), (u >= r + 1) & (u <= r0 + m - 1)))
    rows.append(u <= r)
    rows.append(u > r)
    return np.concatenate(rows, 0).astype(np.float32)


def _hgrn_pair_masks():
    c = HG_CHUNK
    t = np.arange(c)[:, None]
    s = np.arange(c)[None, :]
    masks = [((t // (2 * m)) == (s // (2 * m))) & ((t & m) != 0) & ((s & m) == 0) for m in HG_LEVELS]
    masks.append(t == s)
    return np.stack(masks).astype(np.float32)


def _hgrn_kernel(q_ref, f_ref, v_ref, g_ref, lbl_ref, ng_ref, mall_ref, msk_ref, o_ref, st_ref,
                 *, layer, n_chunks):
    @pl.when(pl.program_id(1) == 0)
    def _():
        st_ref[...] = jnp.zeros_like(st_ref)

    lbl = lbl_ref[...]
    e = jnp.exp(lbl - jnp.max(lbl, axis=0, keepdims=True))
    sm = e / jnp.sum(e, axis=0, keepdims=True)
    lb = jnp.sum(sm[:layer + 1], axis=0, keepdims=True)
    mall = mall_ref[...]
    ng = ng_ref[...]
    c = HG_CHUNK
    nl = len(HG_LEVELS)

    def chunk(ci, carry):
        r0 = pl.multiple_of(ci * c, c)
        fg = lb + (1.0 - lb) * jax.nn.sigmoid(f_ref[pl.ds(r0, c), :].astype(F32))
        kk = 1.0 - fg
        hi, lo = _split(jnp.log(fg))
        ee = jnp.exp(_dot(mall, hi) + _dot(mall, lo))
        qf = _silu(q_ref[pl.ds(r0, c), :].astype(F32)) * HG_KDIM ** -0.5
        gg = _silu(g_ref[pl.ds(r0, c), :].astype(F32))
        for h in range(HG_HEADS):
            sl = slice(h * HG_KDIM, (h + 1) * HG_KDIM)
            qh, kh = qf[:, sl], kk[:, sl]
            a = msk_ref[nl] * _dot_nt(qh.astype(BF16), kh.astype(BF16))
            for j in range(nl):
                ej = ee[j * c:(j + 1) * c, sl]
                a += msk_ref[j] * _dot_nt((qh * ej).astype(BF16), (kh * ej).astype(BF16))
            vh = v_ref[pl.ds(r0, c), sl]
            eb = ee[nl * c:(nl + 1) * c, sl]
            st = st_ref[h]
            o = _dot(a.astype(BF16), vh) + _dot_nt((qh * eb).astype(BF16), st.astype(BF16))
            kd = (kh * ee[(nl + 1) * c:(nl + 2) * c, sl]).astype(BF16)
            st_ref[h] = eb[c - 1:c, :] * st + _dot_tn(vh, kd)
            on = o * lax.rsqrt(jnp.mean(o * o, axis=-1, keepdims=True) + EPS) * ng
            o_ref[pl.ds(r0, c), sl] = (on * gg[:, sl]).astype(o_ref.dtype)
        return carry

    lax.fori_loop(0, n_chunks, chunk, 0)


def _hgrn(proj, lb_logits, norm_g, layer, batch, seq):
    nt = seq // HG_TB
    w = HG_HEADS * HG_KDIM
    col = lambda k: (lambda b, t: (b * nt + t, k))
    const2 = lambda b, t: (0, 0)
    return pl.pallas_call(
        functools.partial(_hgrn_kernel, layer=layer, n_chunks=HG_TB // HG_CHUNK),
        grid=(batch, nt),
        in_specs=[pl.BlockSpec((HG_TB, w), col(0)), pl.BlockSpec((HG_TB, w), col(1)),
                  pl.BlockSpec((HG_TB, w), col(2)), pl.BlockSpec((HG_TB, w), col(3)),
                  pl.BlockSpec(lb_logits.shape, const2),
                  pl.BlockSpec((1, HG_VDIM), const2),
                  pl.BlockSpec(((len(HG_LEVELS) + 2) * HG_CHUNK, HG_CHUNK), const2),
                  pl.BlockSpec((len(HG_LEVELS) + 1, HG_CHUNK, HG_CHUNK), lambda b, t: (0, 0, 0))],
        out_specs=pl.BlockSpec((HG_TB, w), lambda b, t: (b * nt + t, 0)),
        out_shape=jax.ShapeDtypeStruct((batch * seq, HG_WIDTH), BF16),
        scratch_shapes=[pltpu.VMEM((HG_HEADS, HG_VDIM, HG_KDIM), F32)],
        compiler_params=pltpu.CompilerParams(dimension_semantics=("parallel", "arbitrary"),
                                             vmem_limit_bytes=VMEM_LIMIT),
        name="hgrn2",
    )(proj, proj, proj, proj, lb_logits, norm_g,
      jnp.asarray(_hgrn_level_matrices(), BF16), jnp.asarray(_hgrn_pair_masks(), F32))


NSA_TB = 512
COL_NQ = 2048 // LANES
COL_KC, COL_VC, COL_KS, COL_VS, COL_KW, COL_VW, COL_GATE = (COL_NQ + 4 + k for k in range(7))


def _rope_tables(pos):
    half = ROT_DIM // 2
    inv = ROPE_THETA ** (-jnp.arange(half, dtype=F32) / half)
    ang = pos.astype(F32)[:, None] * inv[None, :]
    n = pos.shape[0]
    pad = HEAD_DIM - ROT_DIM
    cos = jnp.concatenate([jnp.cos(ang), jnp.cos(ang), jnp.ones((n, pad), F32)], axis=1)
    sin = jnp.concatenate([jnp.sin(ang), jnp.sin(ang), jnp.zeros((n, pad), F32)], axis=1)
    return cos, sin


def _rot_matrix(heads):
    half = ROT_DIM // 2
    r = np.zeros((HEAD_DIM, HEAD_DIM), np.float32)
    for j in range(half):
        r[j + half, j] = -1.0
        r[j, j + half] = 1.0
    return np.kron(np.eye(heads, dtype=np.float32), r)


def _head_ones(heads):
    return np.kron(np.eye(heads, dtype=np.float32), np.ones((HEAD_DIM, HEAD_DIM), np.float32))


def _norm_rope(x, g, cos, sin, rot, ones):
    hi, lo = _split(x * x)
    ssq = _dot(hi, ones) + _dot(lo, ones)
    xn = x * lax.rsqrt(ssq * (1.0 / HEAD_DIM) + EPS) * g
    return xn * cos + _dot(xn.astype(BF16), rot) * sin


def _nsa_prep_kernel(q_ref, ks_ref, vs_ref, kw_ref, vw_ref, cos_ref, sin_ref, gq_ref, gk_ref,
                     rot_ref, ones_ref, qo_ref, kso_ref, vso_ref, kwo_ref, vwo_ref):
    cos2, sin2 = cos_ref[...], sin_ref[...]
    cos8 = jnp.concatenate([cos2] * (NSA_HEADS // 2), axis=1)
    sin8 = jnp.concatenate([sin2] * (NSA_HEADS // 2), axis=1)
    rot, ones = rot_ref[...], ones_ref[...]
    q = _norm_rope(q_ref[...].astype(F32), gq_ref[...], cos8, sin8, rot, ones) * HEAD_DIM ** -0.5
    for h in range(NSA_HEADS):
        qo_ref[0, h] = q[:, h * HEAD_DIM:(h + 1) * HEAD_DIM].astype(qo_ref.dtype)
    lane = lax.broadcasted_iota(jnp.int32, (q.shape[0], 2 * HEAD_DIM), 1)
    r2, o2 = rot[:2 * HEAD_DIM, :2 * HEAD_DIM], ones[:2 * HEAD_DIM, :2 * HEAD_DIM]
    for j, (k_ref, v_ref, ko_ref, vo_ref) in enumerate(((ks_ref, vs_ref, kso_ref, vso_ref),
                                                         (kw_ref, vw_ref, kwo_ref, vwo_ref))):
        k = _norm_rope(k_ref[...].astype(F32), gk_ref[j:j + 1, :], cos2, sin2, r2, o2)
        v = v_ref[...].astype(F32)
        for g in range(NSA_KV):
            kg = k if g == 0 else pltpu.roll(k, HEAD_DIM, 1)
            vg = v if g == 0 else pltpu.roll(v, HEAD_DIM, 1)
            ko_ref[0, g] = kg[:, :HEAD_DIM].astype(ko_ref.dtype)
            vo_ref[0, g] = jnp.where(lane < HEAD_DIM, vg, 1.0).astype(vo_ref.dtype)


def _nsa_prep(proj, q_norm_g, k_norm_g, batch, seq):
    nt = seq // NSA_TB
    cos, sin = _rope_tables(jnp.arange(seq))
    cos2 = jnp.concatenate([cos, cos], axis=1)
    sin2 = jnp.concatenate([sin, sin], axis=1)
    gq = jnp.tile(q_norm_g, NSA_HEADS)[None, :]
    gk = jnp.tile(k_norm_g[1:3], (1, NSA_KV))
    col = lambda k: (lambda b, t: (b * nt + t, k))
    const2 = lambda b, t: (0, 0)
    head_major = lambda heads, d: pl.BlockSpec((1, heads, NSA_TB, d), lambda b, t: (b, 0, t, 0))
    kv_shape = lambda d: jax.ShapeDtypeStruct((batch, NSA_KV, seq, d), BF16)
    return pl.pallas_call(
        _nsa_prep_kernel,
        grid=(batch, nt),
        in_specs=[pl.BlockSpec((NSA_TB, NSA_WIDTH), col(COL_NQ * LANES // NSA_WIDTH)),
                  pl.BlockSpec((NSA_TB, KV_WIDTH), col(COL_KS)),
                  pl.BlockSpec((NSA_TB, KV_WIDTH), col(COL_VS)),
                  pl.BlockSpec((NSA_TB, KV_WIDTH), col(COL_KW)),
                  pl.BlockSpec((NSA_TB, KV_WIDTH), col(COL_VW)),
                  pl.BlockSpec((NSA_TB, KV_WIDTH), lambda b, t: (t, 0)),
                  pl.BlockSpec((NSA_TB, KV_WIDTH), lambda b, t: (t, 0)),
                  pl.BlockSpec((1, NSA_WIDTH), const2),
                  pl.BlockSpec((2, KV_WIDTH), const2),
                  pl.BlockSpec((NSA_WIDTH, NSA_WIDTH), const2),
                  pl.BlockSpec((NSA_WIDTH, NSA_WIDTH), const2)],
        out_specs=[head_major(NSA_HEADS, HEAD_DIM), head_major(NSA_KV, HEAD_DIM),
                   head_major(NSA_KV, 2 * HEAD_DIM), head_major(NSA_KV, HEAD_DIM),
                   head_major(NSA_KV, 2 * HEAD_DIM)],
        out_shape=[jax.ShapeDtypeStruct((batch, NSA_HEADS, seq, HEAD_DIM), BF16),
                   kv_shape(HEAD_DIM), kv_shape(2 * HEAD_DIM), kv_shape(HEAD_DIM), kv_shape(2 * HEAD_DIM)],
        compiler_params=pltpu.CompilerParams(dimension_semantics=("parallel", "parallel"),
                                             vmem_limit_bytes=VMEM_LIMIT),
        name="nsa_prep",
    )(proj, proj, proj, proj, proj, cos2, sin2, gq, gk,
      jnp.asarray(_rot_matrix(NSA_HEADS), BF16), jnp.asarray(_head_ones(NSA_HEADS), BF16))


def _compress_kernel(xk_ref, xv_ref, pea_ref, peb_ref, w1_ref, w2_ref, gk_ref, cos_ref, sin_ref,
                     rot_ref, kc_ref, vc_ref, shift_scr):
    nc = xk_ref.shape[2]
    half = w1_ref.shape[1] // 2
    for j, x_ref in enumerate((xk_ref, xv_ref)):
        x = x_ref[0, 0].astype(F32)
        u = _dot((x + pea_ref[j:j + 1, :]).astype(BF16), w1_ref[j, :half, :])
        v = _dot((x + peb_ref[j:j + 1, :]).astype(BF16), w1_ref[j, half:, :])
        shift_scr[0:nc, :] = v
        shift_scr[nc:nc + HALO, :] = jnp.zeros((HALO, v.shape[1]), F32)
        hid = _silu(u + shift_scr[1:nc + 1, :])
        out = _dot(hid.astype(BF16), w2_ref[j])
        if j == 0:
            xn = _rms(out, gk_ref[...])
            out = xn * cos_ref[...] + _dot(xn.astype(BF16), rot_ref[...]) * sin_ref[...]
            kc_ref[0, 0] = out.astype(kc_ref.dtype)
        else:
            vc_ref[0, 0] = jnp.concatenate([out, jnp.zeros_like(out)], axis=1).astype(vc_ref.dtype)


def _compress(xk, xv, cmp_pe, cmp_w1, cmp_w2, gk0, batch, seq):
    nc = seq // CMP_STRIDE
    grp = CMP_STRIDE * HEAD_DIM
    cos, sin = _rope_tables(jnp.arange(nc) * CMP_STRIDE + CMP_LEN - 1)
    pea = cmp_pe[:, :CMP_STRIDE].reshape(2, grp)
    peb = cmp_pe[:, CMP_STRIDE:].reshape(2, grp)
    blk = pl.BlockSpec((1, 1, nc, grp), lambda b, g: (b, g, 0, 0))
    const2 = lambda b, g: (0, 0)
    const3 = lambda b, g: (0, 0, 0)
    out_blk = lambda d: pl.BlockSpec((1, 1, nc, d), lambda b, g: (b, g, 0, 0))
    out_sds = lambda d: jax.ShapeDtypeStruct((batch, NSA_KV, nc, d), BF16)
    return pl.pallas_call(
        _compress_kernel,
        grid=(batch, NSA_KV),
        in_specs=[blk, blk,
                  pl.BlockSpec((2, grp), const2), pl.BlockSpec((2, grp), const2),
                  pl.BlockSpec(cmp_w1.shape, const3), pl.BlockSpec(cmp_w2.shape, const3),
                  pl.BlockSpec((1, HEAD_DIM), const2),
                  pl.BlockSpec((nc, HEAD_DIM), const2), pl.BlockSpec((nc, HEAD_DIM), const2),
                  pl.BlockSpec((HEAD_DIM, HEAD_DIM), const2)],
        out_specs=[out_blk(HEAD_DIM), out_blk(LANES)],
        out_shape=[out_sds(HEAD_DIM), out_sds(LANES)],
        scratch_shapes=[pltpu.VMEM((nc + HALO, cmp_w1.shape[2]), F32)],
        compiler_params=pltpu.CompilerParams(dimension_semantics=("parallel", "parallel"),
                                             vmem_limit_bytes=VMEM_LIMIT),
        name="nsa_compress",
    )(xk, xv, pea, peb, cmp_w1.astype(BF16), cmp_w2.astype(BF16), gk0[None, :], cos, sin,
      jnp.asarray(_rot_matrix(1), BF16))


NSA_TQ = 256


def _overlap_t(seq):
    ns, nc = seq // SEL_BLOCK, seq // CMP_STRIDE
    cs = np.arange(nc)[None, :] * CMP_STRIDE
    ss = np.arange(ns)[:, None] * SEL_BLOCK
    ov = np.clip(np.minimum(cs + CMP_LEN, ss + SEL_BLOCK) - np.maximum(cs, ss), 0, None)
    return (ov / CMP_LEN).astype(np.float32)


def _cmpsel_kernel(q_ref, kc_ref, vc_ref, ovt_ref, ocmp_ref, sel_ref, *, top):
    tq, nc = q_ref.shape[2], kc_ref.shape[2]
    ns = ovt_ref.shape[0]
    t0 = pl.program_id(1) * tq
    row_t = t0 + lax.broadcasted_iota(jnp.int32, (tq, nc), 0)
    col_c = lax.broadcasted_iota(jnp.int32, (tq, nc), 1)
    cmask = col_c * CMP_STRIDE + (CMP_LEN - 1) <= row_t
    n_i = lax.broadcasted_iota(jnp.int32, (ns, tq), 0)
    t_i = t0 + lax.broadcasted_iota(jnp.int32, (ns, tq), 1)
    cur = t_i // SEL_BLOCK
    forced = jnp.where(n_i == 0, FORCE, jnp.where(n_i == cur, FORCE, jnp.where(n_i == cur - 1, FORCE, 0.0)))
    visible = n_i * SEL_BLOCK <= t_i
    ovt = ovt_ref[...]
    pairs = []
    for g in range(NSA_KV):
        kc, vc = kc_ref[0, g], vc_ref[0, g]
        psum = jnp.zeros((tq, nc), F32)
        outs = []
        for h in range(NSA_GROUP):
            s = jnp.where(cmask, _dot_nt(q_ref[0, g * NSA_GROUP + h], kc), NEG)
            e = jnp.exp(s - jnp.max(s, axis=-1, keepdims=True))
            p = jnp.where(cmask, e * (1.0 / jnp.sum(e, axis=-1, keepdims=True)), 0.0)
            outs.append(_dot(p.astype(BF16), vc))
            psum += p
        pairs += [outs[0] + pltpu.roll(outs[1], HEAD_DIM, 1), outs[2] + pltpu.roll(outs[3], HEAD_DIM, 1)]
        hi, lo = _split(psum.T)
        val = jnp.where(visible, _dot(ovt, hi) + _dot(ovt, lo) + forced, NEG)
        cnt = jnp.zeros((ns, tq), F32)
        for m in range(ns):
            row = val[m:m + 1, :]
            ge = jnp.where(row >= val, 1.0, 0.0)
            gt = jnp.where(row > val, 1.0, 0.0)
            cnt += jnp.where(n_i > m, ge, gt)
        sel_t = jnp.where(cnt < top, 1.0, 0.0)
        if ns < LANES:
            sel_t = jnp.concatenate([sel_t, jnp.zeros((LANES - ns, tq), F32)], axis=0)
        sel_ref[0, g] = sel_t.T.astype(sel_ref.dtype)
    ocmp_ref[...] = jnp.concatenate(pairs, axis=1).astype(ocmp_ref.dtype)


def _cmpsel(qh, kc, vc, batch, seq):
    nt = seq // NSA_TQ
    nc, ns = seq // CMP_STRIDE, seq // SEL_BLOCK
    assert ns <= LANES
    return pl.pallas_call(
        functools.partial(_cmpsel_kernel, top=min(SEL_TOPK, ns)),
        grid=(batch, nt),
        in_specs=[pl.BlockSpec((1, NSA_HEADS, NSA_TQ, HEAD_DIM), lambda b, t: (b, 0, t, 0)),
                  pl.BlockSpec((1, NSA_KV, nc, HEAD_DIM), lambda b, t: (b, 0, 0, 0)),
                  pl.BlockSpec((1, NSA_KV, nc, LANES), lambda b, t: (b, 0, 0, 0)),
                  pl.BlockSpec((ns, nc), lambda b, t: (0, 0))],
        out_specs=[pl.BlockSpec((NSA_TQ, NSA_WIDTH), lambda b, t: (b * nt + t, 0)),
                   pl.BlockSpec((1, NSA_KV, NSA_TQ, LANES), lambda b, t: (b, 0, t, 0))],
        out_shape=[jax.ShapeDtypeStruct((batch * seq, NSA_WIDTH), BF16),
                   jax.ShapeDtypeStruct((batch, NSA_KV, seq, LANES), BF16)],
        compiler_params=pltpu.CompilerParams(dimension_semantics=("parallel", "parallel"),
                                             vmem_limit_bytes=VMEM_LIMIT),
        name="nsa_cmpsel",
    )(qh, kc, vc, jnp.asarray(_overlap_t(seq), BF16))


def _block_expand(seq):
    n = np.arange(LANES)[:, None]
    s = np.arange(seq)[None, :]
    return (s // SEL_BLOCK == n).astype(np.float32)


def _gate_expand():
    e = np.zeros((LANES, 3 * NSA_WIDTH), np.float32)
    for k in range(3):
        for h in range(NSA_HEADS):
            e[k * NSA_HEADS + h, k * NSA_WIDTH + h * HEAD_DIM:k * NSA_WIDTH + (h + 1) * HEAD_DIM] = 1.0
    return e


def _attn_kernel(q_ref, ks_ref, vs_ref, kw_ref, vw_ref, sel_ref, emat_ref, ocmp_ref, gate_ref,
                 gexp_ref, ones_ref, gout_ref, o_ref, m_scr, acc_scr):
    tq = q_ref.shape[2]
    i = pl.program_id(1)
    r_i = lax.broadcasted_iota(jnp.int32, (tq, tq), 0)
    c_i = lax.broadcasted_iota(jnp.int32, (tq, tq), 1)
    causal = c_i <= r_i
    hw = tq // 2

    m_scr[...] = jnp.full(m_scr.shape, NEG, F32)
    acc_scr[...] = jnp.zeros_like(acc_scr)

    def attend(slot, q, k, v, keep):
        s = _dot_nt(q, k)
        if keep is not None:
            s = jnp.where(keep, s, NEG)
        s0, s1 = s[:, :hw], s[:, hw:]
        m_old = m_scr[slot]
        m_new = jnp.maximum(m_old, jnp.max(jnp.maximum(s0, s1), axis=-1, keepdims=True))
        alpha = jnp.exp(m_old - m_new)
        pv = _dot(jnp.exp(s0 - m_new).astype(BF16), v[:hw]) + _dot(jnp.exp(s1 - m_new).astype(BF16), v[hw:])
        acc_scr[slot] = alpha * acc_scr[slot] + pv
        m_scr[slot] = m_new

    for g in range(NSA_KV):
        sel = sel_ref[0, g]

        def sel_tile(j, diag):
            k0 = pl.multiple_of(j * tq, tq)
            member = _dot(sel, emat_ref[:, pl.ds(k0, tq)])
            keep = (jnp.where(causal, member, 0.0) if diag else member) > 0.5
            k, v = ks_ref[0, g, pl.ds(k0, tq), :], vs_ref[0, g, pl.ds(k0, tq), :]
            for h in range(NSA_GROUP):
                attend(g * NSA_GROUP + h, q_ref[0, g * NSA_GROUP + h], k, v, keep)

        def body(j, carry):
            sel_tile(j, False)
            return carry

        lax.fori_loop(0, i, body, 0)
        sel_tile(i, True)

        def win_tile(j, keep):
            k0 = pl.multiple_of(j * tq, tq)
            k, v = kw_ref[0, g, pl.ds(k0, tq), :], vw_ref[0, g, pl.ds(k0, tq), :]
            for h in range(NSA_GROUP):
                attend(NSA_HEADS + g * NSA_GROUP + h, q_ref[0, g * NSA_GROUP + h], k, v, keep)

        assert WINDOW == 2 * tq

        @pl.when(i >= 2)
        def _():
            win_tile(i - 2, c_i > r_i)

        @pl.when(i >= 1)
        def _():
            win_tile(i - 1, None)

        win_tile(i, causal)

    lane = lax.broadcasted_iota(jnp.int32, (tq, LANES), 1)

    def head_pairs(base):
        tiles = []
        for hp in range(NSA_HEADS // 2):
            pair = []
            for hh in (2 * hp, 2 * hp + 1):
                a = acc_scr[base + hh]
                pair.append(jnp.where(lane < HEAD_DIM, a / pltpu.roll(a, HEAD_DIM, 1), 0.0))
            tiles.append(pair[0] + pltpu.roll(pair[1], HEAD_DIM, 1))
        return jnp.concatenate(tiles, axis=1)

    gate = jax.nn.sigmoid(_dot(gate_ref[...], gexp_ref[...]))
    o = (gate[:, :NSA_WIDTH] * ocmp_ref[...].astype(F32)
         + gate[:, NSA_WIDTH:2 * NSA_WIDTH] * head_pairs(0)
         + gate[:, 2 * NSA_WIDTH:] * head_pairs(NSA_HEADS))
    hi, lo = _split(o * o)
    ssq = _dot(hi, ones_ref[...]) + _dot(lo, ones_ref[...])
    o_ref[...] = (o * lax.rsqrt(ssq * (1.0 / HEAD_DIM) + EPS) * gout_ref[...]).astype(o_ref.dtype)


def _attn(qh, ks, vs, kw, vw, sel, ocmp, proj, out_norm_g, batch, seq):
    nt = seq // NSA_TQ
    full = lambda d: pl.BlockSpec((1, NSA_KV, seq, d), lambda b, t: (b, 0, 0, 0))
    tile = lambda heads, d: pl.BlockSpec((1, heads, NSA_TQ, d), lambda b, t: (b, 0, t, 0))
    return pl.pallas_call(
        _attn_kernel,
        grid=(batch, nt),
        in_specs=[tile(NSA_HEADS, HEAD_DIM), full(HEAD_DIM), full(2 * HEAD_DIM), full(HEAD_DIM),
                  full(2 * HEAD_DIM), tile(NSA_KV, LANES),
                  pl.BlockSpec((LANES, seq), lambda b, t: (0, 0)),
                  pl.BlockSpec((NSA_TQ, NSA_WIDTH), lambda b, t: (b * nt + t, 0)),
                  pl.BlockSpec((NSA_TQ, LANES), lambda b, t: (b * nt + t, COL_GATE)),
                  pl.BlockSpec((LANES, 3 * NSA_WIDTH), lambda b, t: (0, 0)),
                  pl.BlockSpec((NSA_WIDTH, NSA_WIDTH), lambda b, t: (0, 0)),
                  pl.BlockSpec((1, NSA_WIDTH), lambda b, t: (0, 0))],
        out_specs=pl.BlockSpec((NSA_TQ, NSA_WIDTH), lambda b, t: (b * nt + t, 0)),
        out_shape=jax.ShapeDtypeStruct((batch * seq, NSA_WIDTH), BF16),
        scratch_shapes=[pltpu.VMEM((2 * NSA_HEADS, NSA_TQ, LANES), F32),
                        pltpu.VMEM((2 * NSA_HEADS, NSA_TQ, LANES), F32)],
        compiler_params=pltpu.CompilerParams(dimension_semantics=("parallel", "parallel"),
                                             vmem_limit_bytes=VMEM_LIMIT),
        name="nsa_attn",
    )(qh, ks, vs, kw, vw, sel, jnp.asarray(_block_expand(seq), BF16), ocmp, proj,
      jnp.asarray(_gate_expand(), BF16), jnp.asarray(_head_ones(NSA_HEADS), BF16),
      jnp.tile(out_norm_g, NSA_HEADS)[None, :])


def _nsa(proj, q_norm_g, k_norm_g, cmp_pe, cmp_w1, cmp_w2, out_norm_g, batch, seq):
    qh, ks, vs, kw, vw = _nsa_prep(proj, q_norm_g, k_norm_g, batch, seq)

    def groups(col):
        a = proj[:, col * LANES:(col + 1) * LANES].reshape(batch, seq, NSA_KV, HEAD_DIM)
        return a.transpose(0, 2, 1, 3).reshape(batch, NSA_KV, seq // CMP_STRIDE, CMP_STRIDE * HEAD_DIM)

    kc, vc = _compress(groups(COL_KC), groups(COL_VC), cmp_pe, cmp_w1, cmp_w2, k_norm_g[0], batch, seq)
    ocmp, sel = _cmpsel(qh, kc, vc, batch, seq)
    return _attn(qh, ks, vs, kw, vw, sel, ocmp, proj, out_norm_g, batch, seq)


def kernel(x, p, attn_norm_g, w_in, hg_lb_logits, hg_norm_g, nsa_q_norm_g, nsa_k_norm_g, cmp_pe,
           cmp_w1, cmp_w2, nsa_out_norm_g, w_out, ffn_norm_g, w_up, conv_w, conv_b, w_down,
           ple_gate_norm_g, w_ple_gate, w_ple, ple_norm_g):
    B, T, _ = x.shape
    n = B * T
    h = x.reshape(n, D_MODEL)
    for i in range(w_in.shape[0]):
        w_pad = jnp.pad(w_in[i], ((0, 0), (0, IN_PAD - IN_TOTAL))).astype(BF16)
        proj = _in_proj(h, attn_norm_g[i][None, :], w_pad, BF16)
        o_hg = _hgrn(proj, hg_lb_logits.astype(F32), hg_norm_g[i][None, :], i, B, T)
        o_nsa = _nsa(proj, nsa_q_norm_g[i], nsa_k_norm_g[i], cmp_pe[i], cmp_w1[i], cmp_w2[i],
                     nsa_out_norm_g[i], B, T)
        w_o = w_out[i].astype(BF16)
        h1 = _out_proj(h, o_hg, o_nsa, w_o[:HG_WIDTH], w_o[HG_WIDTH:])
        h = _ffn_ple(h1, p[i].reshape(n, PLE_DIM), ffn_norm_g[i][None, :], w_up[i].astype(BF16),
                     conv_w[i], conv_b[i][None, :], w_down[i].astype(BF16),
                     ple_gate_norm_g[i][None, :], w_ple_gate[i].astype(BF16), w_ple[i].astype(BF16),
                     ple_norm_g[i][None, :], T)
    return h.reshape(B, T, D_MODEL)
```

```python
import functools

import numpy as np
import jax
import jax.numpy as jnp
from jax import lax
from jax.experimental import pallas as pl
from jax.experimental.pallas import tpu as pltpu

F32 = jnp.float32
BF16 = jnp.bfloat16

D_MODEL = 1024
PLE_DIM = 256
EPS = 1e-6
NEG = -1e30
FORCE = 1e6

HG_HEADS = 4
HG_KDIM = 128
HG_VDIM = 128
HG_WIDTH = HG_HEADS * HG_VDIM
HG_CHUNK = 64

NSA_HEADS = 8
NSA_KV = 2
NSA_GROUP = NSA_HEADS // NSA_KV
HEAD_DIM = 64
NSA_WIDTH = NSA_HEADS * HEAD_DIM
KV_WIDTH = NSA_KV * HEAD_DIM
CMP_LEN = 32
CMP_STRIDE = 16
SEL_BLOCK = 64
SEL_TOPK = 16
WINDOW = 512
ROT_DIM = HEAD_DIM // 4
ROPE_THETA = 500000.0
D_FF = 2816
CONV_W = 3

IN_SIZES = (HG_HEADS * HG_KDIM, HG_HEADS * HG_KDIM, HG_WIDTH, HG_WIDTH, NSA_WIDTH,
            KV_WIDTH, KV_WIDTH, KV_WIDTH, KV_WIDTH, KV_WIDTH, KV_WIDTH, 3 * NSA_HEADS)
IN_TOTAL = sum(IN_SIZES)
LANES = 128
IN_PAD = -(-IN_TOTAL // LANES) * LANES
VMEM_LIMIT = 56 * 1024 * 1024

ROW_TILE = 512
FF_CHUNK = 256
SUBLANES = 8
HALO = SUBLANES


def _rms(x, g):
    return x * lax.rsqrt(jnp.mean(x * x, axis=-1, keepdims=True) + EPS) * g


def _dot(a, b):
    return jnp.dot(a, b, preferred_element_type=F32)


def _dot_nt(a, b):
    return lax.dot_general(a, b, (((1,), (1,)), ((), ())), preferred_element_type=F32)


def _dot_tn(a, b):
    return lax.dot_general(a, b, (((0,), (0,)), ((), ())), preferred_element_type=F32)


def _split(x):
    hi = x.astype(BF16)
    return hi, (x - hi.astype(F32)).astype(BF16)


def _silu(x):
    hx = 0.5 * x
    return hx + hx * jnp.tanh(hx)


def _in_proj_kernel(x_ref, g_ref, w_ref, o_ref, *, col_chunk):
    xb = _rms(x_ref[...], g_ref[...]).astype(BF16)
    for c in range(0, o_ref.shape[1], col_chunk):
        o_ref[:, c:c + col_chunk] = _dot(xb, w_ref[:, c:c + col_chunk]).astype(o_ref.dtype)


def _in_proj(x2, g, w_pad, out_dtype):
    n = x2.shape[0]
    width = w_pad.shape[1]
    return pl.pallas_call(
        functools.partial(_in_proj_kernel, col_chunk=width // 3),
        grid=(n // ROW_TILE,),
        in_specs=[pl.BlockSpec((ROW_TILE, D_MODEL), lambda i: (i, 0)),
                  pl.BlockSpec((1, D_MODEL), lambda i: (0, 0)),
                  pl.BlockSpec((D_MODEL, width), lambda i: (0, 0))],
        out_specs=pl.BlockSpec((ROW_TILE, width), lambda i: (i, 0)),
        out_shape=jax.ShapeDtypeStruct((n, width), out_dtype),
        compiler_params=pltpu.CompilerParams(dimension_semantics=("parallel",),
                                             vmem_limit_bytes=VMEM_LIMIT),
        name="in_proj",
    )(x2, g, w_pad)


def _out_proj_kernel(x_ref, a_ref, b_ref, wa_ref, wb_ref, o_ref):
    acc = _dot(a_ref[...], wa_ref[...])
    acc += _dot(b_ref[...], wb_ref[...])
    o_ref[...] = x_ref[...] + acc


def _out_proj(x2, o_hg, o_nsa, w_a, w_b):
    n = x2.shape[0]
    return pl.pallas_call(
        _out_proj_kernel,
        grid=(n // ROW_TILE,),
        in_specs=[pl.BlockSpec((ROW_TILE, D_MODEL), lambda i: (i, 0)),
                  pl.BlockSpec((ROW_TILE, HG_WIDTH), lambda i: (i, 0)),
                  pl.BlockSpec((ROW_TILE, NSA_WIDTH), lambda i: (i, 0)),
                  pl.BlockSpec((HG_WIDTH, D_MODEL), lambda i: (0, 0)),
                  pl.BlockSpec((NSA_WIDTH, D_MODEL), lambda i: (0, 0))],
        out_specs=pl.BlockSpec((ROW_TILE, D_MODEL), lambda i: (i, 0)),
        out_shape=jax.ShapeDtypeStruct((n, D_MODEL), F32),
        compiler_params=pltpu.CompilerParams(dimension_semantics=("parallel",),
                                             vmem_limit_bytes=VMEM_LIMIT),
        name="out_proj",
    )(x2, o_hg, o_nsa, w_a, w_b)


def _ffn_ple_kernel(h_ref, halo_ref, p_ref, gf_ref, wup_ref, cw_ref, cb_ref, wdn_ref,
                    gg_ref, wg_ref, wp_ref, gp_ref, o_ref, ug_scr, uu_scr, act_scr, *, tiles_per_seq):
    h = h_ref[...]
    hn = _rms(h, gf_ref[...]).astype(BF16)
    first = (pl.program_id(0) % tiles_per_seq) == 0
    hh = _rms(halo_ref[...], gf_ref[...])
    hh = jnp.where(first, 0.0, hh).astype(BF16)
    rows = h.shape[0]
    for c in range(0, D_FF, FF_CHUNK):
        conv = []
        for scr, off in ((ug_scr, c), (uu_scr, D_FF + c)):
            w = wup_ref[:, off:off + FF_CHUNK]
            scr[0:HALO, :] = _dot(hh, w)
            scr[HALO:HALO + rows, :] = _dot(hn, w)
            cw = cw_ref[:, off:off + FF_CHUNK]
            conv.append(scr[HALO - 2:HALO - 2 + rows, :] * cw[0:1, :]
                        + scr[HALO - 1:HALO - 1 + rows, :] * cw[1:2, :]
                        + scr[HALO:HALO + rows, :] * cw[2:3, :]
                        + cb_ref[:, off:off + FF_CHUNK])
        gate, up = conv
        act_scr[:, c:c + FF_CHUNK] = (_silu(gate) * up).astype(BF16)
    h2 = h + _dot(act_scr[...], wdn_ref[...])
    e = _rms(_dot(p_ref[...].astype(BF16), wp_ref[...]), gp_ref[...])
    gate = jax.nn.sigmoid(_dot(_rms(h2, gg_ref[...]).astype(BF16), wg_ref[...]))
    o_ref[...] = h2 + gate * e


def _ffn_ple(h1, p2, gf, w_up, conv_w, conv_b, w_down, gg, w_gate, w_ple, gp, seq):
    n = h1.shape[0]
    const = lambda i: (0, 0)
    halo_blocks = ROW_TILE // HALO
    return pl.pallas_call(
        functools.partial(_ffn_ple_kernel, tiles_per_seq=seq // ROW_TILE),
        grid=(n // ROW_TILE,),
        in_specs=[pl.BlockSpec((ROW_TILE, D_MODEL), lambda i: (i, 0)),
                  pl.BlockSpec((HALO, D_MODEL), lambda i: (jnp.maximum(i * halo_blocks - 1, 0), 0)),
                  pl.BlockSpec((ROW_TILE, PLE_DIM), lambda i: (i, 0)),
                  pl.BlockSpec((1, D_MODEL), const),
                  pl.BlockSpec((D_MODEL, 2 * D_FF), const, pipeline_mode=pl.Buffered(1)),
                  pl.BlockSpec((CONV_W, 2 * D_FF), const),
                  pl.BlockSpec((1, 2 * D_FF), const),
                  pl.BlockSpec((D_FF, D_MODEL), const, pipeline_mode=pl.Buffered(1)),
                  pl.BlockSpec((1, D_MODEL), const),
                  pl.BlockSpec((D_MODEL, D_MODEL), const, pipeline_mode=pl.Buffered(1)),
                  pl.BlockSpec((PLE_DIM, D_MODEL), const, pipeline_mode=pl.Buffered(1)),
                  pl.BlockSpec((1, D_MODEL), const)],
        out_specs=pl.BlockSpec((ROW_TILE, D_MODEL), lambda i: (i, 0)),
        out_shape=jax.ShapeDtypeStruct((n, D_MODEL), F32),
        scratch_shapes=[pltpu.VMEM((HALO + ROW_TILE, FF_CHUNK), F32),
                        pltpu.VMEM((HALO + ROW_TILE, FF_CHUNK), F32),
                        pltpu.VMEM((ROW_TILE, D_FF), BF16)],
        compiler_params=pltpu.CompilerParams(dimension_semantics=("parallel",),
                                             vmem_limit_bytes=VMEM_LIMIT),
        name="ffn_ple",
    )(h1, h1, p2, gf, w_up, conv_w, conv_b, w_down, gg, w_gate, w_ple, gp)


HG_LEVELS = (32, 16, 8, 4, 2, 1)
HG_TB = 256


def _hgrn_level_matrices():
    c = HG_CHUNK
    r = np.arange(c)[:, None]
    u = np.arange(c)[None, :]
    rows = []
    for m in HG_LEVELS:
        r0 = (r // m) * m
        upper = (r & m) != 0
        rows.append(np.where(upper, (u >= r0) & (u <= r), (u >= r + 1) & (u <= r0 + m - 1)))
    rows.append(u <= r)
    rows.append(u > r)
    return np.concatenate(rows, 0).astype(np.float32)


def _hgrn_pair_masks():
    c = HG_CHUNK
    t = np.arange(c)[:, None]
    s = np.arange(c)[None, :]
    masks = [((t // (2 * m)) == (s // (2 * m))) & ((t & m) != 0) & ((s & m) == 0) for m in HG_LEVELS]
    masks.append(t == s)
    return np.stack(masks).astype(np.float32)


def _hgrn_kernel(q_ref, f_ref, v_ref, g_ref, lbl_ref, ng_ref, mall_ref, msk_ref, o_ref, st_ref,
                 *, layer, n_chunks):
    @pl.when(pl.program_id(1) == 0)
    def _():
        st_ref[...] = jnp.zeros_like(st_ref)

    lbl = lbl_ref[...]
    e = jnp.exp(lbl - jnp.max(lbl, axis=0, keepdims=True))
    sm = e / jnp.sum(e, axis=0, keepdims=True)
    lb = jnp.sum(sm[:layer + 1], axis=0, keepdims=True)
    mall = mall_ref[...]
    ng = ng_ref[...]
    c = HG_CHUNK
    nl = len(HG_LEVELS)

    def chunk(ci, carry):
        r0 = pl.multiple_of(ci * c, c)
        fg = lb + (1.0 - lb) * jax.nn.sigmoid(f_ref[pl.ds(r0, c), :].astype(F32))
        kk = 1.0 - fg
        ee = jnp.exp(_dot(mall, jnp.concatenate(_split(jnp.log(fg)), axis=0)))
        qf = _silu(q_ref[pl.ds(r0, c), :].astype(F32)) * HG_KDIM ** -0.5
        gg = _silu(g_ref[pl.ds(r0, c), :].astype(F32))
        for h in range(HG_HEADS):
            sl = slice(h * HG_KDIM, (h + 1) * HG_KDIM)
            qh, kh = qf[:, sl], kk[:, sl]
            a = msk_ref[nl] * _dot_nt(qh.astype(BF16), kh.astype(BF16))
            for j in range(nl):
                ej = ee[j * c:(j + 1) * c, sl]
                a += msk_ref[j] * _dot_nt((qh * ej).astype(BF16), (kh * ej).astype(BF16))
            vh = v_ref[pl.ds(r0, c), sl]
            eb = ee[nl * c:(nl + 1) * c, sl]
            st = st_ref[h]
            o = _dot(a.astype(BF16), vh) + _dot_nt((qh * eb).astype(BF16), st.astype(BF16))
            kd = (kh * ee[(nl + 1) * c:(nl + 2) * c, sl]).astype(BF16)
            st_ref[h] = eb[c - 1:c, :] * st + _dot_tn(vh, kd)
            on = o * lax.rsqrt(jnp.mean(o * o, axis=-1, keepdims=True) + EPS) * ng
            o_ref[pl.ds(r0, c), sl] = (on * gg[:, sl]).astype(o_ref.dtype)
        return carry

    lax.fori_loop(0, n_chunks, chunk, 0, unroll=True)


def _hgrn(proj, lb_logits, norm_g, layer, batch, seq):
    nt = seq // HG_TB
    w = HG_HEADS * HG_KDIM
    col = lambda k: (lambda b, t: (b * nt + t, k))
    const2 = lambda b, t: (0, 0)
    return pl.pallas_call(
        functools.partial(_hgrn_kernel, layer=layer, n_chunks=HG_TB // HG_CHUNK),
        grid=(batch, nt),
        in_specs=[pl.BlockSpec((HG_TB, w), col(0)), pl.BlockSpec((HG_TB, w), col(1)),
                  pl.BlockSpec((HG_TB, w), col(2)), pl.BlockSpec((HG_TB, w), col(3)),
                  pl.BlockSpec(lb_logits.shape, const2),
                  pl.BlockSpec((1, HG_VDIM), const2),
                  pl.BlockSpec(((len(HG_LEVELS) + 2) * HG_CHUNK, 2 * HG_CHUNK), const2),
                  pl.BlockSpec((len(HG_LEVELS) + 1, HG_CHUNK, HG_CHUNK), lambda b, t: (0, 0, 0))],
        out_specs=pl.BlockSpec((HG_TB, w), lambda b, t: (b * nt + t, 0)),
        out_shape=jax.ShapeDtypeStruct((batch * seq, HG_WIDTH), BF16),
        scratch_shapes=[pltpu.VMEM((HG_HEADS, HG_VDIM, HG_KDIM), F32)],
        compiler_params=pltpu.CompilerParams(dimension_semantics=("parallel", "arbitrary"),
                                             vmem_limit_bytes=VMEM_LIMIT),
        name="hgrn2",
    )(proj, proj, proj, proj, lb_logits, norm_g,
      jnp.asarray(np.tile(_hgrn_level_matrices(), (1, 2)), BF16), jnp.asarray(_hgrn_pair_masks(), F32))


NSA_TB = 512
COL_NQ = 2048 // LANES
COL_KC, COL_VC, COL_KS, COL_VS, COL_KW, COL_VW, COL_GATE = (COL_NQ + 4 + k for k in range(7))
LOG2E = 1.4426950408889634
MASK_BIAS = -1e30


def _rope_tables(pos):
    half = ROT_DIM // 2
    inv = ROPE_THETA ** (-jnp.arange(half, dtype=F32) / half)
    ang = pos.astype(F32)[:, None] * inv[None, :]
    n = pos.shape[0]
    pad = HEAD_DIM - ROT_DIM
    cos = jnp.concatenate([jnp.cos(ang), jnp.cos(ang), jnp.ones((n, pad), F32)], axis=1)
    sin = jnp.concatenate([jnp.sin(ang), jnp.sin(ang), jnp.zeros((n, pad), F32)], axis=1)
    return cos, sin


def _rot_matrix(heads):
    half = ROT_DIM // 2
    r = np.zeros((HEAD_DIM, HEAD_DIM), np.float32)
    for j in range(half):
        r[j + half, j] = -1.0
        r[j, j + half] = 1.0
    return np.kron(np.eye(heads, dtype=np.float32), r)


def _head_ones(heads):
    return np.kron(np.eye(heads, dtype=np.float32), np.ones((HEAD_DIM, HEAD_DIM), np.float32))


def _norm_rope(x, g, cos, sin, rot, ones):
    hi, lo = _split(x * x)
    ssq = _dot(hi, ones) + _dot(lo, ones)
    xn = x * lax.rsqrt(ssq * (1.0 / HEAD_DIM) + EPS) * g
    return xn * cos + _dot(xn.astype(BF16), rot) * sin


def _nsa_prep_kernel(q_ref, ks_ref, vs_ref, kw_ref, vw_ref, cos_ref, sin_ref, gq_ref, gk_ref,
                     rot_ref, ones_ref, qo_ref, kso_ref, vso_ref, kwo_ref, vwo_ref):
    tb = q_ref.shape[0]
    cos2, sin2 = cos_ref[...], sin_ref[...]
    cos8 = jnp.concatenate([cos2] * (NSA_HEADS // 2), axis=1)
    sin8 = jnp.concatenate([sin2] * (NSA_HEADS // 2), axis=1)
    rot, ones = rot_ref[...], ones_ref[...]
    q = _norm_rope(q_ref[...].astype(F32), gq_ref[...], cos8, sin8, rot, ones) * (HEAD_DIM ** -0.5 * LOG2E)
    qo_ref[0] = q.T.reshape(NSA_HEADS, HEAD_DIM, tb).astype(qo_ref.dtype)
    lane = lax.broadcasted_iota(jnp.int32, (tb, 2 * HEAD_DIM), 1)
    block = (pl.program_id(1) * tb + lax.broadcasted_iota(jnp.int32, (tb, 2 * HEAD_DIM), 0)) // SEL_BLOCK
    onehot = jnp.where(lane - HEAD_DIM == block, 1.0, 0.0)
    r2, o2 = rot[:2 * HEAD_DIM, :2 * HEAD_DIM], ones[:2 * HEAD_DIM, :2 * HEAD_DIM]
    row_ones = jnp.ones((HEAD_DIM, tb), F32)
    for j, (k_ref, v_ref, vo_ref) in enumerate(((ks_ref, vs_ref, vso_ref), (kw_ref, vw_ref, vwo_ref))):
        k = _norm_rope(k_ref[...].astype(F32), gk_ref[j:j + 1, :], cos2, sin2, r2, o2)
        vt = v_ref[...].astype(F32).T
        for g in range(NSA_KV):
            kg = k if g == 0 else pltpu.roll(k, HEAD_DIM, 1)
            if j == 0:
                kso_ref[0, g] = jnp.where(lane < HEAD_DIM, kg, onehot).astype(kso_ref.dtype)
            else:
                kwo_ref[0, g] = kg[:, :HEAD_DIM].astype(kwo_ref.dtype)
            vo_ref[0, g] = jnp.concatenate([vt[g * HEAD_DIM:(g + 1) * HEAD_DIM], row_ones], axis=0).astype(vo_ref.dtype)


def _nsa_prep(proj, q_norm_g, k_norm_g, batch, seq):
    nt = seq // NSA_TB
    assert seq // SEL_BLOCK <= HEAD_DIM
    cos, sin = _rope_tables(jnp.arange(seq))
    cos2 = jnp.concatenate([cos, cos], axis=1)
    sin2 = jnp.concatenate([sin, sin], axis=1)
    gq = jnp.tile(q_norm_g, NSA_HEADS)[None, :]
    gk = jnp.tile(k_norm_g[1:3], (1, NSA_KV))
    col = lambda k: (lambda b, t: (b * nt + t, k))
    const2 = lambda b, t: (0, 0)
    rows_major = lambda d: pl.BlockSpec((1, NSA_KV, NSA_TB, d), lambda b, t: (b, 0, t, 0))
    cols_major = lambda heads, d: pl.BlockSpec((1, heads, d, NSA_TB), lambda b, t: (b, 0, 0, t))
    sds = lambda *shape: jax.ShapeDtypeStruct((batch,) + shape, BF16)
    return pl.pallas_call(
        _nsa_prep_kernel,
        grid=(batch, nt),
        in_specs=[pl.BlockSpec((NSA_TB, NSA_WIDTH), col(COL_NQ * LANES // NSA_WIDTH)),
                  pl.BlockSpec((NSA_TB, KV_WIDTH), col(COL_KS)),
                  pl.BlockSpec((NSA_TB, KV_WIDTH), col(COL_VS)),
                  pl.BlockSpec((NSA_TB, KV_WIDTH), col(COL_KW)),
                  pl.BlockSpec((NSA_TB, KV_WIDTH), col(COL_VW)),
                  pl.BlockSpec((NSA_TB, KV_WIDTH), lambda b, t: (t, 0)),
                  pl.BlockSpec((NSA_TB, KV_WIDTH), lambda b, t: (t, 0)),
                  pl.BlockSpec((1, NSA_WIDTH), const2),
                  pl.BlockSpec((2, KV_WIDTH), const2),
                  pl.BlockSpec((NSA_WIDTH, NSA_WIDTH), const2),
                  pl.BlockSpec((NSA_WIDTH, NSA_WIDTH), const2)],
        out_specs=[cols_major(NSA_HEADS, HEAD_DIM), rows_major(2 * HEAD_DIM), cols_major(NSA_KV, 2 * HEAD_DIM),
                   rows_major(HEAD_DIM), cols_major(NSA_KV, 2 * HEAD_DIM)],
        out_shape=[sds(NSA_HEADS, HEAD_DIM, seq), sds(NSA_KV, seq, 2 * HEAD_DIM), sds(NSA_KV, 2 * HEAD_DIM, seq),
                   sds(NSA_KV, seq, HEAD_DIM), sds(NSA_KV, 2 * HEAD_DIM, seq)],
        compiler_params=pltpu.CompilerParams(dimension_semantics=("parallel", "parallel"),
                                             vmem_limit_bytes=VMEM_LIMIT),
        name="nsa_prep",
    )(proj, proj, proj, proj, proj, cos2, sin2, gq, gk,
      jnp.asarray(_rot_matrix(NSA_HEADS), BF16), jnp.asarray(_head_ones(NSA_HEADS), BF16))


def _compress_kernel(xk_ref, xv_ref, pea_ref, peb_ref, w1_ref, w2_ref, gk_ref, cos_ref, sin_ref,
                     rot_ref, kc_ref, vc_ref, shift_scr):
    nc = xk_ref.shape[2]
    half = w1_ref.shape[1] // 2
    for j, x_ref in enumerate((xk_ref, xv_ref)):
        x = x_ref[0, 0].astype(F32)
        u = _dot((x + pea_ref[j:j + 1, :]).astype(BF16), w1_ref[j, :half, :])
        v = _dot((x + peb_ref[j:j + 1, :]).astype(BF16), w1_ref[j, half:, :])
        shift_scr[0:nc, :] = v
        shift_scr[nc:nc + HALO, :] = jnp.zeros((HALO, v.shape[1]), F32)
        hid = _silu(u + shift_scr[1:nc + 1, :])
        out = _dot(hid.astype(BF16), w2_ref[j])
        if j == 0:
            xn = _rms(out, gk_ref[...])
            out = xn * cos_ref[...] + _dot(xn.astype(BF16), rot_ref[...]) * sin_ref[...]
            kc_ref[0, 0] = out.astype(kc_ref.dtype)
        else:
            wide = jnp.concatenate([out, jnp.zeros_like(out)], axis=1)
            vc_ref[0, 0] = wide.T[:HEAD_DIM].astype(vc_ref.dtype)


def _compress(xk, xv, cmp_pe, cmp_w1, cmp_w2, gk0, batch, seq):
    nc = seq // CMP_STRIDE
    grp = CMP_STRIDE * HEAD_DIM
    cos, sin = _rope_tables(jnp.arange(nc) * CMP_STRIDE + CMP_LEN - 1)
    pea = cmp_pe[:, :CMP_STRIDE].reshape(2, grp)
    peb = cmp_pe[:, CMP_STRIDE:].reshape(2, grp)
    blk = pl.BlockSpec((1, 1, nc, grp), lambda b, g: (b, g, 0, 0))
    const2 = lambda b, g: (0, 0)
    const3 = lambda b, g: (0, 0, 0)
    return pl.pallas_call(
        _compress_kernel,
        grid=(batch, NSA_KV),
        in_specs=[blk, blk,
                  pl.BlockSpec((2, grp), const2), pl.BlockSpec((2, grp), const2),
                  pl.BlockSpec(cmp_w1.shape, const3), pl.BlockSpec(cmp_w2.shape, const3),
                  pl.BlockSpec((1, HEAD_DIM), const2),
                  pl.BlockSpec((nc, HEAD_DIM), const2), pl.BlockSpec((nc, HEAD_DIM), const2),
                  pl.BlockSpec((HEAD_DIM, HEAD_DIM), const2)],
        out_specs=[pl.BlockSpec((1, 1, nc, HEAD_DIM), lambda b, g: (b, g, 0, 0)),
                   pl.BlockSpec((1, 1, HEAD_DIM, nc), lambda b, g: (b, g, 0, 0))],
        out_shape=[jax.ShapeDtypeStruct((batch, NSA_KV, nc, HEAD_DIM), BF16),
                   jax.ShapeDtypeStruct((batch, NSA_KV, HEAD_DIM, nc), BF16)],
        scratch_shapes=[pltpu.VMEM((nc + HALO, cmp_w1.shape[2]), F32)],
        compiler_params=pltpu.CompilerParams(dimension_semantics=("parallel", "parallel"),
                                             vmem_limit_bytes=VMEM_LIMIT),
        name="nsa_compress",
    )(xk, xv, pea, peb, cmp_w1.astype(BF16), cmp_w2.astype(BF16), gk0[None, :], cos, sin,
      jnp.asarray(_rot_matrix(1), BF16))


NSA_TQ = 256


def _overlap_t(seq):
    ns, nc = seq // SEL_BLOCK, seq // CMP_STRIDE
    cs = np.arange(nc)[None, :] * CMP_STRIDE
    ss = np.arange(ns)[:, None] * SEL_BLOCK
    ov = np.clip(np.minimum(cs + CMP_LEN, ss + SEL_BLOCK) - np.maximum(cs, ss), 0, None)
    return (ov / CMP_LEN).astype(np.float32)


def _cmpsel_kernel(q_ref, kc_ref, vc_ref, ovt_ref, ocmp_ref, bias_ref, val_scr, cnt_scr, *, top):
    tq, nc = q_ref.shape[3], kc_ref.shape[2]
    ns = ovt_ref.shape[0]
    t0 = pl.program_id(1) * tq
    c_i = lax.broadcasted_iota(jnp.int32, (nc, NSA_GROUP * tq), 0)
    q_i = lax.broadcasted_iota(jnp.int32, (nc, NSA_GROUP * tq), 1) & (tq - 1)
    cmask = c_i * CMP_STRIDE + (CMP_LEN - 1) <= t0 + q_i
    n_i = lax.broadcasted_iota(jnp.int32, (ns, tq), 0)
    t_i = t0 + lax.broadcasted_iota(jnp.int32, (ns, tq), 1)
    cur = t_i // SEL_BLOCK
    forced = jnp.where(n_i == 0, FORCE, jnp.where(n_i == cur, FORCE, jnp.where(n_i == cur - 1, FORCE, 0.0)))
    visible = n_i * SEL_BLOCK <= t_i
    ovt = ovt_ref[...]
    for g in range(NSA_KV):
        kc, vc = kc_ref[0, g], vc_ref[0, g]
        qg = jnp.concatenate([q_ref[0, g * NSA_GROUP + h] for h in range(NSA_GROUP)], axis=1)
        s = jnp.where(cmask, _dot(kc, qg), NEG)
        e = jnp.exp2(s - jnp.max(s, axis=0, keepdims=True))
        p = jnp.where(cmask, e * (1.0 / jnp.sum(e, axis=0, keepdims=True)), 0.0)
        o = _dot(vc, p.astype(BF16)).astype(ocmp_ref.dtype)
        psum = p[:, :tq]
        for h in range(NSA_GROUP):
            ocmp_ref[0, g * NSA_GROUP + h] = o[:, h * tq:(h + 1) * tq]
            if h:
                psum += p[:, h * tq:(h + 1) * tq]
        hi, lo = _split(psum)
        val_scr[...] = jnp.where(visible, _dot(ovt, hi) + _dot(ovt, lo) + forced, NEG)
        cnt_scr[...] = jnp.zeros_like(cnt_scr)
        sub = lax.broadcasted_iota(jnp.int32, (SUBLANES, tq), 0)
        quarter = max(ns // 4, SUBLANES)
        last_quarter = (t0 + tq - 1) // (SEL_BLOCK * quarter)
        for rq in range(ns // quarter):
            for mq in range(ns // quarter):
                @pl.when(last_quarter >= max(rq, mq))
                def _():
                    for r in range(rq * quarter // SUBLANES, (rq + 1) * quarter // SUBLANES):
                        rows = pl.ds(r * SUBLANES, SUBLANES)
                        v_r = val_scr[rows, :]
                        c = cnt_scr[rows, :]
                        for m in range(mq * quarter, (mq + 1) * quarter):
                            row = val_scr[m:m + 1, :]
                            if m // SUBLANES > r:
                                hit = row > v_r
                            elif m // SUBLANES < r:
                                hit = row >= v_r
                            else:
                                hit = jnp.where(sub > m % SUBLANES, jnp.where(row >= v_r, 1.0, 0.0),
                                                jnp.where(row > v_r, 1.0, 0.0)) > 0.5
                            c = jnp.where(hit, c + 1.0, c)
                        cnt_scr[rows, :] = c
        bias = jnp.where(cnt_scr[...] < top, 0.0, MASK_BIAS)
        if ns < HEAD_DIM:
            bias = jnp.concatenate([bias, jnp.zeros((HEAD_DIM - ns, tq), F32)], axis=0)
        bias_ref[0, g] = bias.astype(bias_ref.dtype)


def _cmpsel(qt, kc, vct, batch, seq):
    nt = seq // NSA_TQ
    nc, ns = seq // CMP_STRIDE, seq // SEL_BLOCK
    qblk = pl.BlockSpec((1, NSA_HEADS, HEAD_DIM, NSA_TQ), lambda b, t: (b, 0, 0, t))
    return pl.pallas_call(
        functools.partial(_cmpsel_kernel, top=min(SEL_TOPK, ns)),
        grid=(batch, nt),
        in_specs=[qblk,
                  pl.BlockSpec((1, NSA_KV, nc, HEAD_DIM), lambda b, t: (b, 0, 0, 0)),
                  pl.BlockSpec((1, NSA_KV, HEAD_DIM, nc), lambda b, t: (b, 0, 0, 0)),
                  pl.BlockSpec((ns, nc), lambda b, t: (0, 0))],
        out_specs=[qblk, pl.BlockSpec((1, NSA_KV, HEAD_DIM, NSA_TQ), lambda b, t: (b, 0, 0, t))],
        out_shape=[jax.ShapeDtypeStruct((batch, NSA_HEADS, HEAD_DIM, seq), BF16),
                   jax.ShapeDtypeStruct((batch, NSA_KV, HEAD_DIM, seq), BF16)],
        scratch_shapes=[pltpu.VMEM((ns, NSA_TQ), F32), pltpu.VMEM((ns, NSA_TQ), F32)],
        compiler_params=pltpu.CompilerParams(dimension_semantics=("parallel", "parallel"),
                                             vmem_limit_bytes=VMEM_LIMIT),
        name="nsa_cmpsel",
    )(qt, kc, vct, jnp.asarray(_overlap_t(seq), BF16))


def _attn_kernel(q_ref, ks_ref, vs_ref, kw_ref, vw_ref, bias_ref, ocmp_ref, gate_ref, gout_ref, o_ref,
                 m_scr, acc_scr):
    tq = q_ref.shape[3]
    i = pl.program_id(1)
    t0 = i * tq
    q_grp, q_sel = [], []
    for g in range(NSA_KV):
        q_grp.append(jnp.concatenate([q_ref[0, g * NSA_GROUP + h] for h in range(NSA_GROUP)], axis=1))
        q_sel.append(jnp.concatenate([q_grp[g], jnp.concatenate([bias_ref[0, g]] * NSA_GROUP, axis=1)], axis=0))

    def per_head(s, keep):
        return jnp.concatenate([jnp.where(keep, s[:, h * tq:(h + 1) * tq], NEG) for h in range(NSA_GROUP)], axis=1)

    m_scr[...] = jnp.full(m_scr.shape, NEG, F32)
    acc_scr[...] = jnp.zeros_like(acc_scr)

    def sel_step(k0, nk, diag):
        for g in range(NSA_KV):
            s = _dot(ks_ref[0, g, pl.ds(k0, nk), :], q_sel[g])
            if diag:
                key_i = lax.broadcasted_iota(jnp.int32, (nk, tq), 0)
                s = per_head(s, key_i <= lax.broadcasted_iota(jnp.int32, (nk, tq), 1))
            m_old = m_scr[g]
            m_new = jnp.maximum(m_old, jnp.max(s, axis=0, keepdims=True))
            p = jnp.exp2(s - m_new).astype(BF16)
            acc_scr[g] = jnp.exp2(m_old - m_new) * acc_scr[g] + _dot(vs_ref[0, g, :, pl.ds(k0, nk)], p)
            m_scr[g] = m_new

    def body(jj, carry):
        sel_step(pl.multiple_of(jj * 2 * tq, 2 * tq), 2 * tq, False)
        return carry

    lax.fori_loop(0, i // 2, body, 0)

    @pl.when(i % 2 == 1)
    def _():
        sel_step(pl.multiple_of(t0 - tq, tq), tq, False)

    sel_step(pl.multiple_of(t0, tq), tq, True)

    span = WINDOW + tq
    w0 = pl.multiple_of(jnp.maximum(t0 + tq - span, 0), tq)
    dist = (t0 - w0) + (lax.broadcasted_iota(jnp.int32, (span, tq), 1)
                        - lax.broadcasted_iota(jnp.int32, (span, tq), 0))
    in_window = jnp.abs(2 * dist - (WINDOW - 1)) < WINDOW
    win = []
    for g in range(NSA_KV):
        s = per_head(_dot(kw_ref[0, g, pl.ds(w0, span), :], q_grp[g]), in_window)
        p = jnp.exp2(s - jnp.max(s, axis=0, keepdims=True)).astype(BF16)
        win.append(_dot(vw_ref[0, g, :, pl.ds(w0, span)], p))

    gate = jax.nn.sigmoid(gate_ref[...].astype(F32).T)
    gout = gout_ref[...]
    rows = []
    for hh in range(NSA_HEADS):
        g, lanes = hh // NSA_GROUP, slice((hh % NSA_GROUP) * tq, (hh % NSA_GROUP + 1) * tq)
        a_s, a_w = acc_scr[g][:, lanes], win[g][:, lanes]
        o = (gate[hh:hh + 1] * ocmp_ref[0, hh].astype(F32)
             + gate[NSA_HEADS + hh:NSA_HEADS + hh + 1] * (a_s[:HEAD_DIM] * (1.0 / a_s[HEAD_DIM:HEAD_DIM + 1]))
             + gate[2 * NSA_HEADS + hh:2 * NSA_HEADS + hh + 1] * (a_w[:HEAD_DIM] * (1.0 / a_w[HEAD_DIM:HEAD_DIM + 1])))
        rows.append(o * lax.rsqrt(jnp.mean(o * o, axis=0, keepdims=True) + EPS) * gout)
    o_ref[...] = jnp.concatenate(rows, axis=0).T.astype(o_ref.dtype)


def _attn(qt, ks, vst, kw, vwt, bias, ocmp, proj, out_norm_g, batch, seq):
    nt = seq // NSA_TQ
    assert WINDOW % NSA_TQ == 0 and seq >= WINDOW + NSA_TQ
    rows_full = lambda d: pl.BlockSpec((1, NSA_KV, seq, d), lambda b, t: (b, 0, 0, 0))
    cols_full = pl.BlockSpec((1, NSA_KV, 2 * HEAD_DIM, seq), lambda b, t: (b, 0, 0, 0))
    qtile = lambda heads: pl.BlockSpec((1, heads, HEAD_DIM, NSA_TQ), lambda b, t: (b, 0, 0, t))
    return pl.pallas_call(
        _attn_kernel,
        grid=(batch, nt),
        in_specs=[qtile(NSA_HEADS), rows_full(2 * HEAD_DIM), cols_full, rows_full(HEAD_DIM), cols_full,
                  qtile(NSA_KV), qtile(NSA_HEADS),
                  pl.BlockSpec((NSA_TQ, LANES), lambda b, t: (b * nt + t, COL_GATE)),
                  pl.BlockSpec((HEAD_DIM, NSA_TQ), lambda b, t: (0, 0))],
        out_specs=pl.BlockSpec((NSA_TQ, NSA_WIDTH), lambda b, t: (b * nt + t, 0)),
        out_shape=jax.ShapeDtypeStruct((batch * seq, NSA_WIDTH), BF16),
        scratch_shapes=[pltpu.VMEM((NSA_KV, 1, NSA_GROUP * NSA_TQ), F32),
                        pltpu.VMEM((NSA_KV, 2 * HEAD_DIM, NSA_GROUP * NSA_TQ), F32)],
        compiler_params=pltpu.CompilerParams(dimension_semantics=("parallel", "parallel"),
                                             vmem_limit_bytes=VMEM_LIMIT),
        name="nsa_attn",
    )(qt, ks, vst, kw, vwt, bias, ocmp, proj, jnp.tile(out_norm_g[:, None], (1, NSA_TQ)))


def _nsa(proj, q_norm_g, k_norm_g, cmp_pe, cmp_w1, cmp_w2, out_norm_g, batch, seq):
    qt, ks, vst, kw, vwt = _nsa_prep(proj, q_norm_g, k_norm_g, batch, seq)

    def groups(col):
        a = proj[:, col * LANES:(col + 1) * LANES].reshape(batch, seq, NSA_KV, HEAD_DIM)
        return a.transpose(0, 2, 1, 3).reshape(batch, NSA_KV, seq // CMP_STRIDE, CMP_STRIDE * HEAD_DIM)

    kc, vct = _compress(groups(COL_KC), groups(COL_VC), cmp_pe, cmp_w1, cmp_w2, k_norm_g[0], batch, seq)
    ocmp, bias = _cmpsel(qt, kc, vct, batch, seq)
    return _attn(qt, ks, vst, kw, vwt, bias, ocmp, proj, out_norm_g, batch, seq)


def kernel(x, p, attn_norm_g, w_in, hg_lb_logits, hg_norm_g, nsa_q_norm_g, nsa_k_norm_g, cmp_pe,
           cmp_w1, cmp_w2, nsa_out_norm_g, w_out, ffn_norm_g, w_up, conv_w, conv_b, w_down,
           ple_gate_norm_g, w_ple_gate, w_ple, ple_norm_g):
    B, T, _ = x.shape
    n = B * T
    h = x.reshape(n, D_MODEL)
    for i in range(w_in.shape[0]):
        w_pad = jnp.pad(w_in[i], ((0, 0), (0, IN_PAD - IN_TOTAL))).astype(BF16)
        proj = _in_proj(h, attn_norm_g[i][None, :], w_pad, BF16)
        o_hg = _hgrn(proj, hg_lb_logits.astype(F32), hg_norm_g[i][None, :], i, B, T)
        o_nsa = _nsa(proj, nsa_q_norm_g[i], nsa_k_norm_g[i], cmp_pe[i], cmp_w1[i], cmp_w2[i],
                     nsa_out_norm_g[i], B, T)
        w_o = w_out[i].astype(BF16)
        h1 = _out_proj(h, o_hg, o_nsa, w_o[:HG_WIDTH], w_o[HG_WIDTH:])
        h = _ffn_ple(h1, p[i].reshape(n, PLE_DIM), ffn_norm_g[i][None, :], w_up[i].astype(BF16),
                     conv_w[i], conv_b[i][None, :], w_down[i].astype(BF16),
                     ple_gate_norm_g[i][None, :], w_ple_gate[i].astype(BF16), w_ple[i].astype(BF16),
                     ple_norm_g[i][None, :], T)
    return h.reshape(B, T, D_MODEL)
```

```python
import functools

import numpy as np
import jax
import jax.numpy as jnp
from jax import lax
from jax.experimental import pallas as pl
from jax.experimental.pallas import tpu as pltpu

F32 = jnp.float32
BF16 = jnp.bfloat16

D_MODEL = 1024
PLE_DIM = 256
EPS = 1e-6
NEG = -1e30
FORCE = 1e6

HG_HEADS = 4
HG_KDIM = 128
HG_VDIM = 128
HG_WIDTH = HG_HEADS * HG_VDIM
HG_CHUNK = 64

NSA_HEADS = 8
NSA_KV = 2
NSA_GROUP = NSA_HEADS // NSA_KV
HEAD_DIM = 64
NSA_WIDTH = NSA_HEADS * HEAD_DIM
KV_WIDTH = NSA_KV * HEAD_DIM
CMP_LEN = 32
CMP_STRIDE = 16
SEL_BLOCK = 64
SEL_TOPK = 16
WINDOW = 512
ROT_DIM = HEAD_DIM // 4
ROPE_THETA = 500000.0
D_FF = 2816
CONV_W = 3

IN_SIZES = (HG_HEADS * HG_KDIM, HG_HEADS * HG_KDIM, HG_WIDTH, HG_WIDTH, NSA_WIDTH,
            KV_WIDTH, KV_WIDTH, KV_WIDTH, KV_WIDTH, KV_WIDTH, KV_WIDTH, 3 * NSA_HEADS)
IN_TOTAL = sum(IN_SIZES)
LANES = 128
IN_PAD = -(-IN_TOTAL // LANES) * LANES
VMEM_LIMIT = 56 * 1024 * 1024

ROW_TILE = 512
FF_CHUNK = 256
SUBLANES = 8
HALO = SUBLANES
FFN_HALO = 2 * SUBLANES


def _rms(x, g):
    return x * lax.rsqrt(jnp.mean(x * x, axis=-1, keepdims=True) + EPS) * g


def _dot(a, b):
    return jnp.dot(a, b, preferred_element_type=F32)


def _dot_nt(a, b):
    return lax.dot_general(a, b, (((1,), (1,)), ((), ())), preferred_element_type=F32)


def _dot_tn(a, b):
    return lax.dot_general(a, b, (((0,), (0,)), ((), ())), preferred_element_type=F32)


def _split(x):
    hi = x.astype(BF16)
    return hi, (x - hi.astype(F32)).astype(BF16)


def _silu(x):
    hx = 0.5 * x
    return hx + hx * jnp.tanh(hx)


def _in_proj_kernel(x_ref, g_ref, w_ref, o_ref, *, col_chunk):
    xb = _rms(x_ref[...], g_ref[...]).astype(BF16)
    for c in range(0, o_ref.shape[1], col_chunk):
        o_ref[:, c:c + col_chunk] = _dot(xb, w_ref[:, c:c + col_chunk]).astype(o_ref.dtype)


def _in_proj(x2, g, w_pad, out_dtype):
    n = x2.shape[0]
    width = w_pad.shape[1]
    return pl.pallas_call(
        functools.partial(_in_proj_kernel, col_chunk=width // 3),
        grid=(n // ROW_TILE,),
        in_specs=[pl.BlockSpec((ROW_TILE, D_MODEL), lambda i: (i, 0)),
                  pl.BlockSpec((1, D_MODEL), lambda i: (0, 0)),
                  pl.BlockSpec((D_MODEL, width), lambda i: (0, 0))],
        out_specs=pl.BlockSpec((ROW_TILE, width), lambda i: (i, 0)),
        out_shape=jax.ShapeDtypeStruct((n, width), out_dtype),
        compiler_params=pltpu.CompilerParams(dimension_semantics=("parallel",),
                                             vmem_limit_bytes=VMEM_LIMIT),
        name="in_proj",
    )(x2, g, w_pad)


def _out_proj_kernel(x_ref, a_ref, b_ref, wa_ref, wb_ref, o_ref):
    acc = _dot(a_ref[...], wa_ref[...])
    acc += _dot(b_ref[...], wb_ref[...])
    o_ref[...] = x_ref[...] + acc


def _out_proj(x2, o_hg, o_nsa, w_a, w_b):
    n = x2.shape[0]
    return pl.pallas_call(
        _out_proj_kernel,
        grid=(n // ROW_TILE,),
        in_specs=[pl.BlockSpec((ROW_TILE, D_MODEL), lambda i: (i, 0)),
                  pl.BlockSpec((ROW_TILE, HG_WIDTH), lambda i: (i, 0)),
                  pl.BlockSpec((ROW_TILE, NSA_WIDTH), lambda i: (i, 0)),
                  pl.BlockSpec((HG_WIDTH, D_MODEL), lambda i: (0, 0)),
                  pl.BlockSpec((NSA_WIDTH, D_MODEL), lambda i: (0, 0))],
        out_specs=pl.BlockSpec((ROW_TILE, D_MODEL), lambda i: (i, 0)),
        out_shape=jax.ShapeDtypeStruct((n, D_MODEL), F32),
        compiler_params=pltpu.CompilerParams(dimension_semantics=("parallel",),
                                             vmem_limit_bytes=VMEM_LIMIT),
        name="out_proj",
    )(x2, o_hg, o_nsa, w_a, w_b)


def _ffn_ple_kernel(h_ref, halo_ref, p_ref, gf_ref, wup_ref, cw_ref, cb_ref, wdn_ref,
                    gg_ref, wg_ref, wp_ref, gp_ref, o_ref, hn_scr, ug_scr, uu_scr, act_scr, *, tiles_per_seq):
    h = h_ref[...]
    rows = h.shape[0]
    first = (pl.program_id(0) % tiles_per_seq) == 0
    hn_scr[0:FFN_HALO, :] = jnp.where(first, 0.0, _rms(halo_ref[...], gf_ref[...])).astype(BF16)
    hn_scr[FFN_HALO:FFN_HALO + rows, :] = _rms(h, gf_ref[...]).astype(BF16)
    for c in range(0, D_FF, FF_CHUNK):
        conv = []
        for scr, off in ((ug_scr, c), (uu_scr, D_FF + c)):
            scr[...] = _dot(hn_scr[...], wup_ref[:, off:off + FF_CHUNK])
            cw = cw_ref[:, off:off + FF_CHUNK]
            conv.append(scr[FFN_HALO - 2:FFN_HALO - 2 + rows, :] * cw[0:1, :]
                        + scr[FFN_HALO - 1:FFN_HALO - 1 + rows, :] * cw[1:2, :]
                        + scr[FFN_HALO:FFN_HALO + rows, :] * cw[2:3, :]
                        + cb_ref[:, off:off + FF_CHUNK])
        gate, up = conv
        act_scr[:, c:c + FF_CHUNK] = (_silu(gate) * up).astype(BF16)
    o_ref[...] = h_ref[...] + _dot(act_scr[...], wdn_ref[...])
    hn_scr[FFN_HALO:FFN_HALO + rows, :] = _rms(o_ref[...], gg_ref[...]).astype(BF16)
    e = _rms(_dot(p_ref[...].astype(BF16), wp_ref[...]), gp_ref[...])
    gate = jax.nn.sigmoid(_dot(hn_scr[FFN_HALO:FFN_HALO + rows, :], wg_ref[...]))
    o_ref[...] = o_ref[...] + gate * e


def _ffn_ple(h1, p2, gf, w_up, conv_w, conv_b, w_down, gg, w_gate, w_ple, gp, seq):
    n = h1.shape[0]
    const = lambda i: (0, 0)
    halo_blocks = ROW_TILE // FFN_HALO
    return pl.pallas_call(
        functools.partial(_ffn_ple_kernel, tiles_per_seq=seq // ROW_TILE),
        grid=(n // ROW_TILE,),
        in_specs=[pl.BlockSpec((ROW_TILE, D_MODEL), lambda i: (i, 0)),
                  pl.BlockSpec((FFN_HALO, D_MODEL), lambda i: (jnp.maximum(i * halo_blocks - 1, 0), 0)),
                  pl.BlockSpec((ROW_TILE, PLE_DIM), lambda i: (i, 0)),
                  pl.BlockSpec((1, D_MODEL), const),
                  pl.BlockSpec((D_MODEL, 2 * D_FF), const, pipeline_mode=pl.Buffered(1)),
                  pl.BlockSpec((CONV_W, 2 * D_FF), const),
                  pl.BlockSpec((1, 2 * D_FF), const),
                  pl.BlockSpec((D_FF, D_MODEL), const, pipeline_mode=pl.Buffered(1)),
                  pl.BlockSpec((1, D_MODEL), const),
                  pl.BlockSpec((D_MODEL, D_MODEL), const, pipeline_mode=pl.Buffered(1)),
                  pl.BlockSpec((PLE_DIM, D_MODEL), const, pipeline_mode=pl.Buffered(1)),
                  pl.BlockSpec((1, D_MODEL), const)],
        out_specs=pl.BlockSpec((ROW_TILE, D_MODEL), lambda i: (i, 0)),
        out_shape=jax.ShapeDtypeStruct((n, D_MODEL), F32),
        scratch_shapes=[pltpu.VMEM((FFN_HALO + ROW_TILE, D_MODEL), BF16),
                        pltpu.VMEM((FFN_HALO + ROW_TILE, FF_CHUNK), F32),
                        pltpu.VMEM((FFN_HALO + ROW_TILE, FF_CHUNK), F32),
                        pltpu.VMEM((ROW_TILE, D_FF), BF16)],
        compiler_params=pltpu.CompilerParams(dimension_semantics=("parallel",),
                                             vmem_limit_bytes=VMEM_LIMIT),
        name="ffn_ple",
    )(h1, h1, p2, gf, w_up, conv_w, conv_b, w_down, gg, w_gate, w_ple, gp)


HG_LEVELS = (32, 16, 8, 4, 2, 1)
HG_TB = 256


def _hgrn_level_matrices():
    c = HG_CHUNK
    r = np.arange(c)[:, None]
    u = np.arange(c)[None, :]
    rows = []
    for m in HG_LEVELS:
        r0 = (r // m) * m
        upper = (r & m) != 0
        rows.append(np.where(upper, (u >= r0) & (u <= r), (u >= r + 1) & (u <= r0 + m - 1)))
    rows.append(u <= r)
    rows.append(u > r)
    return np.concatenate(rows, 0).astype(np.float32)


def _hgrn_pair_masks():
    c = HG_CHUNK
    t = np.arange(c)[:, None]
    s = np.arange(c)[None, :]
    masks = [((t // (2 * m)) == (s // (2 * m))) & ((t & m) != 0) & ((s & m) == 0) for m in HG_LEVELS]
    masks.append(t == s)
    return np.stack(masks).astype(np.float32)


def _hgrn_kernel(q_ref, f_ref, v_ref, g_ref, lbl_ref, ng_ref, mall_ref, msk_ref, o_ref, st_ref,
                 *, layer, n_chunks):
    @pl.when(pl.program_id(1) == 0)
    def _():
        st_ref[...] = jnp.zeros_like(st_ref)

    lbl = lbl_ref[...]
    e = jnp.exp(lbl - jnp.max(lbl, axis=0, keepdims=True))
    sm = e / jnp.sum(e, axis=0, keepdims=True)
    lb = jnp.sum(sm[:layer + 1], axis=0, keepdims=True)
    mall = mall_ref[...]
    ng = ng_ref[...]
    c = HG_CHUNK
    nl = len(HG_LEVELS)

    intra, q_decayed, carry_decay, carry_add = {}, {}, {}, {}
    for ci in range(n_chunks):
        r0 = ci * c
        fg = lb + (1.0 - lb) * jax.nn.sigmoid(f_ref[r0:r0 + c, :].astype(F32))
        kk = 1.0 - fg
        ee = jnp.exp(_dot(mall, jnp.concatenate(_split(jnp.log(fg)), axis=0)))
        qf = _silu(q_ref[r0:r0 + c, :].astype(F32)) * HG_KDIM ** -0.5
        for h in range(HG_HEADS):
            sl = slice(h * HG_KDIM, (h + 1) * HG_KDIM)
            qh, kh = qf[:, sl], kk[:, sl]
            a = msk_ref[nl] * _dot_nt(qh.astype(BF16), kh.astype(BF16))
            for j in range(nl):
                ej = ee[j * c:(j + 1) * c, sl]
                a += msk_ref[j] * _dot_nt((qh * ej).astype(BF16), (kh * ej).astype(BF16))
            vh = v_ref[r0:r0 + c, sl]
            eb = ee[nl * c:(nl + 1) * c, sl]
            intra[ci, h] = _dot(a.astype(BF16), vh)
            q_decayed[ci, h] = (qh * eb).astype(BF16)
            carry_decay[ci, h] = eb[c - 1:c, :]
            carry_add[ci, h] = _dot_tn(vh, (kh * ee[(nl + 1) * c:(nl + 2) * c, sl]).astype(BF16))

    for h in range(HG_HEADS):
        sl = slice(h * HG_KDIM, (h + 1) * HG_KDIM)
        st = st_ref[h]
        states = []
        for ci in range(n_chunks):
            states.append(st.astype(BF16))
            st = carry_decay[ci, h] * st + carry_add[ci, h]
        st_ref[h] = st
        for ci in range(n_chunks):
            r0 = ci * c
            o = intra[ci, h] + _dot_nt(q_decayed[ci, h], states[ci])
            on = o * lax.rsqrt(jnp.mean(o * o, axis=-1, keepdims=True) + EPS) * ng
            o_ref[r0:r0 + c, sl] = (on * _silu(g_ref[r0:r0 + c, sl].astype(F32))).astype(o_ref.dtype)


def _hgrn(proj, lb_logits, norm_g, layer, batch, seq):
    nt = seq // HG_TB
    w = HG_HEADS * HG_KDIM
    col = lambda k: (lambda b, t: (b * nt + t, k))
    const2 = lambda b, t: (0, 0)
    return pl.pallas_call(
        functools.partial(_hgrn_kernel, layer=layer, n_chunks=HG_TB // HG_CHUNK),
        grid=(batch, nt),
        in_specs=[pl.BlockSpec((HG_TB, w), col(0)), pl.BlockSpec((HG_TB, w), col(1)),
                  pl.BlockSpec((HG_TB, w), col(2)), pl.BlockSpec((HG_TB, w), col(3)),
                  pl.BlockSpec(lb_logits.shape, const2),
                  pl.BlockSpec((1, HG_VDIM), const2),
                  pl.BlockSpec(((len(HG_LEVELS) + 2) * HG_CHUNK, 2 * HG_CHUNK), const2),
                  pl.BlockSpec((len(HG_LEVELS) + 1, HG_CHUNK, HG_CHUNK), lambda b, t: (0, 0, 0))],
        out_specs=pl.BlockSpec((HG_TB, w), lambda b, t: (b * nt + t, 0)),
        out_shape=jax.ShapeDtypeStruct((batch * seq, HG_WIDTH), BF16),
        scratch_shapes=[pltpu.VMEM((HG_HEADS, HG_VDIM, HG_KDIM), F32)],
        compiler_params=pltpu.CompilerParams(dimension_semantics=("parallel", "arbitrary"),
                                             vmem_limit_bytes=VMEM_LIMIT),
        name="hgrn2",
    )(proj, proj, proj, proj, lb_logits, norm_g,
      jnp.asarray(np.tile(_hgrn_level_matrices(), (1, 2)), BF16), jnp.asarray(_hgrn_pair_masks(), F32))


NSA_TB = 512
COL_NQ = 2048 // LANES
COL_KC, COL_VC, COL_KS, COL_VS, COL_KW, COL_VW, COL_GATE = (COL_NQ + 4 + k for k in range(7))
LOG2E = 1.4426950408889634
MASK_BIAS = -1e30


def _rope_tables(pos):
    half = ROT_DIM // 2
    inv = ROPE_THETA ** (-jnp.arange(half, dtype=F32) / half)
    ang = pos.astype(F32)[:, None] * inv[None, :]
    n = pos.shape[0]
    pad = HEAD_DIM - ROT_DIM
    cos = jnp.concatenate([jnp.cos(ang), jnp.cos(ang), jnp.ones((n, pad), F32)], axis=1)
    sin = jnp.concatenate([jnp.sin(ang), jnp.sin(ang), jnp.zeros((n, pad), F32)], axis=1)
    return cos, sin


def _rot_matrix(heads):
    half = ROT_DIM // 2
    r = np.zeros((HEAD_DIM, HEAD_DIM), np.float32)
    for j in range(half):
        r[j + half, j] = -1.0
        r[j, j + half] = 1.0
    return np.kron(np.eye(heads, dtype=np.float32), r)


def _head_ones(heads):
    return np.kron(np.eye(heads, dtype=np.float32), np.ones((HEAD_DIM, HEAD_DIM), np.float32))


def _norm_rope(x, g, cos, sin, rot, ones):
    hi, lo = _split(x * x)
    ssq = _dot(hi, ones) + _dot(lo, ones)
    xn = x * lax.rsqrt(ssq * (1.0 / HEAD_DIM) + EPS) * g
    return xn * cos + _dot(xn.astype(BF16), rot) * sin


def _nsa_prep_kernel(q_ref, ks_ref, vs_ref, kw_ref, vw_ref, cos_ref, sin_ref, gq_ref, gk_ref,
                     rot_ref, ones_ref, qo_ref, kso_ref, vso_ref, kwo_ref, vwo_ref):
    tb = q_ref.shape[0]
    cos2, sin2 = cos_ref[...], sin_ref[...]
    cos8 = jnp.concatenate([cos2] * (NSA_HEADS // 2), axis=1)
    sin8 = jnp.concatenate([sin2] * (NSA_HEADS // 2), axis=1)
    rot, ones = rot_ref[...], ones_ref[...]
    q = _norm_rope(q_ref[...].astype(F32), gq_ref[...], cos8, sin8, rot, ones) * (HEAD_DIM ** -0.5 * LOG2E)
    qo_ref[0] = q.T.reshape(NSA_HEADS, HEAD_DIM, tb).astype(qo_ref.dtype)
    lane = lax.broadcasted_iota(jnp.int32, (tb, 2 * HEAD_DIM), 1)
    block = (pl.program_id(1) * tb + lax.broadcasted_iota(jnp.int32, (tb, 2 * HEAD_DIM), 0)) // SEL_BLOCK
    onehot = jnp.where(lane - HEAD_DIM == block, 1.0, 0.0)
    r2, o2 = rot[:2 * HEAD_DIM, :2 * HEAD_DIM], ones[:2 * HEAD_DIM, :2 * HEAD_DIM]
    row_ones = jnp.ones((HEAD_DIM, tb), F32)
    for j, (k_ref, v_ref, vo_ref) in enumerate(((ks_ref, vs_ref, vso_ref), (kw_ref, vw_ref, vwo_ref))):
        k = _norm_rope(k_ref[...].astype(F32), gk_ref[j:j + 1, :], cos2, sin2, r2, o2)
        vt = v_ref[...].astype(F32).T
        for g in range(NSA_KV):
            kg = k if g == 0 else pltpu.roll(k, HEAD_DIM, 1)
            if j == 0:
                kso_ref[0, g] = jnp.where(lane < HEAD_DIM, kg, onehot).astype(kso_ref.dtype)
            else:
                kwo_ref[0, g] = kg[:, :HEAD_DIM].astype(kwo_ref.dtype)
            vo_ref[0, g] = jnp.concatenate([vt[g * HEAD_DIM:(g + 1) * HEAD_DIM], row_ones], axis=0).astype(vo_ref.dtype)


def _nsa_prep(proj, q_norm_g, k_norm_g, batch, seq):
    nt = seq // NSA_TB
    assert seq // SEL_BLOCK <= HEAD_DIM
    cos, sin = _rope_tables(jnp.arange(seq))
    cos2 = jnp.concatenate([cos, cos], axis=1)
    sin2 = jnp.concatenate([sin, sin], axis=1)
    gq = jnp.tile(q_norm_g, NSA_HEADS)[None, :]
    gk = jnp.tile(k_norm_g[1:3], (1, NSA_KV))
    col = lambda k: (lambda b, t: (b * nt + t, k))
    const2 = lambda b, t: (0, 0)
    rows_major = lambda d: pl.BlockSpec((1, NSA_KV, NSA_TB, d), lambda b, t: (b, 0, t, 0))
    cols_major = lambda heads, d: pl.BlockSpec((1, heads, d, NSA_TB), lambda b, t: (b, 0, 0, t))
    sds = lambda *shape: jax.ShapeDtypeStruct((batch,) + shape, BF16)
    return pl.pallas_call(
        _nsa_prep_kernel,
        grid=(batch, nt),
        in_specs=[pl.BlockSpec((NSA_TB, NSA_WIDTH), col(COL_NQ * LANES // NSA_WIDTH)),
                  pl.BlockSpec((NSA_TB, KV_WIDTH), col(COL_KS)),
                  pl.BlockSpec((NSA_TB, KV_WIDTH), col(COL_VS)),
                  pl.BlockSpec((NSA_TB, KV_WIDTH), col(COL_KW)),
                  pl.BlockSpec((NSA_TB, KV_WIDTH), col(COL_VW)),
                  pl.BlockSpec((NSA_TB, KV_WIDTH), lambda b, t: (t, 0)),
                  pl.BlockSpec((NSA_TB, KV_WIDTH), lambda b, t: (t, 0)),
                  pl.BlockSpec((1, NSA_WIDTH), const2),
                  pl.BlockSpec((2, KV_WIDTH), const2),
                  pl.BlockSpec((NSA_WIDTH, NSA_WIDTH), const2),
                  pl.BlockSpec((NSA_WIDTH, NSA_WIDTH), const2)],
        out_specs=[cols_major(NSA_HEADS, HEAD_DIM), rows_major(2 * HEAD_DIM), cols_major(NSA_KV, 2 * HEAD_DIM),
                   rows_major(HEAD_DIM), cols_major(NSA_KV, 2 * HEAD_DIM)],
        out_shape=[sds(NSA_HEADS, HEAD_DIM, seq), sds(NSA_KV, seq, 2 * HEAD_DIM), sds(NSA_KV, 2 * HEAD_DIM, seq),
                   sds(NSA_KV, seq, HEAD_DIM), sds(NSA_KV, 2 * HEAD_DIM, seq)],
        compiler_params=pltpu.CompilerParams(dimension_semantics=("parallel", "parallel"),
                                             vmem_limit_bytes=VMEM_LIMIT),
        name="nsa_prep",
    )(proj, proj, proj, proj, proj, cos2, sin2, gq, gk,
      jnp.asarray(_rot_matrix(NSA_HEADS), BF16), jnp.asarray(_head_ones(NSA_HEADS), BF16))


def _compress_kernel(xk_ref, xv_ref, pea_ref, peb_ref, w1_ref, w2_ref, gk_ref, cos_ref, sin_ref,
                     rot_ref, kc_ref, vc_ref, shift_scr):
    nc = xk_ref.shape[2]
    half = w1_ref.shape[1] // 2
    for j, x_ref in enumerate((xk_ref, xv_ref)):
        x = x_ref[0, 0].astype(F32)
        u = _dot((x + pea_ref[j:j + 1, :]).astype(BF16), w1_ref[j, :half, :])
        v = _dot((x + peb_ref[j:j + 1, :]).astype(BF16), w1_ref[j, half:, :])
        shift_scr[0:nc, :] = v
        shift_scr[nc:nc + HALO, :] = jnp.zeros((HALO, v.shape[1]), F32)
        hid = _silu(u + shift_scr[1:nc + 1, :])
        out = _dot(hid.astype(BF16), w2_ref[j])
        if j == 0:
            xn = _rms(out, gk_ref[...])
            out = xn * cos_ref[...] + _dot(xn.astype(BF16), rot_ref[...]) * sin_ref[...]
            kc_ref[0, 0] = out.astype(kc_ref.dtype)
        else:
            wide = jnp.concatenate([out, jnp.zeros_like(out)], axis=1)
            vc_ref[0, 0] = wide.T[:HEAD_DIM].astype(vc_ref.dtype)


def _compress(xk, xv, cmp_pe, cmp_w1, cmp_w2, gk0, batch, seq):
    nc = seq // CMP_STRIDE
    grp = CMP_STRIDE * HEAD_DIM
    cos, sin = _rope_tables(jnp.arange(nc) * CMP_STRIDE + CMP_LEN - 1)
    pea = cmp_pe[:, :CMP_STRIDE].reshape(2, grp)
    peb = cmp_pe[:, CMP_STRIDE:].reshape(2, grp)
    blk = pl.BlockSpec((1, 1, nc, grp), lambda b, g: (b, g, 0, 0))
    const2 = lambda b, g: (0, 0)
    const3 = lambda b, g: (0, 0, 0)
    return pl.pallas_call(
        _compress_kernel,
        grid=(batch, NSA_KV),
        in_specs=[blk, blk,
                  pl.BlockSpec((2, grp), const2), pl.BlockSpec((2, grp), const2),
                  pl.BlockSpec(cmp_w1.shape, const3), pl.BlockSpec(cmp_w2.shape, const3),
                  pl.BlockSpec((1, HEAD_DIM), const2),
                  pl.BlockSpec((nc, HEAD_DIM), const2), pl.BlockSpec((nc, HEAD_DIM), const2),
                  pl.BlockSpec((HEAD_DIM, HEAD_DIM), const2)],
        out_specs=[pl.BlockSpec((1, 1, nc, HEAD_DIM), lambda b, g: (b, g, 0, 0)),
                   pl.BlockSpec((1, 1, HEAD_DIM, nc), lambda b, g: (b, g, 0, 0))],
        out_shape=[jax.ShapeDtypeStruct((batch, NSA_KV, nc, HEAD_DIM), BF16),
                   jax.ShapeDtypeStruct((batch, NSA_KV, HEAD_DIM, nc), BF16)],
        scratch_shapes=[pltpu.VMEM((nc + HALO, cmp_w1.shape[2]), F32)],
        compiler_params=pltpu.CompilerParams(dimension_semantics=("parallel", "parallel"),
                                             vmem_limit_bytes=VMEM_LIMIT),
        name="nsa_compress",
    )(xk, xv, pea, peb, cmp_w1.astype(BF16), cmp_w2.astype(BF16), gk0[None, :], cos, sin,
      jnp.asarray(_rot_matrix(1), BF16))


NSA_TQ = 256


def _overlap_t(seq):
    ns, nc = seq // SEL_BLOCK, seq // CMP_STRIDE
    cs = np.arange(nc)[None, :] * CMP_STRIDE
    ss = np.arange(ns)[:, None] * SEL_BLOCK
    ov = np.clip(np.minimum(cs + CMP_LEN, ss + SEL_BLOCK) - np.maximum(cs, ss), 0, None)
    return (ov / CMP_LEN).astype(np.float32)


def _cmpsel_kernel(q_ref, kc_ref, vc_ref, ovt_ref, ocmp_ref, bias_ref, val_scr, cnt_scr, *, top):
    tq, nc = q_ref.shape[3], kc_ref.shape[2]
    ns = ovt_ref.shape[0]
    t0 = pl.program_id(1) * tq
    c_i = lax.broadcasted_iota(jnp.int32, (nc, NSA_GROUP * tq), 0)
    q_i = lax.broadcasted_iota(jnp.int32, (nc, NSA_GROUP * tq), 1) & (tq - 1)
    cmask = c_i * CMP_STRIDE + (CMP_LEN - 1) <= t0 + q_i
    n_i = lax.broadcasted_iota(jnp.int32, (ns, tq), 0)
    t_i = t0 + lax.broadcasted_iota(jnp.int32, (ns, tq), 1)
    cur = t_i // SEL_BLOCK
    forced = jnp.where(n_i == 0, FORCE, jnp.where(n_i == cur, FORCE, jnp.where(n_i == cur - 1, FORCE, 0.0)))
    visible = n_i * SEL_BLOCK <= t_i
    ovt = ovt_ref[...]
    for g in range(NSA_KV):
        kc, vc = kc_ref[0, g], vc_ref[0, g]
        qg = jnp.concatenate([q_ref[0, g * NSA_GROUP + h] for h in range(NSA_GROUP)], axis=1)
        s = jnp.where(cmask, _dot(kc, qg), NEG)
        e = jnp.exp2(s - jnp.max(s, axis=0, keepdims=True))
        p = jnp.where(cmask, e * (1.0 / jnp.sum(e, axis=0, keepdims=True)), 0.0)
        o = _dot(vc, p.astype(BF16)).astype(ocmp_ref.dtype)
        psum = p[:, :tq]
        for h in range(NSA_GROUP):
            ocmp_ref[0, g * NSA_GROUP + h] = o[:, h * tq:(h + 1) * tq]
            if h:
                psum += p[:, h * tq:(h + 1) * tq]
        hi, lo = _split(psum)
        val_scr[...] = jnp.where(visible, _dot(ovt, hi) + _dot(ovt, lo) + forced, NEG)
        cnt_scr[...] = jnp.zeros_like(cnt_scr)
        sub = lax.broadcasted_iota(jnp.int32, (SUBLANES, tq), 0)
        quarter = max(ns // 4, SUBLANES)
        last_quarter = (t0 + tq - 1) // (SEL_BLOCK * quarter)
        for rq in range(ns // quarter):
            for mq in range(ns // quarter):
                @pl.when(last_quarter >= max(rq, mq))
                def _():
                    for r in range(rq * quarter // SUBLANES, (rq + 1) * quarter // SUBLANES):
                        rows = pl.ds(r * SUBLANES, SUBLANES)
                        v_r = val_scr[rows, :]
                        c = cnt_scr[rows, :]
                        for m in range(mq * quarter, (mq + 1) * quarter):
                            row = val_scr[m:m + 1, :]
                            if m // SUBLANES > r:
                                hit = row > v_r
                            elif m // SUBLANES < r:
                                hit = row >= v_r
                            else:
                                hit = jnp.where(sub > m % SUBLANES, jnp.where(row >= v_r, 1.0, 0.0),
                                                jnp.where(row > v_r, 1.0, 0.0)) > 0.5
                            c = jnp.where(hit, c + 1.0, c)
                        cnt_scr[rows, :] = c
        bias = jnp.where(cnt_scr[...] < top, 0.0, MASK_BIAS)
        if ns < HEAD_DIM:
            bias = jnp.concatenate([bias, jnp.zeros((HEAD_DIM - ns, tq), F32)], axis=0)
        bias_ref[0, g] = bias.astype(bias_ref.dtype)


def _cmpsel(qt, kc, vct, batch, seq):
    nt = seq // NSA_TQ
    nc, ns = seq // CMP_STRIDE, seq // SEL_BLOCK
    qblk = pl.BlockSpec((1, NSA_HEADS, HEAD_DIM, NSA_TQ), lambda b, t: (b, 0, 0, t))
    return pl.pallas_call(
        functools.partial(_cmpsel_kernel, top=min(SEL_TOPK, ns)),
        grid=(batch, nt),
        in_specs=[qblk,
                  pl.BlockSpec((1, NSA_KV, nc, HEAD_DIM), lambda b, t: (b, 0, 0, 0)),
                  pl.BlockSpec((1, NSA_KV, HEAD_DIM, nc), lambda b, t: (b, 0, 0, 0)),
                  pl.BlockSpec((ns, nc), lambda b, t: (0, 0))],
        out_specs=[qblk, pl.BlockSpec((1, NSA_KV, HEAD_DIM, NSA_TQ), lambda b, t: (b, 0, 0, t))],
        out_shape=[jax.ShapeDtypeStruct((batch, NSA_HEADS, HEAD_DIM, seq), BF16),
                   jax.ShapeDtypeStruct((batch, NSA_KV, HEAD_DIM, seq), BF16)],
        scratch_shapes=[pltpu.VMEM((ns, NSA_TQ), F32), pltpu.VMEM((ns, NSA_TQ), F32)],
        compiler_params=pltpu.CompilerParams(dimension_semantics=("parallel", "parallel"),
                                             vmem_limit_bytes=VMEM_LIMIT),
        name="nsa_cmpsel",
    )(qt, kc, vct, jnp.asarray(_overlap_t(seq), BF16))


def _attn_kernel(q_ref, ks_ref, vs_ref, kw_ref, vw_ref, bias_ref, ocmp_ref, gate_ref, gout_ref, o_ref,
                 m_scr, acc_scr, s0_scr, s1_scr):
    tq = q_ref.shape[3]
    i = pl.program_id(1)
    t0 = i * tq
    q_grp, q_sel = [], []
    for g in range(NSA_KV):
        q_grp.append(jnp.concatenate([q_ref[0, g * NSA_GROUP + h] for h in range(NSA_GROUP)], axis=1))
        q_sel.append(jnp.concatenate([q_grp[g], jnp.concatenate([bias_ref[0, g]] * NSA_GROUP, axis=1)], axis=0))

    def per_head(s, keep):
        return jnp.concatenate([jnp.where(keep, s[:, h * tq:(h + 1) * tq], NEG) for h in range(NSA_GROUP)], axis=1)

    m_scr[...] = jnp.full(m_scr.shape, NEG, F32)
    acc_scr[...] = jnp.zeros_like(acc_scr)

    def scores(j, s_scr):
        k0 = pl.multiple_of(j * tq, tq)
        for g in range(NSA_KV):
            s_scr[g] = _dot(ks_ref[0, g, pl.ds(k0, tq), :], q_sel[g])

    def consume(j, s_scr, diag):
        k0 = pl.multiple_of(j * tq, tq)
        for g in range(NSA_KV):
            s = s_scr[g]
            if diag:
                key_i = lax.broadcasted_iota(jnp.int32, (tq, tq), 0)
                s = per_head(s, key_i <= lax.broadcasted_iota(jnp.int32, (tq, tq), 1))
            m_old = m_scr[g]
            m_new = jnp.maximum(m_old, jnp.max(s, axis=0, keepdims=True))
            p = jnp.exp2(s - m_new).astype(BF16)
            acc_scr[g] = jnp.exp2(m_old - m_new) * acc_scr[g] + _dot(vs_ref[0, g, :, pl.ds(k0, tq)], p)
            m_scr[g] = m_new

    scores(0, s0_scr)

    def body(jj, carry):
        scores(2 * jj + 1, s1_scr)
        consume(2 * jj, s0_scr, False)
        scores(2 * jj + 2, s0_scr)
        consume(2 * jj + 1, s1_scr, False)
        return carry

    lax.fori_loop(0, i // 2, body, 0)

    @pl.when(i % 2 == 0)
    def _():
        consume(i, s0_scr, True)

    @pl.when(i % 2 == 1)
    def _():
        scores(i, s1_scr)
        consume(i - 1, s0_scr, False)
        consume(i, s1_scr, True)

    span = WINDOW + tq
    w0 = pl.multiple_of(jnp.maximum(t0 + tq - span, 0), tq)
    dist = (t0 - w0) + (lax.broadcasted_iota(jnp.int32, (span, tq), 1)
                        - lax.broadcasted_iota(jnp.int32, (span, tq), 0))
    in_window = jnp.abs(2 * dist - (WINDOW - 1)) < WINDOW
    win = []
    for g in range(NSA_KV):
        s = per_head(_dot(kw_ref[0, g, pl.ds(w0, span), :], q_grp[g]), in_window)
        p = jnp.exp2(s - jnp.max(s, axis=0, keepdims=True)).astype(BF16)
        win.append(_dot(vw_ref[0, g, :, pl.ds(w0, span)], p))

    gate = jax.nn.sigmoid(gate_ref[...].astype(F32).T)
    gout = gout_ref[...]
    rows = []
    for hh in range(NSA_HEADS):
        g, lanes = hh // NSA_GROUP, slice((hh % NSA_GROUP) * tq, (hh % NSA_GROUP + 1) * tq)
        a_s, a_w = acc_scr[g][:, lanes], win[g][:, lanes]
        o = (gate[hh:hh + 1] * ocmp_ref[0, hh].astype(F32)
             + gate[NSA_HEADS + hh:NSA_HEADS + hh + 1] * (a_s[:HEAD_DIM] * (1.0 / a_s[HEAD_DIM:HEAD_DIM + 1]))
             + gate[2 * NSA_HEADS + hh:2 * NSA_HEADS + hh + 1] * (a_w[:HEAD_DIM] * (1.0 / a_w[HEAD_DIM:HEAD_DIM + 1])))
        rows.append(o * lax.rsqrt(jnp.mean(o * o, axis=0, keepdims=True) + EPS) * gout)
    o_ref[...] = jnp.concatenate(rows, axis=0).T.astype(o_ref.dtype)


def _attn(qt, ks, vst, kw, vwt, bias, ocmp, proj, out_norm_g, batch, seq):
    nt = seq // NSA_TQ
    assert WINDOW % NSA_TQ == 0 and seq >= WINDOW + NSA_TQ
    rows_full = lambda d: pl.BlockSpec((1, NSA_KV, seq, d), lambda b, t: (b, 0, 0, 0))
    cols_full = pl.BlockSpec((1, NSA_KV, 2 * HEAD_DIM, seq), lambda b, t: (b, 0, 0, 0))
    qtile = lambda heads: pl.BlockSpec((1, heads, HEAD_DIM, NSA_TQ), lambda b, t: (b, 0, 0, t))
    return pl.pallas_call(
        _attn_kernel,
        grid=(batch, nt),
        in_specs=[qtile(NSA_HEADS), rows_full(2 * HEAD_DIM), cols_full, rows_full(HEAD_DIM), cols_full,
                  qtile(NSA_KV), qtile(NSA_HEADS),
                  pl.BlockSpec((NSA_TQ, LANES), lambda b, t: (b * nt + t, COL_GATE)),
                  pl.BlockSpec((HEAD_DIM, NSA_TQ), lambda b, t: (0, 0))],
        out_specs=pl.BlockSpec((NSA_TQ, NSA_WIDTH), lambda b, t: (b * nt + t, 0)),
        out_shape=jax.ShapeDtypeStruct((batch * seq, NSA_WIDTH), BF16),
        scratch_shapes=[pltpu.VMEM((NSA_KV, 1, NSA_GROUP * NSA_TQ), F32),
                        pltpu.VMEM((NSA_KV, 2 * HEAD_DIM, NSA_GROUP * NSA_TQ), F32),
                        pltpu.VMEM((NSA_KV, NSA_TQ, NSA_GROUP * NSA_TQ), F32),
                        pltpu.VMEM((NSA_KV, NSA_TQ, NSA_GROUP * NSA_TQ), F32)],
        compiler_params=pltpu.CompilerParams(dimension_semantics=("parallel", "parallel"),
                                             vmem_limit_bytes=VMEM_LIMIT),
        name="nsa_attn",
    )(qt, ks, vst, kw, vwt, bias, ocmp, proj, jnp.tile(out_norm_g[:, None], (1, NSA_TQ)))


def _nsa(proj, q_norm_g, k_norm_g, cmp_pe, cmp_w1, cmp_w2, out_norm_g, batch, seq):
    qt, ks, vst, kw, vwt = _nsa_prep(proj, q_norm_g, k_norm_g, batch, seq)

    def groups(col):
        a = proj[:, col * LANES:(col + 1) * LANES].reshape(batch, seq, NSA_KV, HEAD_DIM)
        return a.transpose(0, 2, 1, 3).reshape(batch, NSA_KV, seq // CMP_STRIDE, CMP_STRIDE * HEAD_DIM)

    kc, vct = _compress(groups(COL_KC), groups(COL_VC), cmp_pe, cmp_w1, cmp_w2, k_norm_g[0], batch, seq)
    ocmp, bias = _cmpsel(qt, kc, vct, batch, seq)
    return _attn(qt, ks, vst, kw, vwt, bias, ocmp, proj, out_norm_g, batch, seq)


def kernel(x, p, attn_norm_g, w_in, hg_lb_logits, hg_norm_g, nsa_q_norm_g, nsa_k_norm_g, cmp_pe,
           cmp_w1, cmp_w2, nsa_out_norm_g, w_out, ffn_norm_g, w_up, conv_w, conv_b, w_down,
           ple_gate_norm_g, w_ple_gate, w_ple, ple_norm_g):
    B, T, _ = x.shape
    n = B * T
    h = x.reshape(n, D_MODEL)
    for i in range(w_in.shape[0]):
        w_pad = jnp.pad(w_in[i], ((0, 0), (0, IN_PAD - IN_TOTAL))).astype(BF16)
        proj = _in_proj(h, attn_norm_g[i][None, :], w_pad, BF16)
        o_hg = _hgrn(proj, hg_lb_logits.astype(F32), hg_norm_g[i][None, :], i, B, T)
        o_nsa = _nsa(proj, nsa_q_norm_g[i], nsa_k_norm_g[i], cmp_pe[i], cmp_w1[i], cmp_w2[i],
                     nsa_out_norm_g[i], B, T)
        w_o = w_out[i].astype(BF16)
        h1 = _out_proj(h, o_hg, o_nsa, w_o[:HG_WIDTH], w_o[HG_WIDTH:])
        h = _ffn_ple(h1, p[i].reshape(n, PLE_DIM), ffn_norm_g[i][None, :], w_up[i].astype(BF16),
                     conv_w[i], conv_b[i][None, :], w_down[i].astype(BF16),
                     ple_gate_norm_g[i][None, :], w_ple_gate[i].astype(BF16), w_ple[i].astype(BF16),
                     ple_norm_g[i][None, :], T)
    return h.reshape(B, T, D_MODEL)
```

```python
import functools

import numpy as np
import jax
import jax.numpy as jnp
from jax import lax
from jax.experimental import pallas as pl
from jax.experimental.pallas import tpu as pltpu

F32 = jnp.float32
BF16 = jnp.bfloat16

D_MODEL = 1024
PLE_DIM = 256
EPS = 1e-6
NEG = -1e30
FORCE = 1e6

HG_HEADS = 4
HG_KDIM = 128
HG_VDIM = 128
HG_WIDTH = HG_HEADS * HG_VDIM
HG_CHUNK = 64

NSA_HEADS = 8
NSA_KV = 2
NSA_GROUP = NSA_HEADS // NSA_KV
HEAD_DIM = 64
NSA_WIDTH = NSA_HEADS * HEAD_DIM
KV_WIDTH = NSA_KV * HEAD_DIM
CMP_LEN = 32
CMP_STRIDE = 16
SEL_BLOCK = 64
SEL_TOPK = 16
WINDOW = 512
ROT_DIM = HEAD_DIM // 4
ROPE_THETA = 500000.0
D_FF = 2816
CONV_W = 3

IN_SIZES = (HG_HEADS * HG_KDIM, HG_HEADS * HG_KDIM, HG_WIDTH, HG_WIDTH, NSA_WIDTH,
            KV_WIDTH, KV_WIDTH, KV_WIDTH, KV_WIDTH, KV_WIDTH, KV_WIDTH, 3 * NSA_HEADS)
IN_TOTAL = sum(IN_SIZES)
LANES = 128
IN_PAD = -(-IN_TOTAL // LANES) * LANES
VMEM_LIMIT = 56 * 1024 * 1024

ROW_TILE = 512
FF_CHUNK = 256
SUBLANES = 8
HALO = SUBLANES
FFN_HALO = 2 * SUBLANES


def _rms(x, g):
    return x * lax.rsqrt(jnp.mean(x * x, axis=-1, keepdims=True) + EPS) * g


def _dot(a, b):
    return jnp.dot(a, b, preferred_element_type=F32)


def _dot_nt(a, b):
    return lax.dot_general(a, b, (((1,), (1,)), ((), ())), preferred_element_type=F32)


def _dot_tn(a, b):
    return lax.dot_general(a, b, (((0,), (0,)), ((), ())), preferred_element_type=F32)


def _split(x):
    hi = x.astype(BF16)
    return hi, (x - hi.astype(F32)).astype(BF16)


def _silu(x):
    hx = 0.5 * x
    return hx + hx * jnp.tanh(hx)


def _out_proj_kernel(x_ref, a_ref, b_ref, wa_ref, wb_ref, o_ref):
    acc = _dot(a_ref[...], wa_ref[...])
    acc += _dot(b_ref[...], wb_ref[...])
    o_ref[...] = x_ref[...] + acc


def _out_proj(x2, o_hg, o_nsa, w_a, w_b):
    n = x2.shape[0]
    return pl.pallas_call(
        _out_proj_kernel,
        grid=(n // ROW_TILE,),
        in_specs=[pl.BlockSpec((ROW_TILE, D_MODEL), lambda i: (i, 0)),
                  pl.BlockSpec((ROW_TILE, HG_WIDTH), lambda i: (i, 0)),
                  pl.BlockSpec((ROW_TILE, NSA_WIDTH), lambda i: (i, 0)),
                  pl.BlockSpec((HG_WIDTH, D_MODEL), lambda i: (0, 0)),
                  pl.BlockSpec((NSA_WIDTH, D_MODEL), lambda i: (0, 0))],
        out_specs=pl.BlockSpec((ROW_TILE, D_MODEL), lambda i: (i, 0)),
        out_shape=jax.ShapeDtypeStruct((n, D_MODEL), F32),
        compiler_params=pltpu.CompilerParams(dimension_semantics=("parallel",),
                                             vmem_limit_bytes=VMEM_LIMIT),
        name="out_proj",
    )(x2, o_hg, o_nsa, w_a, w_b)


def _ffn_ple_kernel(h_ref, halo_ref, p_ref, gf_ref, wup_ref, cw_ref, cb_ref, wdn_ref,
                    gg_ref, wg_ref, wp_ref, gp_ref, o_ref, hn_scr, ug_scr, uu_scr, act_scr, *, tiles_per_seq):
    h = h_ref[...]
    rows = h.shape[0]
    first = (pl.program_id(0) % tiles_per_seq) == 0
    hn_scr[0:FFN_HALO, :] = jnp.where(first, 0.0, _rms(halo_ref[...], gf_ref[...])).astype(BF16)
    hn_scr[FFN_HALO:FFN_HALO + rows, :] = _rms(h, gf_ref[...]).astype(BF16)
    for c in range(0, D_FF, FF_CHUNK):
        conv = []
        for scr, off in ((ug_scr, c), (uu_scr, D_FF + c)):
            scr[...] = _dot(hn_scr[...], wup_ref[:, off:off + FF_CHUNK])
            cw = cw_ref[:, off:off + FF_CHUNK]
            conv.append(scr[FFN_HALO - 2:FFN_HALO - 2 + rows, :] * cw[0:1, :]
                        + scr[FFN_HALO - 1:FFN_HALO - 1 + rows, :] * cw[1:2, :]
                        + scr[FFN_HALO:FFN_HALO + rows, :] * cw[2:3, :]
                        + cb_ref[:, off:off + FF_CHUNK])
        gate, up = conv
        act_scr[:, c:c + FF_CHUNK] = (_silu(gate) * up).astype(BF16)
    o_ref[...] = h_ref[...] + _dot(act_scr[...], wdn_ref[...])
    hn_scr[FFN_HALO:FFN_HALO + rows, :] = _rms(o_ref[...], gg_ref[...]).astype(BF16)
    e = _rms(_dot(p_ref[...].astype(BF16), wp_ref[...]), gp_ref[...])
    gate = jax.nn.sigmoid(_dot(hn_scr[FFN_HALO:FFN_HALO + rows, :], wg_ref[...]))
    o_ref[...] = o_ref[...] + gate * e


def _ffn_ple(h1, p2, gf, w_up, conv_w, conv_b, w_down, gg, w_gate, w_ple, gp, seq):
    n = h1.shape[0]
    const = lambda i: (0, 0)
    halo_blocks = ROW_TILE // FFN_HALO
    return pl.pallas_call(
        functools.partial(_ffn_ple_kernel, tiles_per_seq=seq // ROW_TILE),
        grid=(n // ROW_TILE,),
        in_specs=[pl.BlockSpec((ROW_TILE, D_MODEL), lambda i: (i, 0)),
                  pl.BlockSpec((FFN_HALO, D_MODEL), lambda i: (jnp.maximum(i * halo_blocks - 1, 0), 0)),
                  pl.BlockSpec((ROW_TILE, PLE_DIM), lambda i: (i, 0)),
                  pl.BlockSpec((1, D_MODEL), const),
                  pl.BlockSpec((D_MODEL, 2 * D_FF), const, pipeline_mode=pl.Buffered(1)),
                  pl.BlockSpec((CONV_W, 2 * D_FF), const),
                  pl.BlockSpec((1, 2 * D_FF), const),
                  pl.BlockSpec((D_FF, D_MODEL), const, pipeline_mode=pl.Buffered(1)),
                  pl.BlockSpec((1, D_MODEL), const),
                  pl.BlockSpec((D_MODEL, D_MODEL), const, pipeline_mode=pl.Buffered(1)),
                  pl.BlockSpec((PLE_DIM, D_MODEL), const, pipeline_mode=pl.Buffered(1)),
                  pl.BlockSpec((1, D_MODEL), const)],
        out_specs=pl.BlockSpec((ROW_TILE, D_MODEL), lambda i: (i, 0)),
        out_shape=jax.ShapeDtypeStruct((n, D_MODEL), F32),
        scratch_shapes=[pltpu.VMEM((FFN_HALO + ROW_TILE, D_MODEL), BF16),
                        pltpu.VMEM((FFN_HALO + ROW_TILE, FF_CHUNK), F32),
                        pltpu.VMEM((FFN_HALO + ROW_TILE, FF_CHUNK), F32),
                        pltpu.VMEM((ROW_TILE, D_FF), BF16)],
        compiler_params=pltpu.CompilerParams(dimension_semantics=("parallel",),
                                             vmem_limit_bytes=VMEM_LIMIT),
        name="ffn_ple",
    )(h1, h1, p2, gf, w_up, conv_w, conv_b, w_down, gg, w_gate, w_ple, gp)


HG_LEVELS = (32, 16, 8, 4, 2, 1)
HG_TB = 256


def _hgrn_level_matrices():
    c = HG_CHUNK
    r = np.arange(c)[:, None]
    u = np.arange(c)[None, :]
    rows = []
    for m in HG_LEVELS:
        r0 = (r // m) * m
        upper = (r & m) != 0
        rows.append(np.where(upper, (u >= r0) & (u <= r), (u >= r + 1) & (u <= r0 + m - 1)))
    rows.append(u <= r)
    rows.append(u > r)
    return np.concatenate(rows, 0).astype(np.float32)


def _hgrn_pair_masks():
    c = HG_CHUNK
    t = np.arange(c)[:, None]
    s = np.arange(c)[None, :]
    masks = [((t // (2 * m)) == (s // (2 * m))) & ((t & m) != 0) & ((s & m) == 0) for m in HG_LEVELS]
    masks.append(t == s)
    return np.stack(masks).astype(np.float32)


def _hgrn_kernel(q_ref, f_ref, v_ref, g_ref, lbl_ref, ng_ref, mall_ref, msk_ref, o_ref, st_ref,
                 *, layer, n_chunks):
    @pl.when(pl.program_id(1) == 0)
    def _():
        st_ref[...] = jnp.zeros_like(st_ref)

    lbl = lbl_ref[...]
    e = jnp.exp(lbl - jnp.max(lbl, axis=0, keepdims=True))
    sm = e / jnp.sum(e, axis=0, keepdims=True)
    lb = jnp.sum(sm[:layer + 1], axis=0, keepdims=True)
    mall = mall_ref[...]
    ng = ng_ref[...]
    c = HG_CHUNK
    nl = len(HG_LEVELS)

    intra, q_decayed, carry_decay, carry_add = {}, {}, {}, {}
    for ci in range(n_chunks):
        r0 = ci * c
        fg = lb + (1.0 - lb) * jax.nn.sigmoid(f_ref[r0:r0 + c, :].astype(F32))
        kk = 1.0 - fg
        ee = jnp.exp(_dot(mall, jnp.concatenate(_split(jnp.log(fg)), axis=0)))
        qf = _silu(q_ref[r0:r0 + c, :].astype(F32)) * HG_KDIM ** -0.5
        for h in range(HG_HEADS):
            sl = slice(h * HG_KDIM, (h + 1) * HG_KDIM)
            qh, kh = qf[:, sl], kk[:, sl]
            a = msk_ref[nl] * _dot_nt(qh.astype(BF16), kh.astype(BF16))
            for j in range(nl):
                ej = ee[j * c:(j + 1) * c, sl]
                a += msk_ref[j] * _dot_nt((qh * ej).astype(BF16), (kh * ej).astype(BF16))
            vh = v_ref[r0:r0 + c, sl]
            eb = ee[nl * c:(nl + 1) * c, sl]
            intra[ci, h] = _dot(a.astype(BF16), vh)
            q_decayed[ci, h] = (qh * eb).astype(BF16)
            carry_decay[ci, h] = eb[c - 1:c, :]
            carry_add[ci, h] = _dot_tn(vh, (kh * ee[(nl + 1) * c:(nl + 2) * c, sl]).astype(BF16))

    for h in range(HG_HEADS):
        sl = slice(h * HG_KDIM, (h + 1) * HG_KDIM)
        st = st_ref[h]
        states = []
        for ci in range(n_chunks):
            states.append(st.astype(BF16))
            st = carry_decay[ci, h] * st + carry_add[ci, h]
        st_ref[h] = st
        for ci in range(n_chunks):
            r0 = ci * c
            o = intra[ci, h] + _dot_nt(q_decayed[ci, h], states[ci])
            on = o * lax.rsqrt(jnp.mean(o * o, axis=-1, keepdims=True) + EPS) * ng
            o_ref[r0:r0 + c, sl] = (on * _silu(g_ref[r0:r0 + c, sl].astype(F32))).astype(o_ref.dtype)


def _hgrn(proj, lb_logits, norm_g, layer, batch, seq):
    nt = seq // HG_TB
    w = HG_HEADS * HG_KDIM
    col = lambda k: (lambda b, t: (b * nt + t, k))
    const2 = lambda b, t: (0, 0)
    return pl.pallas_call(
        functools.partial(_hgrn_kernel, layer=layer, n_chunks=HG_TB // HG_CHUNK),
        grid=(batch, nt),
        in_specs=[pl.BlockSpec((HG_TB, w), col(0)), pl.BlockSpec((HG_TB, w), col(1)),
                  pl.BlockSpec((HG_TB, w), col(2)), pl.BlockSpec((HG_TB, w), col(3)),
                  pl.BlockSpec(lb_logits.shape, const2),
                  pl.BlockSpec((1, HG_VDIM), const2),
                  pl.BlockSpec(((len(HG_LEVELS) + 2) * HG_CHUNK, 2 * HG_CHUNK), const2),
                  pl.BlockSpec((len(HG_LEVELS) + 1, HG_CHUNK, HG_CHUNK), lambda b, t: (0, 0, 0))],
        out_specs=pl.BlockSpec((HG_TB, w), lambda b, t: (b * nt + t, 0)),
        out_shape=jax.ShapeDtypeStruct((batch * seq, HG_WIDTH), BF16),
        scratch_shapes=[pltpu.VMEM((HG_HEADS, HG_VDIM, HG_KDIM), F32)],
        compiler_params=pltpu.CompilerParams(dimension_semantics=("parallel", "arbitrary"),
                                             vmem_limit_bytes=VMEM_LIMIT),
        name="hgrn2",
    )(proj, proj, proj, proj, lb_logits, norm_g,
      jnp.asarray(np.tile(_hgrn_level_matrices(), (1, 2)), BF16), jnp.asarray(_hgrn_pair_masks(), F32))


COL_NQ = 2048 // LANES
COL_KC, COL_VC, COL_KS, COL_VS, COL_KW, COL_VW, COL_GATE = (COL_NQ + 4 + k for k in range(7))
LOG2E = 1.4426950408889634
MASK_BIAS = -1e30
V_ROWS = HEAD_DIM + 2 * SUBLANES


def _rope_tables(pos):
    half = ROT_DIM // 2
    inv = np.float32(ROPE_THETA) ** (-np.arange(half, dtype=np.float32) / np.float32(half))
    ang = pos.astype(np.float32)[:, None] * inv[None, :]
    n = pos.shape[0]
    pad = HEAD_DIM - ROT_DIM
    cos = np.concatenate([np.cos(ang), np.cos(ang), np.ones((n, pad), np.float32)], axis=1)
    sin = np.concatenate([np.sin(ang), np.sin(ang), np.zeros((n, pad), np.float32)], axis=1)
    return cos.astype(np.float32), sin.astype(np.float32)


def _rot_matrix(heads):
    half = ROT_DIM // 2
    r = np.zeros((HEAD_DIM, HEAD_DIM), np.float32)
    for j in range(half):
        r[j + half, j] = -1.0
        r[j, j + half] = 1.0
    return np.kron(np.eye(heads, dtype=np.float32), r)


def _norm_rope_t(xt, gain, cos, sin):
    half = ROT_DIM // 2
    xn = xt * lax.rsqrt(jnp.mean(xt * xt, axis=0, keepdims=True) + EPS) * gain
    x1, x2 = xn[:half], xn[half:ROT_DIM]
    return jnp.concatenate([x1 * cos - x2 * sin, x2 * cos + x1 * sin, xn[ROT_DIM:]], axis=0)


def _in_proj_kernel(x_ref, g_ref, w_ref, cos_ref, sin_ref, gq_ref, gk_ref,
                    o_ref, qo_ref, kso_ref, vso_ref, kwo_ref, vwo_ref, *, tiles_per_seq):
    tb = x_ref.shape[0]
    xb = _rms(x_ref[...], g_ref[...]).astype(BF16)
    hg_cols = COL_NQ * LANES
    qp = _dot(xb, w_ref[:, hg_cols:hg_cols + NSA_WIDTH])
    o_ref[:, hg_cols:hg_cols + NSA_WIDTH] = qp.astype(o_ref.dtype)
    kv = _dot(xb, w_ref[:, COL_KC * LANES:])
    o_ref[:, COL_KC * LANES:] = kv.astype(o_ref.dtype)
    for c in range(0, hg_cols, hg_cols // 2):
        o_ref[:, c:c + hg_cols // 2] = _dot(xb, w_ref[:, c:c + hg_cols // 2]).astype(o_ref.dtype)
    part_t = lambda col: kv[:, (col - COL_KC) * LANES:(col - COL_KC + 1) * LANES].T

    cos, sin = cos_ref[...], sin_ref[...]
    qt = qp.T
    for h in range(NSA_HEADS):
        qo_ref[0, h] = _norm_rope_t(qt[h * HEAD_DIM:(h + 1) * HEAD_DIM], gq_ref[...], cos, sin).astype(qo_ref.dtype)

    t0 = (pl.program_id(0) % tiles_per_seq) * tb
    block = (t0 + lax.broadcasted_iota(jnp.int32, (HEAD_DIM, tb), 1)) // SEL_BLOCK
    onehot_t = jnp.where(lax.broadcasted_iota(jnp.int32, (HEAD_DIM, tb), 0) == block, 1.0, 0.0)
    row_ones = jnp.ones((V_ROWS - HEAD_DIM, tb), F32)
    for j, (kcol, vcol, vo_ref) in enumerate(((COL_KS, COL_VS, vso_ref), (COL_KW, COL_VW, vwo_ref))):
        kt, vt = part_t(kcol), part_t(vcol)
        heads = [_norm_rope_t(kt[g * HEAD_DIM:(g + 1) * HEAD_DIM], gk_ref[j], cos, sin) for g in range(NSA_KV)]
        if j == 0:
            for g in range(NSA_KV):
                kso_ref[0, g] = jnp.concatenate([heads[g], onehot_t], axis=0).T.astype(kso_ref.dtype)
        else:
            both = jnp.concatenate(heads, axis=0).T
            kwo_ref[0, 0] = both[:, :HEAD_DIM].astype(kwo_ref.dtype)
            kwo_ref[0, 1] = pltpu.roll(both, HEAD_DIM, 1)[:, :HEAD_DIM].astype(kwo_ref.dtype)
        for g in range(NSA_KV):
            vo_ref[0, g] = jnp.concatenate([vt[g * HEAD_DIM:(g + 1) * HEAD_DIM], row_ones], axis=0).astype(vo_ref.dtype)


def _in_proj(x2, g, w_pad, q_norm_g, k_norm_g, batch, seq):
    n = x2.shape[0]
    nt = seq // ROW_TILE
    assert seq // SEL_BLOCK <= HEAD_DIM
    cos, sin = _rope_tables(np.arange(seq))
    half = ROT_DIM // 2
    cos_t, sin_t = jnp.asarray(cos[:, :half].T.copy()), jnp.asarray(sin[:, :half].T.copy())
    lanes = lambda gain: jnp.tile(gain[..., None], (1,) * gain.ndim + (ROW_TILE,))
    gq = lanes(q_norm_g * (HEAD_DIM ** -0.5 * LOG2E))
    gk = lanes(k_norm_g[1:3])
    const = lambda i: (0, 0)
    rows_major = lambda d: pl.BlockSpec((1, NSA_KV, ROW_TILE, d), lambda i: (i // nt, 0, i % nt, 0))
    cols_major = lambda heads, d: pl.BlockSpec((1, heads, d, ROW_TILE), lambda i: (i // nt, 0, 0, i % nt))
    sds = lambda *shape: jax.ShapeDtypeStruct((batch,) + shape, BF16)
    return pl.pallas_call(
        functools.partial(_in_proj_kernel, tiles_per_seq=nt),
        grid=(n // ROW_TILE,),
        in_specs=[pl.BlockSpec((ROW_TILE, D_MODEL), lambda i: (i, 0)),
                  pl.BlockSpec((1, D_MODEL), const),
                  pl.BlockSpec((D_MODEL, IN_PAD), const),
                  pl.BlockSpec((half, ROW_TILE), lambda i: (0, i % nt)),
                  pl.BlockSpec((half, ROW_TILE), lambda i: (0, i % nt)),
                  pl.BlockSpec((HEAD_DIM, ROW_TILE), const),
                  pl.BlockSpec((2, HEAD_DIM, ROW_TILE), lambda i: (0, 0, 0))],
        out_specs=[pl.BlockSpec((ROW_TILE, IN_PAD), lambda i: (i, 0)),
                   cols_major(NSA_HEADS, HEAD_DIM), rows_major(2 * HEAD_DIM), cols_major(NSA_KV, V_ROWS),
                   rows_major(HEAD_DIM), cols_major(NSA_KV, V_ROWS)],
        out_shape=[jax.ShapeDtypeStruct((n, IN_PAD), BF16),
                   sds(NSA_HEADS, HEAD_DIM, seq), sds(NSA_KV, seq, 2 * HEAD_DIM), sds(NSA_KV, V_ROWS, seq),
                   sds(NSA_KV, seq, HEAD_DIM), sds(NSA_KV, V_ROWS, seq)],
        compiler_params=pltpu.CompilerParams(dimension_semantics=("parallel",),
                                             vmem_limit_bytes=VMEM_LIMIT),
        name="in_proj",
    )(x2, g, w_pad, cos_t, sin_t, gq, gk)


def _compress_kernel(xk_ref, xv_ref, pea_ref, peb_ref, w1_ref, w2_ref, gk_ref, cos_ref, sin_ref,
                     rot_ref, kc_ref, vc_ref, shift_scr):
    nc = xk_ref.shape[2]
    half = w1_ref.shape[1] // 2
    for j, x_ref in enumerate((xk_ref, xv_ref)):
        x = x_ref[0, 0].astype(F32)
        u = _dot((x + pea_ref[j:j + 1, :]).astype(BF16), w1_ref[j, :half, :])
        v = _dot((x + peb_ref[j:j + 1, :]).astype(BF16), w1_ref[j, half:, :])
        shift_scr[0:nc, :] = v
        shift_scr[nc:nc + HALO, :] = jnp.zeros((HALO, v.shape[1]), F32)
        hid = _silu(u + shift_scr[1:nc + 1, :])
        out = _dot(hid.astype(BF16), w2_ref[j])
        if j == 0:
            xn = _rms(out, gk_ref[...])
            out = xn * cos_ref[...] + _dot(xn.astype(BF16), rot_ref[...]) * sin_ref[...]
            kc_ref[0, 0] = out.astype(kc_ref.dtype)
        else:
            wide = jnp.concatenate([out, jnp.zeros_like(out)], axis=1)
            vc_ref[0, 0] = wide.T[:HEAD_DIM].astype(vc_ref.dtype)


def _compress(xk, xv, cmp_pe, cmp_w1, cmp_w2, gk0, batch, seq):
    nc = seq // CMP_STRIDE
    grp = CMP_STRIDE * HEAD_DIM
    cos, sin = (jnp.asarray(t) for t in _rope_tables(np.arange(nc) * CMP_STRIDE + CMP_LEN - 1))
    pea = cmp_pe[:, :CMP_STRIDE].reshape(2, grp)
    peb = cmp_pe[:, CMP_STRIDE:].reshape(2, grp)
    blk = pl.BlockSpec((1, 1, nc, grp), lambda b, g: (b, g, 0, 0))
    const2 = lambda b, g: (0, 0)
    const3 = lambda b, g: (0, 0, 0)
    return pl.pallas_call(
        _compress_kernel,
        grid=(batch, NSA_KV),
        in_specs=[blk, blk,
                  pl.BlockSpec((2, grp), const2), pl.BlockSpec((2, grp), const2),
                  pl.BlockSpec(cmp_w1.shape, const3), pl.BlockSpec(cmp_w2.shape, const3),
                  pl.BlockSpec((1, HEAD_DIM), const2),
                  pl.BlockSpec((nc, HEAD_DIM), const2), pl.BlockSpec((nc, HEAD_DIM), const2),
                  pl.BlockSpec((HEAD_DIM, HEAD_DIM), const2)],
        out_specs=[pl.BlockSpec((1, 1, nc, HEAD_DIM), lambda b, g: (b, g, 0, 0)),
                   pl.BlockSpec((1, 1, HEAD_DIM, nc), lambda b, g: (b, g, 0, 0))],
        out_shape=[jax.ShapeDtypeStruct((batch, NSA_KV, nc, HEAD_DIM), BF16),
                   jax.ShapeDtypeStruct((batch, NSA_KV, HEAD_DIM, nc), BF16)],
        scratch_shapes=[pltpu.VMEM((nc + HALO, cmp_w1.shape[2]), F32)],
        compiler_params=pltpu.CompilerParams(dimension_semantics=("parallel", "parallel"),
                                             vmem_limit_bytes=VMEM_LIMIT),
        name="nsa_compress",
    )(xk, xv, pea, peb, cmp_w1.astype(BF16), cmp_w2.astype(BF16), gk0[None, :], cos, sin,
      jnp.asarray(_rot_matrix(1), BF16))


NSA_TQ = 256


def _overlap_t(seq):
    ns, nc = seq // SEL_BLOCK, seq // CMP_STRIDE
    cs = np.arange(nc)[None, :] * CMP_STRIDE
    ss = np.arange(ns)[:, None] * SEL_BLOCK
    ov = np.clip(np.minimum(cs + CMP_LEN, ss + SEL_BLOCK) - np.maximum(cs, ss), 0, None)
    return (ov / CMP_LEN).astype(np.float32)


def _cmpsel_kernel(q_ref, kc_ref, vc_ref, ovt_ref, ocmp_ref, bias_ref, val_scr, cnt_scr, *, top):
    tq, nc = q_ref.shape[3], kc_ref.shape[2]
    ns = ovt_ref.shape[0]
    t0 = pl.program_id(1) * tq
    n_i = lax.broadcasted_iota(jnp.int32, (ns, tq), 0)
    t_i = t0 + lax.broadcasted_iota(jnp.int32, (ns, tq), 1)
    cur = t_i // SEL_BLOCK
    forced = jnp.where(n_i == 0, FORCE, jnp.where(n_i == cur, FORCE, jnp.where(n_i == cur - 1, FORCE, 0.0)))
    visible = n_i * SEL_BLOCK <= t_i

    def attend(g, ncv):
        c_i = lax.broadcasted_iota(jnp.int32, (ncv, NSA_GROUP * tq), 0)
        q_i = lax.broadcasted_iota(jnp.int32, (ncv, NSA_GROUP * tq), 1) & (tq - 1)
        cmask = c_i * CMP_STRIDE + (CMP_LEN - 1) <= t0 + q_i
        qg = jnp.concatenate([q_ref[0, g * NSA_GROUP + h] for h in range(NSA_GROUP)], axis=1)
        s = jnp.where(cmask, _dot(kc_ref[0, g, 0:ncv, :], qg), NEG)
        e = jnp.exp2(s - jnp.max(s, axis=0, keepdims=True))
        p = jnp.where(cmask, e * (1.0 / jnp.sum(e, axis=0, keepdims=True)), 0.0)
        o = _dot(vc_ref[0, g, :, 0:ncv], p.astype(BF16)).astype(ocmp_ref.dtype)
        psum = p[:, :tq]
        for h in range(NSA_GROUP):
            ocmp_ref[0, g * NSA_GROUP + h] = o[:, h * tq:(h + 1) * tq]
            if h:
                psum += p[:, h * tq:(h + 1) * tq]
        ovt = ovt_ref[:, 0:ncv]
        hi, lo = _split(psum)
        val_scr[...] = jnp.where(visible, _dot(ovt, hi) + _dot(ovt, lo) + forced, NEG)

    nc_quarter = nc // 4
    prefix = ((t0 + tq) // CMP_STRIDE - 1) // nc_quarter
    for g in range(NSA_KV):
        for v in range(4):
            @pl.when(prefix == v)
            def _():
                attend(g, (v + 1) * nc_quarter)
        cnt_scr[...] = jnp.zeros_like(cnt_scr)
        sub = lax.broadcasted_iota(jnp.int32, (SUBLANES, tq), 0)
        quarter = max(ns // 4, SUBLANES)
        last_quarter = (t0 + tq - 1) // (SEL_BLOCK * quarter)
        for rq in range(ns // quarter):
            for mq in range(ns // quarter):
                @pl.when(last_quarter >= max(rq, mq))
                def _():
                    for r in range(rq * quarter // SUBLANES, (rq + 1) * quarter // SUBLANES):
                        rows = pl.ds(r * SUBLANES, SUBLANES)
                        v_r = val_scr[rows, :]
                        c = cnt_scr[rows, :]
                        for m in range(mq * quarter, (mq + 1) * quarter):
                            row = val_scr[m:m + 1, :]
                            if m // SUBLANES > r:
                                hit = row > v_r
                            elif m // SUBLANES < r:
                                hit = row >= v_r
                            else:
                                hit = jnp.where(sub > m % SUBLANES, jnp.where(row >= v_r, 1.0, 0.0),
                                                jnp.where(row > v_r, 1.0, 0.0)) > 0.5
                            c = jnp.where(hit, c + 1.0, c)
                        cnt_scr[rows, :] = c
        bias = jnp.where(cnt_scr[...] < top, 0.0, MASK_BIAS)
        if ns < HEAD_DIM:
            bias = jnp.concatenate([bias, jnp.zeros((HEAD_DIM - ns, tq), F32)], axis=0)
        bias_ref[0, g] = bias.astype(bias_ref.dtype)


def _cmpsel(qt, kc, vct, batch, seq):
    nt = seq // NSA_TQ
    nc, ns = seq // CMP_STRIDE, seq // SEL_BLOCK
    qblk = pl.BlockSpec((1, NSA_HEADS, HEAD_DIM, NSA_TQ), lambda b, t: (b, 0, 0, t))
    return pl.pallas_call(
        functools.partial(_cmpsel_kernel, top=min(SEL_TOPK, ns)),
        grid=(batch, nt),
        in_specs=[qblk,
                  pl.BlockSpec((1, NSA_KV, nc, HEAD_DIM), lambda b, t: (b, 0, 0, 0)),
                  pl.BlockSpec((1, NSA_KV, HEAD_DIM, nc), lambda b, t: (b, 0, 0, 0)),
                  pl.BlockSpec((ns, nc), lambda b, t: (0, 0))],
        out_specs=[qblk, pl.BlockSpec((1, NSA_KV, HEAD_DIM, NSA_TQ), lambda b, t: (b, 0, 0, t))],
        out_shape=[jax.ShapeDtypeStruct((batch, NSA_HEADS, HEAD_DIM, seq), BF16),
                   jax.ShapeDtypeStruct((batch, NSA_KV, HEAD_DIM, seq), BF16)],
        scratch_shapes=[pltpu.VMEM((ns, NSA_TQ), F32), pltpu.VMEM((ns, NSA_TQ), F32)],
        compiler_params=pltpu.CompilerParams(dimension_semantics=("parallel", "parallel"),
                                             vmem_limit_bytes=VMEM_LIMIT),
        name="nsa_cmpsel",
    )(qt, kc, vct, jnp.asarray(_overlap_t(seq), BF16))


def _attn_kernel(q_ref, ks_ref, vs_ref, kw_ref, vw_ref, bias_ref, ocmp_ref, gate_ref, gout_ref, o_ref,
                 m_scr, acc_scr, s0_scr, s1_scr):
    tq = q_ref.shape[3]
    i = pl.program_id(1)
    t0 = i * tq
    q_grp, q_sel = [], []
    for g in range(NSA_KV):
        q_grp.append(jnp.concatenate([q_ref[0, g * NSA_GROUP + h] for h in range(NSA_GROUP)], axis=1))
        q_sel.append(jnp.concatenate([q_grp[g], jnp.concatenate([bias_ref[0, g]] * NSA_GROUP, axis=1)], axis=0))

    def per_head(s, keep):
        return jnp.concatenate([jnp.where(keep, s[:, h * tq:(h + 1) * tq], NEG) for h in range(NSA_GROUP)], axis=1)

    m_scr[...] = jnp.full(m_scr.shape, NEG, F32)
    acc_scr[...] = jnp.zeros_like(acc_scr)

    def scores(j, s_scr):
        k0 = pl.multiple_of(j * tq, tq)
        for g in range(NSA_KV):
            s_scr[g] = _dot(ks_ref[0, g, pl.ds(k0, tq), :], q_sel[g])

    def consume(j, s_scr, diag):
        k0 = pl.multiple_of(j * tq, tq)
        for g in range(NSA_KV):
            s = s_scr[g]
            if diag:
                key_i = lax.broadcasted_iota(jnp.int32, (tq, tq), 0)
                s = per_head(s, key_i <= lax.broadcasted_iota(jnp.int32, (tq, tq), 1))
            m_old = m_scr[g]
            m_new = jnp.maximum(m_old, jnp.max(s, axis=0, keepdims=True))
            p = jnp.exp2(s - m_new).astype(BF16)
            acc_scr[g] = jnp.exp2(m_old - m_new) * acc_scr[g] + _dot(vs_ref[0, g, :, pl.ds(k0, tq)], p)
            m_scr[g] = m_new

    scores(0, s0_scr)

    def body(jj, carry):
        scores(2 * jj + 1, s1_scr)
        consume(2 * jj, s0_scr, False)
        scores(2 * jj + 2, s0_scr)
        consume(2 * jj + 1, s1_scr, False)
        return carry

    lax.fori_loop(0, i // 2, body, 0)

    @pl.when(i % 2 == 0)
    def _():
        consume(i, s0_scr, True)

    @pl.when(i % 2 == 1)
    def _():
        scores(i, s1_scr)
        consume(i - 1, s0_scr, False)
        consume(i, s1_scr, True)

    span = WINDOW + tq
    w0 = pl.multiple_of(jnp.maximum(t0 + tq - span, 0), tq)
    dist = (t0 - w0) + (lax.broadcasted_iota(jnp.int32, (span, tq), 1)
                        - lax.broadcasted_iota(jnp.int32, (span, tq), 0))
    in_window = jnp.abs(2 * dist - (WINDOW - 1)) < WINDOW
    win = []
    for g in range(NSA_KV):
        s = per_head(_dot(kw_ref[0, g, pl.ds(w0, span), :], q_grp[g]), in_window)
        p = jnp.exp2(s - jnp.max(s, axis=0, keepdims=True)).astype(BF16)
        win.append(_dot(vw_ref[0, g, :, pl.ds(w0, span)], p))

    gate = jax.nn.sigmoid(gate_ref[...].astype(F32).T)
    gout = gout_ref[...]
    rows = []
    for hh in range(NSA_HEADS):
        g, lanes = hh // NSA_GROUP, slice((hh % NSA_GROUP) * tq, (hh % NSA_GROUP + 1) * tq)
        a_s, a_w = acc_scr[g][:, lanes], win[g][:, lanes]
        o = (gate[hh:hh + 1] * ocmp_ref[0, hh].astype(F32)
             + gate[NSA_HEADS + hh:NSA_HEADS + hh + 1] * (a_s[:HEAD_DIM] * (1.0 / a_s[HEAD_DIM:HEAD_DIM + 1]))
             + gate[2 * NSA_HEADS + hh:2 * NSA_HEADS + hh + 1] * (a_w[:HEAD_DIM] * (1.0 / a_w[HEAD_DIM:HEAD_DIM + 1])))
        rows.append(o * lax.rsqrt(jnp.mean(o * o, axis=0, keepdims=True) + EPS) * gout)
    o_ref[...] = jnp.concatenate(rows, axis=0).T.astype(o_ref.dtype)


def _attn(qt, ks, vst, kw, vwt, bias, ocmp, proj, out_norm_g, batch, seq):
    nt = seq // NSA_TQ
    assert WINDOW % NSA_TQ == 0 and seq >= WINDOW + NSA_TQ
    rows_full = lambda d: pl.BlockSpec((1, NSA_KV, seq, d), lambda b, t: (b, 0, 0, 0))
    cols_full = pl.BlockSpec((1, NSA_KV, V_ROWS, seq), lambda b, t: (b, 0, 0, 0))
    qtile = lambda heads: pl.BlockSpec((1, heads, HEAD_DIM, NSA_TQ), lambda b, t: (b, 0, 0, t))
    return pl.pallas_call(
        _attn_kernel,
        grid=(batch, nt),
        in_specs=[qtile(NSA_HEADS), rows_full(2 * HEAD_DIM), cols_full, rows_full(HEAD_DIM), cols_full,
                  qtile(NSA_KV), qtile(NSA_HEADS),
                  pl.BlockSpec((NSA_TQ, LANES), lambda b, t: (b * nt + t, COL_GATE)),
                  pl.BlockSpec((HEAD_DIM, NSA_TQ), lambda b, t: (0, 0))],
        out_specs=pl.BlockSpec((NSA_TQ, NSA_WIDTH), lambda b, t: (b * nt + t, 0)),
        out_shape=jax.ShapeDtypeStruct((batch * seq, NSA_WIDTH), BF16),
        scratch_shapes=[pltpu.VMEM((NSA_KV, 1, NSA_GROUP * NSA_TQ), F32),
                        pltpu.VMEM((NSA_KV, V_ROWS, NSA_GROUP * NSA_TQ), F32),
                        pltpu.VMEM((NSA_KV, NSA_TQ, NSA_GROUP * NSA_TQ), F32),
                        pltpu.VMEM((NSA_KV, NSA_TQ, NSA_GROUP * NSA_TQ), F32)],
        compiler_params=pltpu.CompilerParams(dimension_semantics=("parallel", "parallel"),
                                             vmem_limit_bytes=VMEM_LIMIT),
        name="nsa_attn",
    )(qt, ks, vst, kw, vwt, bias, ocmp, proj, jnp.tile(out_norm_g[:, None], (1, NSA_TQ)))


def _nsa(proj, qt, ks, vst, kw, vwt, k_norm_g, cmp_pe, cmp_w1, cmp_w2, out_norm_g, batch, seq):
    def groups(col):
        a = proj[:, col * LANES:(col + 1) * LANES].reshape(batch, seq, NSA_KV, HEAD_DIM)
        return a.transpose(0, 2, 1, 3).reshape(batch, NSA_KV, seq // CMP_STRIDE, CMP_STRIDE * HEAD_DIM)

    kc, vct = _compress(groups(COL_KC), groups(COL_VC), cmp_pe, cmp_w1, cmp_w2, k_norm_g[0], batch, seq)
    ocmp, bias = _cmpsel(qt, kc, vct, batch, seq)
    return _attn(qt, ks, vst, kw, vwt, bias, ocmp, proj, out_norm_g, batch, seq)


def kernel(x, p, attn_norm_g, w_in, hg_lb_logits, hg_norm_g, nsa_q_norm_g, nsa_k_norm_g, cmp_pe,
           cmp_w1, cmp_w2, nsa_out_norm_g, w_out, ffn_norm_g, w_up, conv_w, conv_b, w_down,
           ple_gate_norm_g, w_ple_gate, w_ple, ple_norm_g):
    B, T, _ = x.shape
    n = B * T
    h = x.reshape(n, D_MODEL)
    for i in range(w_in.shape[0]):
        w_pad = jnp.pad(w_in[i], ((0, 0), (0, IN_PAD - IN_TOTAL))).astype(BF16)
        proj, *qkv = _in_proj(h, attn_norm_g[i][None, :], w_pad, nsa_q_norm_g[i], nsa_k_norm_g[i], B, T)
        o_hg = _hgrn(proj, hg_lb_logits.astype(F32), hg_norm_g[i][None, :], i, B, T)
        o_nsa = _nsa(proj, *qkv, nsa_k_norm_g[i], cmp_pe[i], cmp_w1[i], cmp_w2[i], nsa_out_norm_g[i], B, T)
        w_o = w_out[i].astype(BF16)
        h1 = _out_proj(h, o_hg, o_nsa, w_o[:HG_WIDTH], w_o[HG_WIDTH:])
        h = _ffn_ple(h1, p[i].reshape(n, PLE_DIM), ffn_norm_g[i][None, :], w_up[i].astype(BF16),
                     conv_w[i], conv_b[i][None, :], w_down[i].astype(BF16),
                     ple_gate_norm_g[i][None, :], w_ple_gate[i].astype(BF16), w_ple[i].astype(BF16),
                     ple_norm_g[i][None, :], T)
    return h.reshape(B, T, D_MODEL)
```

```python
import functools

import numpy as np
import jax
import jax.numpy as jnp
from jax import lax
from jax.experimental import pallas as pl
from jax.experimental.pallas import tpu as pltpu

F32 = jnp.float32
BF16 = jnp.bfloat16

D_MODEL = 1024
PLE_DIM = 256
EPS = 1e-6
NEG = -1e30
FORCE = 1e6

HG_HEADS = 4
HG_KDIM = 128
HG_VDIM = 128
HG_WIDTH = HG_HEADS * HG_VDIM
HG_CHUNK = 64

NSA_HEADS = 8
NSA_KV = 2
NSA_GROUP = NSA_HEADS // NSA_KV
HEAD_DIM = 64
NSA_WIDTH = NSA_HEADS * HEAD_DIM
KV_WIDTH = NSA_KV * HEAD_DIM
CMP_LEN = 32
CMP_STRIDE = 16
SEL_BLOCK = 64
SEL_TOPK = 16
WINDOW = 512
ROT_DIM = HEAD_DIM // 4
ROPE_THETA = 500000.0
D_FF = 2816
CONV_W = 3

IN_SIZES = (HG_HEADS * HG_KDIM, HG_HEADS * HG_KDIM, HG_WIDTH, HG_WIDTH, NSA_WIDTH,
            KV_WIDTH, KV_WIDTH, KV_WIDTH, KV_WIDTH, KV_WIDTH, KV_WIDTH, 3 * NSA_HEADS)
IN_TOTAL = sum(IN_SIZES)
LANES = 128
IN_PAD = -(-IN_TOTAL // LANES) * LANES
VMEM_LIMIT = 56 * 1024 * 1024

ROW_TILE = 512
FF_CHUNK = 256
SUBLANES = 8
HALO = SUBLANES
FFN_HALO = 2 * SUBLANES


def _rms(x, g):
    return x * lax.rsqrt(jnp.mean(x * x, axis=-1, keepdims=True) + EPS) * g


def _dot(a, b):
    return jnp.dot(a, b, preferred_element_type=F32)


def _dot_nt(a, b):
    return lax.dot_general(a, b, (((1,), (1,)), ((), ())), preferred_element_type=F32)


def _dot_tn(a, b):
    return lax.dot_general(a, b, (((0,), (0,)), ((), ())), preferred_element_type=F32)


def _split(x):
    hi = x.astype(BF16)
    return hi, (x - hi.astype(F32)).astype(BF16)


def _silu(x):
    hx = 0.5 * x
    return hx + hx * jnp.tanh(hx)


def _out_proj_kernel(x_ref, a_ref, b_ref, wa_ref, wb_ref, o_ref):
    acc = _dot(a_ref[...], wa_ref[...])
    acc += _dot(b_ref[...], wb_ref[...])
    o_ref[...] = x_ref[...] + acc


def _out_proj(x2, o_hg, o_nsa, w_a, w_b):
    n = x2.shape[0]
    return pl.pallas_call(
        _out_proj_kernel,
        grid=(n // ROW_TILE,),
        in_specs=[pl.BlockSpec((ROW_TILE, D_MODEL), lambda i: (i, 0)),
                  pl.BlockSpec((ROW_TILE, HG_WIDTH), lambda i: (i, 0)),
                  pl.BlockSpec((ROW_TILE, NSA_WIDTH), lambda i: (i, 0)),
                  pl.BlockSpec((HG_WIDTH, D_MODEL), lambda i: (0, 0)),
                  pl.BlockSpec((NSA_WIDTH, D_MODEL), lambda i: (0, 0))],
        out_specs=pl.BlockSpec((ROW_TILE, D_MODEL), lambda i: (i, 0)),
        out_shape=jax.ShapeDtypeStruct((n, D_MODEL), F32),
        compiler_params=pltpu.CompilerParams(dimension_semantics=("parallel",),
                                             vmem_limit_bytes=VMEM_LIMIT),
        name="out_proj",
    )(x2, o_hg, o_nsa, w_a, w_b)


def _ffn_ple_kernel(h_ref, halo_ref, p_ref, gf_ref, wup_ref, cw_ref, cb_ref, wdn_ref,
                    gg_ref, wg_ref, wp_ref, gp_ref, o_ref, hn_scr, ug_scr, uu_scr, act_scr, *, tiles_per_seq):
    h = h_ref[...]
    rows = h.shape[0]
    first = (pl.program_id(0) % tiles_per_seq) == 0
    hn_scr[0:FFN_HALO, :] = jnp.where(first, 0.0, _rms(halo_ref[...], gf_ref[...])).astype(BF16)
    hn_scr[FFN_HALO:FFN_HALO + rows, :] = _rms(h, gf_ref[...]).astype(BF16)
    for c in range(0, D_FF, FF_CHUNK):
        conv = []
        for scr, off in ((ug_scr, c), (uu_scr, D_FF + c)):
            scr[...] = _dot(hn_scr[...], wup_ref[:, off:off + FF_CHUNK])
            cw = cw_ref[:, off:off + FF_CHUNK]
            conv.append(scr[FFN_HALO - 2:FFN_HALO - 2 + rows, :] * cw[0:1, :]
                        + scr[FFN_HALO - 1:FFN_HALO - 1 + rows, :] * cw[1:2, :]
                        + scr[FFN_HALO:FFN_HALO + rows, :] * cw[2:3, :]
                        + cb_ref[:, off:off + FF_CHUNK])
        gate, up = conv
        act_scr[:, c:c + FF_CHUNK] = (_silu(gate) * up).astype(BF16)
    o_ref[...] = h_ref[...] + _dot(act_scr[...], wdn_ref[...])
    hn_scr[FFN_HALO:FFN_HALO + rows, :] = _rms(o_ref[...], gg_ref[...]).astype(BF16)
    e = _rms(_dot(p_ref[...].astype(BF16), wp_ref[...]), gp_ref[...])
    gate = jax.nn.sigmoid(_dot(hn_scr[FFN_HALO:FFN_HALO + rows, :], wg_ref[...]))
    o_ref[...] = o_ref[...] + gate * e


def _ffn_ple(h1, p2, gf, w_up, conv_w, conv_b, w_down, gg, w_gate, w_ple, gp, seq):
    n = h1.shape[0]
    const = lambda i: (0, 0)
    halo_blocks = ROW_TILE // FFN_HALO
    return pl.pallas_call(
        functools.partial(_ffn_ple_kernel, tiles_per_seq=seq // ROW_TILE),
        grid=(n // ROW_TILE,),
        in_specs=[pl.BlockSpec((ROW_TILE, D_MODEL), lambda i: (i, 0)),
                  pl.BlockSpec((FFN_HALO, D_MODEL), lambda i: (jnp.maximum(i * halo_blocks - 1, 0), 0)),
                  pl.BlockSpec((ROW_TILE, PLE_DIM), lambda i: (i, 0)),
                  pl.BlockSpec((1, D_MODEL), const),
                  pl.BlockSpec((D_MODEL, 2 * D_FF), const, pipeline_mode=pl.Buffered(1)),
                  pl.BlockSpec((CONV_W, 2 * D_FF), const),
                  pl.BlockSpec((1, 2 * D_FF), const),
                  pl.BlockSpec((D_FF, D_MODEL), const, pipeline_mode=pl.Buffered(1)),
                  pl.BlockSpec((1, D_MODEL), const),
                  pl.BlockSpec((D_MODEL, D_MODEL), const, pipeline_mode=pl.Buffered(1)),
                  pl.BlockSpec((PLE_DIM, D_MODEL), const, pipeline_mode=pl.Buffered(1)),
                  pl.BlockSpec((1, D_MODEL), const)],
        out_specs=pl.BlockSpec((ROW_TILE, D_MODEL), lambda i: (i, 0)),
        out_shape=jax.ShapeDtypeStruct((n, D_MODEL), F32),
        scratch_shapes=[pltpu.VMEM((FFN_HALO + ROW_TILE, D_MODEL), BF16),
                        pltpu.VMEM((FFN_HALO + ROW_TILE, FF_CHUNK), F32),
                        pltpu.VMEM((FFN_HALO + ROW_TILE, FF_CHUNK), F32),
                        pltpu.VMEM((ROW_TILE, D_FF), BF16)],
        compiler_params=pltpu.CompilerParams(dimension_semantics=("parallel",),
                                             vmem_limit_bytes=VMEM_LIMIT),
        name="ffn_ple",
    )(h1, h1, p2, gf, w_up, conv_w, conv_b, w_down, gg, w_gate, w_ple, gp)


HG_LEVELS = (32, 16, 8, 4, 2, 1)
HG_TB = 256


def _hgrn_level_matrices():
    c = HG_CHUNK
    r = np.arange(c)[:, None]
    u = np.arange(c)[None, :]
    rows = []
    for m in HG_LEVELS:
        r0 = (r // m) * m
        upper = (r & m) != 0
        rows.append(np.where(upper, (u >= r0) & (u <= r), (u >= r + 1) & (u <= r0 + m - 1)))
    rows.append(u <= r)
    rows.append(u > r)
    return np.concatenate(rows, 0).astype(np.float32)


def _hgrn_pair_masks():
    c = HG_CHUNK
    t = np.arange(c)[:, None]
    s = np.arange(c)[None, :]
    masks = [((t // (2 * m)) == (s // (2 * m))) & ((t & m) != 0) & ((s & m) == 0) for m in HG_LEVELS]
    masks.append(t == s)
    return np.stack(masks).astype(np.float32)


def _hgrn_kernel(q_ref, f_ref, v_ref, g_ref, lbl_ref, ng_ref, mall_ref, msk_ref, o_ref, st_ref,
                 *, layer, n_chunks):
    @pl.when(pl.program_id(1) == 0)
    def _():
        st_ref[...] = jnp.zeros_like(st_ref)

    lbl = lbl_ref[...]
    e = jnp.exp(lbl - jnp.max(lbl, axis=0, keepdims=True))
    sm = e / jnp.sum(e, axis=0, keepdims=True)
    lb = jnp.sum(sm[:layer + 1], axis=0, keepdims=True)
    mall = mall_ref[...]
    ng = ng_ref[...]
    c = HG_CHUNK
    nl = len(HG_LEVELS)

    intra, q_decayed, carry_decay, carry_add = {}, {}, {}, {}
    for ci in range(n_chunks):
        r0 = ci * c
        fg = lb + (1.0 - lb) * jax.nn.sigmoid(f_ref[r0:r0 + c, :].astype(F32))
        kk = 1.0 - fg
        ee = jnp.exp(_dot(mall, jnp.concatenate(_split(jnp.log(fg)), axis=0)))
        qf = _silu(q_ref[r0:r0 + c, :].astype(F32)) * HG_KDIM ** -0.5
        for h in range(HG_HEADS):
            sl = slice(h * HG_KDIM, (h + 1) * HG_KDIM)
            qh, kh = qf[:, sl], kk[:, sl]
            a = msk_ref[nl] * _dot_nt(qh.astype(BF16), kh.astype(BF16))
            for j in range(nl):
                ej = ee[j * c:(j + 1) * c, sl]
                a += msk_ref[j] * _dot_nt((qh * ej).astype(BF16), (kh * ej).astype(BF16))
            vh = v_ref[r0:r0 + c, sl]
            eb = ee[nl * c:(nl + 1) * c, sl]
            intra[ci, h] = _dot(a.astype(BF16), vh)
            q_decayed[ci, h] = (qh * eb).astype(BF16)
            carry_decay[ci, h] = eb[c - 1:c, :]
            carry_add[ci, h] = _dot_tn(vh, (kh * ee[(nl + 1) * c:(nl + 2) * c, sl]).astype(BF16))

    for h in range(HG_HEADS):
        sl = slice(h * HG_KDIM, (h + 1) * HG_KDIM)
        st = st_ref[h]
        states = []
        for ci in range(n_chunks):
            states.append(st.astype(BF16))
            st = carry_decay[ci, h] * st + carry_add[ci, h]
        st_ref[h] = st
        for ci in range(n_chunks):
            r0 = ci * c
            o = intra[ci, h] + _dot_nt(q_decayed[ci, h], states[ci])
            on = o * lax.rsqrt(jnp.mean(o * o, axis=-1, keepdims=True) + EPS) * ng
            o_ref[r0:r0 + c, sl] = (on * _silu(g_ref[r0:r0 + c, sl].astype(F32))).astype(o_ref.dtype)


def _hgrn(proj, lb_logits, norm_g, layer, batch, seq):
    nt = seq // HG_TB
    w = HG_HEADS * HG_KDIM
    col = lambda k: (lambda b, t: (b * nt + t, k))
    const2 = lambda b, t: (0, 0)
    return pl.pallas_call(
        functools.partial(_hgrn_kernel, layer=layer, n_chunks=HG_TB // HG_CHUNK),
        grid=(batch, nt),
        in_specs=[pl.BlockSpec((HG_TB, w), col(0)), pl.BlockSpec((HG_TB, w), col(1)),
                  pl.BlockSpec((HG_TB, w), col(2)), pl.BlockSpec((HG_TB, w), col(3)),
                  pl.BlockSpec(lb_logits.shape, const2),
                  pl.BlockSpec((1, HG_VDIM), const2),
                  pl.BlockSpec(((len(HG_LEVELS) + 2) * HG_CHUNK, 2 * HG_CHUNK), const2),
                  pl.BlockSpec((len(HG_LEVELS) + 1, HG_CHUNK, HG_CHUNK), lambda b, t: (0, 0, 0))],
        out_specs=pl.BlockSpec((HG_TB, w), lambda b, t: (b * nt + t, 0)),
        out_shape=jax.ShapeDtypeStruct((batch * seq, HG_WIDTH), BF16),
        scratch_shapes=[pltpu.VMEM((HG_HEADS, HG_VDIM, HG_KDIM), F32)],
        compiler_params=pltpu.CompilerParams(dimension_semantics=("parallel", "arbitrary"),
                                             vmem_limit_bytes=VMEM_LIMIT),
        name="hgrn2",
    )(proj, proj, proj, proj, lb_logits, norm_g,
      jnp.asarray(np.tile(_hgrn_level_matrices(), (1, 2)), BF16), jnp.asarray(_hgrn_pair_masks(), F32))


COL_NQ = 2048 // LANES
COL_KC, COL_VC, COL_KS, COL_VS, COL_KW, COL_VW, COL_GATE = (COL_NQ + 4 + k for k in range(7))
LOG2E = 1.4426950408889634
MASK_BIAS = -1e30
V_ROWS = HEAD_DIM + 2 * SUBLANES


def _rope_tables(pos):
    half = ROT_DIM // 2
    inv = np.float32(ROPE_THETA) ** (-np.arange(half, dtype=np.float32) / np.float32(half))
    ang = pos.astype(np.float32)[:, None] * inv[None, :]
    n = pos.shape[0]
    pad = HEAD_DIM - ROT_DIM
    cos = np.concatenate([np.cos(ang), np.cos(ang), np.ones((n, pad), np.float32)], axis=1)
    sin = np.concatenate([np.sin(ang), np.sin(ang), np.zeros((n, pad), np.float32)], axis=1)
    return cos.astype(np.float32), sin.astype(np.float32)


def _rot_matrix(heads):
    half = ROT_DIM // 2
    r = np.zeros((HEAD_DIM, HEAD_DIM), np.float32)
    for j in range(half):
        r[j + half, j] = -1.0
        r[j, j + half] = 1.0
    return np.kron(np.eye(heads, dtype=np.float32), r)


def _norm_rope_t(xt, gain, cos, sin):
    half = ROT_DIM // 2
    xn = xt * lax.rsqrt(jnp.mean(xt * xt, axis=0, keepdims=True) + EPS) * gain
    x1, x2 = xn[:half], xn[half:ROT_DIM]
    return jnp.concatenate([x1 * cos - x2 * sin, x2 * cos + x1 * sin, xn[ROT_DIM:]], axis=0)


def _in_proj_kernel(x_ref, g_ref, w_ref, cos_ref, sin_ref, gq_ref, gk_ref,
                    o_ref, qo_ref, kso_ref, vso_ref, kwo_ref, vwo_ref, *, tiles_per_seq):
    tb = x_ref.shape[0]
    xb = _rms(x_ref[...], g_ref[...]).astype(BF16)
    hg_cols = COL_NQ * LANES
    qp = _dot(xb, w_ref[:, hg_cols:hg_cols + NSA_WIDTH])
    o_ref[:, hg_cols:hg_cols + NSA_WIDTH] = qp.astype(o_ref.dtype)
    kv = _dot(xb, w_ref[:, COL_KC * LANES:])
    o_ref[:, COL_KC * LANES:] = kv.astype(o_ref.dtype)
    for c in range(0, hg_cols, hg_cols // 2):
        o_ref[:, c:c + hg_cols // 2] = _dot(xb, w_ref[:, c:c + hg_cols // 2]).astype(o_ref.dtype)
    part_t = lambda col: kv[:, (col - COL_KC) * LANES:(col - COL_KC + 1) * LANES].T

    cos, sin = cos_ref[...], sin_ref[...]
    qt = qp.T
    for h in range(NSA_HEADS):
        qo_ref[0, h] = _norm_rope_t(qt[h * HEAD_DIM:(h + 1) * HEAD_DIM], gq_ref[...], cos, sin).astype(qo_ref.dtype)

    t0 = (pl.program_id(0) % tiles_per_seq) * tb
    block = (t0 + lax.broadcasted_iota(jnp.int32, (HEAD_DIM, tb), 1)) // SEL_BLOCK
    row_i = lax.broadcasted_iota(jnp.int32, (HEAD_DIM, tb), 0)
    tails = (jnp.where(row_i == block, 1.0, 0.0), jnp.where(row_i == 0, 1.0, 0.0))
    row_ones = jnp.ones((V_ROWS - HEAD_DIM, tb), F32)
    for j, (kcol, vcol, ko_ref, vo_ref) in enumerate(((COL_KS, COL_VS, kso_ref, vso_ref),
                                                      (COL_KW, COL_VW, kwo_ref, vwo_ref))):
        kt, vt = part_t(kcol), part_t(vcol)
        for g in range(NSA_KV):
            kh = _norm_rope_t(kt[g * HEAD_DIM:(g + 1) * HEAD_DIM], gk_ref[j], cos, sin)
            ko_ref[0, g] = jnp.concatenate([kh, tails[j]], axis=0).T.astype(ko_ref.dtype)
        for g in range(NSA_KV):
            vo_ref[0, g] = jnp.concatenate([vt[g * HEAD_DIM:(g + 1) * HEAD_DIM], row_ones], axis=0).astype(vo_ref.dtype)


def _in_proj(x2, g, w_pad, q_norm_g, k_norm_g, batch, seq):
    n = x2.shape[0]
    nt = seq // ROW_TILE
    assert seq // SEL_BLOCK <= HEAD_DIM
    cos, sin = _rope_tables(np.arange(seq))
    half = ROT_DIM // 2
    cos_t, sin_t = jnp.asarray(cos[:, :half].T.copy()), jnp.asarray(sin[:, :half].T.copy())
    lanes = lambda gain: jnp.tile(gain[..., None], (1,) * gain.ndim + (ROW_TILE,))
    gq = lanes(q_norm_g * (HEAD_DIM ** -0.5 * LOG2E))
    gk = lanes(k_norm_g[1:3])
    const = lambda i: (0, 0)
    rows_major = lambda d: pl.BlockSpec((1, NSA_KV, ROW_TILE, d), lambda i: (i // nt, 0, i % nt, 0))
    cols_major = lambda heads, d: pl.BlockSpec((1, heads, d, ROW_TILE), lambda i: (i // nt, 0, 0, i % nt))
    sds = lambda *shape: jax.ShapeDtypeStruct((batch,) + shape, BF16)
    return pl.pallas_call(
        functools.partial(_in_proj_kernel, tiles_per_seq=nt),
        grid=(n // ROW_TILE,),
        in_specs=[pl.BlockSpec((ROW_TILE, D_MODEL), lambda i: (i, 0)),
                  pl.BlockSpec((1, D_MODEL), const),
                  pl.BlockSpec((D_MODEL, IN_PAD), const),
                  pl.BlockSpec((half, ROW_TILE), lambda i: (0, i % nt)),
                  pl.BlockSpec((half, ROW_TILE), lambda i: (0, i % nt)),
                  pl.BlockSpec((HEAD_DIM, ROW_TILE), const),
                  pl.BlockSpec((2, HEAD_DIM, ROW_TILE), lambda i: (0, 0, 0))],
        out_specs=[pl.BlockSpec((ROW_TILE, IN_PAD), lambda i: (i, 0)),
                   cols_major(NSA_HEADS, HEAD_DIM), rows_major(2 * HEAD_DIM), cols_major(NSA_KV, V_ROWS),
                   rows_major(2 * HEAD_DIM), cols_major(NSA_KV, V_ROWS)],
        out_shape=[jax.ShapeDtypeStruct((n, IN_PAD), BF16),
                   sds(NSA_HEADS, HEAD_DIM, seq), sds(NSA_KV, seq, 2 * HEAD_DIM), sds(NSA_KV, V_ROWS, seq),
                   sds(NSA_KV, seq, 2 * HEAD_DIM), sds(NSA_KV, V_ROWS, seq)],
        compiler_params=pltpu.CompilerParams(dimension_semantics=("parallel",),
                                             vmem_limit_bytes=VMEM_LIMIT),
        name="in_proj",
    )(x2, g, w_pad, cos_t, sin_t, gq, gk)


def _compress_kernel(xk_ref, xv_ref, pea_ref, peb_ref, w1_ref, w2_ref, gk_ref, cos_ref, sin_ref,
                     rot_ref, kc_ref, vc_ref, shift_scr):
    nc = xk_ref.shape[2]
    half = w1_ref.shape[1] // 2
    for j, x_ref in enumerate((xk_ref, xv_ref)):
        x = x_ref[0, 0].astype(F32)
        u = _dot((x + pea_ref[j:j + 1, :]).astype(BF16), w1_ref[j, :half, :])
        v = _dot((x + peb_ref[j:j + 1, :]).astype(BF16), w1_ref[j, half:, :])
        shift_scr[0:nc, :] = v
        shift_scr[nc:nc + HALO, :] = jnp.zeros((HALO, v.shape[1]), F32)
        hid = _silu(u + shift_scr[1:nc + 1, :])
        out = _dot(hid.astype(BF16), w2_ref[j])
        if j == 0:
            xn = _rms(out, gk_ref[...])
            out = xn * cos_ref[...] + _dot(xn.astype(BF16), rot_ref[...]) * sin_ref[...]
            kc_ref[0, 0] = out.astype(kc_ref.dtype)
        else:
            wide = jnp.concatenate([out, jnp.zeros_like(out)], axis=1)
            vc_ref[0, 0] = wide.T[:HEAD_DIM].astype(vc_ref.dtype)


def _compress(xk, xv, cmp_pe, cmp_w1, cmp_w2, gk0, batch, seq):
    nc = seq // CMP_STRIDE
    grp = CMP_STRIDE * HEAD_DIM
    cos, sin = (jnp.asarray(t) for t in _rope_tables(np.arange(nc) * CMP_STRIDE + CMP_LEN - 1))
    pea = cmp_pe[:, :CMP_STRIDE].reshape(2, grp)
    peb = cmp_pe[:, CMP_STRIDE:].reshape(2, grp)
    blk = pl.BlockSpec((1, 1, nc, grp), lambda b, g: (b, g, 0, 0))
    const2 = lambda b, g: (0, 0)
    const3 = lambda b, g: (0, 0, 0)
    return pl.pallas_call(
        _compress_kernel,
        grid=(batch, NSA_KV),
        in_specs=[blk, blk,
                  pl.BlockSpec((2, grp), const2), pl.BlockSpec((2, grp), const2),
                  pl.BlockSpec(cmp_w1.shape, const3), pl.BlockSpec(cmp_w2.shape, const3),
                  pl.BlockSpec((1, HEAD_DIM), const2),
                  pl.BlockSpec((nc, HEAD_DIM), const2), pl.BlockSpec((nc, HEAD_DIM), const2),
                  pl.BlockSpec((HEAD_DIM, HEAD_DIM), const2)],
        out_specs=[pl.BlockSpec((1, 1, nc, HEAD_DIM), lambda b, g: (b, g, 0, 0)),
                   pl.BlockSpec((1, 1, HEAD_DIM, nc), lambda b, g: (b, g, 0, 0))],
        out_shape=[jax.ShapeDtypeStruct((batch, NSA_KV, nc, HEAD_DIM), BF16),
                   jax.ShapeDtypeStruct((batch, NSA_KV, HEAD_DIM, nc), BF16)],
        scratch_shapes=[pltpu.VMEM((nc + HALO, cmp_w1.shape[2]), F32)],
        compiler_params=pltpu.CompilerParams(dimension_semantics=("parallel", "parallel"),
                                             vmem_limit_bytes=VMEM_LIMIT),
        name="nsa_compress",
    )(xk, xv, pea, peb, cmp_w1.astype(BF16), cmp_w2.astype(BF16), gk0[None, :], cos, sin,
      jnp.asarray(_rot_matrix(1), BF16))


NSA_TQ = 256


def _overlap_t(seq):
    ns, nc = seq // SEL_BLOCK, seq // CMP_STRIDE
    cs = np.arange(nc)[None, :] * CMP_STRIDE
    ss = np.arange(ns)[:, None] * SEL_BLOCK
    ov = np.clip(np.minimum(cs + CMP_LEN, ss + SEL_BLOCK) - np.maximum(cs, ss), 0, None)
    return (ov / CMP_LEN).astype(np.float32)


def _cmpsel_kernel(offset_ref, q_ref, kc_ref, vc_ref, ovt_ref, ocmp_ref, bias_ref, val_scr, cnt_scr, *, top):
    tq, nc = q_ref.shape[3], kc_ref.shape[2]
    ns = ovt_ref.shape[0]
    t0 = pl.program_id(1) * tq
    n_i = lax.broadcasted_iota(jnp.int32, (ns, tq), 0)
    t_i = t0 + lax.broadcasted_iota(jnp.int32, (ns, tq), 1)
    cur = t_i // SEL_BLOCK
    forced = jnp.where(n_i == 0, FORCE, jnp.where(n_i == cur, FORCE, jnp.where(n_i == cur - 1, FORCE, 0.0)))
    visible = n_i * SEL_BLOCK <= t_i

    def attend(g, ncv):
        c_i = lax.broadcasted_iota(jnp.int32, (ncv, NSA_GROUP * tq), 0)
        q_i = lax.broadcasted_iota(jnp.int32, (ncv, NSA_GROUP * tq), 1) & (tq - 1)
        cmask = c_i * CMP_STRIDE + (CMP_LEN - 1) <= t0 + q_i
        qg = jnp.concatenate([q_ref[0, g * NSA_GROUP + h] for h in range(NSA_GROUP)], axis=1)
        s = jnp.where(cmask, _dot(kc_ref[0, g, 0:ncv, :], qg), NEG)
        e = jnp.exp2(s - jnp.max(s, axis=0, keepdims=True))
        p = jnp.where(cmask, e * (1.0 / jnp.sum(e, axis=0, keepdims=True)), 0.0)
        o = _dot(vc_ref[0, g, :, 0:ncv], p.astype(BF16)).astype(ocmp_ref.dtype)
        psum = p[:, :tq]
        for h in range(NSA_GROUP):
            ocmp_ref[0, g * NSA_GROUP + h] = o[:, h * tq:(h + 1) * tq]
            if h:
                psum += p[:, h * tq:(h + 1) * tq]
        ovt = ovt_ref[:, 0:ncv]
        hi, lo = _split(psum)
        val_scr[...] = jnp.where(visible, _dot(ovt, hi) + _dot(ovt, lo) + forced, NEG)

    nc_quarter = nc // 4
    prefix = ((t0 + tq) // CMP_STRIDE - 1) // nc_quarter
    for g in range(NSA_KV):
        for v in range(4):
            @pl.when(prefix == v)
            def _():
                attend(g, (v + 1) * nc_quarter)
        cnt_scr[...] = jnp.zeros_like(cnt_scr)
        sub = lax.broadcasted_iota(jnp.int32, (SUBLANES, tq), 0)
        quarter = max(ns // 4, SUBLANES)
        last_quarter = (t0 + tq - 1) // (SEL_BLOCK * quarter)
        for rq in range(ns // quarter):
            for mq in range(ns // quarter):
                @pl.when(last_quarter >= max(rq, mq))
                def _():
                    for r in range(rq * quarter // SUBLANES, (rq + 1) * quarter // SUBLANES):
                        rows = pl.ds(r * SUBLANES, SUBLANES)
                        v_r = val_scr[rows, :]
                        c = cnt_scr[rows, :]
                        for m in range(mq * quarter, (mq + 1) * quarter):
                            row = val_scr[m:m + 1, :]
                            if m // SUBLANES > r:
                                hit = row > v_r
                            elif m // SUBLANES < r:
                                hit = row >= v_r
                            else:
                                hit = jnp.where(sub > m % SUBLANES, jnp.where(row >= v_r, 1.0, 0.0),
                                                jnp.where(row > v_r, 1.0, 0.0)) > 0.5
                            c = jnp.where(hit, c + 1.0, c)
                        cnt_scr[rows, :] = c
        bias = jnp.where(cnt_scr[...] < top, offset_ref[0], MASK_BIAS)
        if ns < HEAD_DIM:
            bias = jnp.concatenate([bias, jnp.zeros((HEAD_DIM - ns, tq), F32)], axis=0)
        bias_ref[0, g] = bias.astype(bias_ref.dtype)


def _cmpsel(offset, qt, kc, vct, batch, seq):
    nt = seq // NSA_TQ
    nc, ns = seq // CMP_STRIDE, seq // SEL_BLOCK
    qblk = pl.BlockSpec((1, NSA_HEADS, HEAD_DIM, NSA_TQ), lambda b, t: (b, 0, 0, t))
    return pl.pallas_call(
        functools.partial(_cmpsel_kernel, top=min(SEL_TOPK, ns)),
        grid=(batch, nt),
        in_specs=[pl.BlockSpec(memory_space=pltpu.SMEM), qblk,
                  pl.BlockSpec((1, NSA_KV, nc, HEAD_DIM), lambda b, t: (b, 0, 0, 0)),
                  pl.BlockSpec((1, NSA_KV, HEAD_DIM, nc), lambda b, t: (b, 0, 0, 0)),
                  pl.BlockSpec((ns, nc), lambda b, t: (0, 0))],
        out_specs=[qblk, pl.BlockSpec((1, NSA_KV, HEAD_DIM, NSA_TQ), lambda b, t: (b, 0, 0, t))],
        out_shape=[jax.ShapeDtypeStruct((batch, NSA_HEADS, HEAD_DIM, seq), BF16),
                   jax.ShapeDtypeStruct((batch, NSA_KV, HEAD_DIM, seq), BF16)],
        scratch_shapes=[pltpu.VMEM((ns, NSA_TQ), F32), pltpu.VMEM((ns, NSA_TQ), F32)],
        compiler_params=pltpu.CompilerParams(dimension_semantics=("parallel", "parallel"),
                                             vmem_limit_bytes=VMEM_LIMIT),
        name="nsa_cmpsel",
    )(offset, qt, kc, vct, jnp.asarray(_overlap_t(seq), BF16))


SAFE_OFFSET = 60.0


def _attn_kernel(q_ref, ks_ref, vs_ref, kw_ref, vw_ref, bias_ref, wrow_ref, ocmp_ref, gate_ref, gout_ref,
                 o_ref, acc_scr, *scratch, bounded):
    tq = q_ref.shape[3]
    i = pl.program_id(1)
    t0 = i * tq
    q_sel, q_win = [], []
    for g in range(NSA_KV):
        q_grp = jnp.concatenate([q_ref[0, g * NSA_GROUP + h] for h in range(NSA_GROUP)], axis=1)
        q_sel.append(jnp.concatenate([q_grp, jnp.concatenate([bias_ref[0, g]] * NSA_GROUP, axis=1)], axis=0))
        q_win.append(jnp.concatenate([q_grp, wrow_ref[...]], axis=0))

    def per_head(s, keep):
        return jnp.concatenate([jnp.where(keep, s[:, h * tq:(h + 1) * tq], NEG) for h in range(NSA_GROUP)], axis=1)

    def causal(s):
        key_i = lax.broadcasted_iota(jnp.int32, (tq, tq), 0)
        return per_head(s, key_i <= lax.broadcasted_iota(jnp.int32, (tq, tq), 1))

    def tile(j):
        return pl.ds(pl.multiple_of(j * tq, tq), tq)

    acc_scr[...] = jnp.zeros_like(acc_scr)
    span = WINDOW + tq
    w0 = pl.multiple_of(jnp.maximum(t0 + tq - span, 0), tq)
    dist = (t0 - w0) + (lax.broadcasted_iota(jnp.int32, (span, tq), 1)
                        - lax.broadcasted_iota(jnp.int32, (span, tq), 0))
    in_window = jnp.abs(2 * dist - (WINDOW - 1)) < WINDOW
    win_scores = lambda g: per_head(_dot(kw_ref[0, g, pl.ds(w0, span), :], q_win[g]), in_window)

    if bounded:
        p0_scr, p1_scr = scratch

        def probs(j, p_scr, diag):
            for g in range(NSA_KV):
                s = _dot(ks_ref[0, g, tile(j), :], q_sel[g])
                p_scr[g] = jnp.exp2(causal(s) if diag else s).astype(BF16)

        def accumulate(j, p_scr):
            for g in range(NSA_KV):
                acc_scr[g] += _dot(vs_ref[0, g, :, tile(j)], p_scr[g])

        @pl.when(i > 0)
        def _():
            probs(0, p0_scr, False)

        def body(jj, carry):
            probs(2 * jj + 1, p1_scr, False)
            accumulate(2 * jj, p0_scr)
            probs(2 * jj + 2, p0_scr, False)
            accumulate(2 * jj + 1, p1_scr)
            return carry

        lax.fori_loop(0, (i - 1) // 2, body, 0)

        @pl.when((i > 0) & (i % 2 == 1))
        def _():
            accumulate(i - 1, p0_scr)

        @pl.when((i > 0) & (i % 2 == 0))
        def _():
            probs(i - 1, p1_scr, False)
            accumulate(i - 2, p0_scr)
            accumulate(i - 1, p1_scr)

        probs(i, p0_scr, True)
        s_win = [win_scores(g) for g in range(NSA_KV)]
        accumulate(i, p0_scr)
        win = [_dot(vw_ref[0, g, :, pl.ds(w0, span)], jnp.exp2(s_win[g]).astype(BF16)) for g in range(NSA_KV)]
    else:
        m_scr, s0_scr, s1_scr = scratch
        m_scr[...] = jnp.full(m_scr.shape, NEG, F32)

        def scores(j, s_scr):
            for g in range(NSA_KV):
                s_scr[g] = _dot(ks_ref[0, g, tile(j), :], q_sel[g])

        def consume(j, s_scr, diag):
            for g in range(NSA_KV):
                s = causal(s_scr[g]) if diag else s_scr[g]
                m_old = m_scr[g]
                m_new = jnp.maximum(m_old, jnp.max(s, axis=0, keepdims=True))
                p = jnp.exp2(s - m_new).astype(BF16)
                acc_scr[g] = jnp.exp2(m_old - m_new) * acc_scr[g] + _dot(vs_ref[0, g, :, tile(j)], p)
                m_scr[g] = m_new

        scores(0, s0_scr)

        def body(jj, carry):
            scores(2 * jj + 1, s1_scr)
            consume(2 * jj, s0_scr, False)
            scores(2 * jj + 2, s0_scr)
            consume(2 * jj + 1, s1_scr, False)
            return carry

        lax.fori_loop(0, i // 2, body, 0)

        @pl.when(i % 2 == 0)
        def _():
            consume(i, s0_scr, True)

        @pl.when(i % 2 == 1)
        def _():
            scores(i, s1_scr)
            consume(i - 1, s0_scr, False)
            consume(i, s1_scr, True)

        win = []
        for g in range(NSA_KV):
            s = win_scores(g)
            p = jnp.exp2(s - jnp.max(s, axis=0, keepdims=True)).astype(BF16)
            win.append(_dot(vw_ref[0, g, :, pl.ds(w0, span)], p))

    gate = jax.nn.sigmoid(gate_ref[...].astype(F32).T)
    gout = gout_ref[...]
    rows = []
    for hh in range(NSA_HEADS):
        g, lanes = hh // NSA_GROUP, slice((hh % NSA_GROUP) * tq, (hh % NSA_GROUP + 1) * tq)
        a_s, a_w = acc_scr[g][:, lanes], win[g][:, lanes]
        o = (gate[hh:hh + 1] * ocmp_ref[0, hh].astype(F32)
             + gate[NSA_HEADS + hh:NSA_HEADS + hh + 1] * (a_s[:HEAD_DIM] * (1.0 / a_s[HEAD_DIM:HEAD_DIM + 1]))
             + gate[2 * NSA_HEADS + hh:2 * NSA_HEADS + hh + 1] * (a_w[:HEAD_DIM] * (1.0 / a_w[HEAD_DIM:HEAD_DIM + 1])))
        rows.append(o * lax.rsqrt(jnp.mean(o * o, axis=0, keepdims=True) + EPS) * gout)
    o_ref[...] = jnp.concatenate(rows, axis=0).T.astype(o_ref.dtype)


def _attn(qt, ks, vst, kw, vwt, bias, wrow, ocmp, proj, gout, *, bounded):
    batch, _, _, seq = qt.shape
    nt = seq // NSA_TQ
    assert WINDOW % NSA_TQ == 0 and seq >= WINDOW + NSA_TQ
    wide = NSA_GROUP * NSA_TQ
    rows_full = pl.BlockSpec((1, NSA_KV, seq, 2 * HEAD_DIM), lambda b, t: (b, 0, 0, 0))
    cols_full = pl.BlockSpec((1, NSA_KV, V_ROWS, seq), lambda b, t: (b, 0, 0, 0))
    qtile = lambda heads: pl.BlockSpec((1, heads, HEAD_DIM, NSA_TQ), lambda b, t: (b, 0, 0, t))
    if bounded:
        scratch = [pltpu.VMEM((NSA_KV, NSA_TQ, wide), BF16), pltpu.VMEM((NSA_KV, NSA_TQ, wide), BF16)]
    else:
        scratch = [pltpu.VMEM((NSA_KV, 1, wide), F32),
                   pltpu.VMEM((NSA_KV, NSA_TQ, wide), F32), pltpu.VMEM((NSA_KV, NSA_TQ, wide), F32)]
    return pl.pallas_call(
        functools.partial(_attn_kernel, bounded=bounded),
        grid=(batch, nt),
        in_specs=[qtile(NSA_HEADS), rows_full, cols_full, rows_full, cols_full, qtile(NSA_KV),
                  pl.BlockSpec((HEAD_DIM, wide), lambda b, t: (0, 0)),
                  qtile(NSA_HEADS),
                  pl.BlockSpec((NSA_TQ, LANES), lambda b, t: (b * nt + t, COL_GATE)),
                  pl.BlockSpec((HEAD_DIM, NSA_TQ), lambda b, t: (0, 0))],
        out_specs=pl.BlockSpec((NSA_TQ, NSA_WIDTH), lambda b, t: (b * nt + t, 0)),
        out_shape=jax.ShapeDtypeStruct((batch * seq, NSA_WIDTH), BF16),
        scratch_shapes=[pltpu.VMEM((NSA_KV, V_ROWS, wide), F32)] + scratch,
        compiler_params=pltpu.CompilerParams(dimension_semantics=("parallel", "parallel"),
                                             vmem_limit_bytes=VMEM_LIMIT),
        name="nsa_attn" if bounded else "nsa_attn_online",
    )(qt, ks, vst, kw, vwt, bias, wrow, ocmp, proj, gout)


def _nsa(proj, qt, ks, vst, kw, vwt, q_norm_g, k_norm_g, cmp_pe, cmp_w1, cmp_w2, out_norm_g, batch, seq):
    def groups(col):
        a = proj[:, col * LANES:(col + 1) * LANES].reshape(batch, seq, NSA_KV, HEAD_DIM)
        return a.transpose(0, 2, 1, 3).reshape(batch, NSA_KV, seq // CMP_STRIDE, CMP_STRIDE * HEAD_DIM)

    def offset(gk):
        bound = HEAD_DIM * (HEAD_DIM ** -0.5 * LOG2E) * jnp.max(jnp.abs(q_norm_g)) * jnp.max(jnp.abs(gk))
        return (1.02 * bound).astype(BF16).astype(F32)

    off_sel, off_win = offset(k_norm_g[1]), offset(k_norm_g[2])
    kc, vct = _compress(groups(COL_KC), groups(COL_VC), cmp_pe, cmp_w1, cmp_w2, k_norm_g[0], batch, seq)
    ocmp, bias = _cmpsel(-off_sel[None], qt, kc, vct, batch, seq)
    wrow = jnp.zeros((HEAD_DIM, NSA_GROUP * NSA_TQ), F32).at[0].set(-off_win).astype(BF16)
    gout = jnp.tile(out_norm_g[:, None], (1, NSA_TQ))
    return lax.cond(jnp.maximum(off_sel, off_win) <= SAFE_OFFSET,
                    functools.partial(_attn, bounded=True), functools.partial(_attn, bounded=False),
                    qt, ks, vst, kw, vwt, bias, wrow, ocmp, proj, gout)


def kernel(x, p, attn_norm_g, w_in, hg_lb_logits, hg_norm_g, nsa_q_norm_g, nsa_k_norm_g, cmp_pe,
           cmp_w1, cmp_w2, nsa_out_norm_g, w_out, ffn_norm_g, w_up, conv_w, conv_b, w_down,
           ple_gate_norm_g, w_ple_gate, w_ple, ple_norm_g):
    B, T, _ = x.shape
    n = B * T
    h = x.reshape(n, D_MODEL)
    for i in range(w_in.shape[0]):
        w_pad = jnp.pad(w_in[i], ((0, 0), (0, IN_PAD - IN_TOTAL))).astype(BF16)
        proj, *qkv = _in_proj(h, attn_norm_g[i][None, :], w_pad, nsa_q_norm_g[i], nsa_k_norm_g[i], B, T)
        o_hg = _hgrn(proj, hg_lb_logits.astype(F32), hg_norm_g[i][None, :], i, B, T)
        o_nsa = _nsa(proj, *qkv, nsa_q_norm_g[i], nsa_k_norm_g[i], cmp_pe[i], cmp_w1[i], cmp_w2[i],
                     nsa_out_norm_g[i], B, T)
        w_o = w_out[i].astype(BF16)
        h1 = _out_proj(h, o_hg, o_nsa, w_o[:HG_WIDTH], w_o[HG_WIDTH:])
        h = _ffn_ple(h1, p[i].reshape(n, PLE_DIM), ffn_norm_g[i][None, :], w_up[i].astype(BF16),
                     conv_w[i], conv_b[i][None, :], w_down[i].astype(BF16),
                     ple_gate_norm_g[i][None, :], w_ple_gate[i].astype(BF16), w_ple[i].astype(BF16),
                     ple_norm_g[i][None, :], T)
    return h.reshape(B, T, D_MODEL)
```

```python
import functools

import numpy as np
import jax
import jax.numpy as jnp
from jax import lax
from jax.experimental import pallas as pl
from jax.experimental.pallas import tpu as pltpu

F32 = jnp.float32
BF16 = jnp.bfloat16

D_MODEL = 1024
PLE_DIM = 256
EPS = 1e-6
NEG = -1e30
FORCE = 1e6

HG_HEADS = 4
HG_KDIM = 128
HG_VDIM = 128
HG_WIDTH = HG_HEADS * HG_VDIM
HG_CHUNK = 64

NSA_HEADS = 8
NSA_KV = 2
NSA_GROUP = NSA_HEADS // NSA_KV
HEAD_DIM = 64
NSA_WIDTH = NSA_HEADS * HEAD_DIM
KV_WIDTH = NSA_KV * HEAD_DIM
CMP_LEN = 32
CMP_STRIDE = 16
SEL_BLOCK = 64
SEL_TOPK = 16
WINDOW = 512
ROT_DIM = HEAD_DIM // 4
ROPE_THETA = 500000.0
D_FF = 2816
CONV_W = 3

IN_SIZES = (HG_HEADS * HG_KDIM, HG_HEADS * HG_KDIM, HG_WIDTH, HG_WIDTH, NSA_WIDTH,
            KV_WIDTH, KV_WIDTH, KV_WIDTH, KV_WIDTH, KV_WIDTH, KV_WIDTH, 3 * NSA_HEADS)
IN_TOTAL = sum(IN_SIZES)
LANES = 128
IN_PAD = -(-IN_TOTAL // LANES) * LANES
VMEM_LIMIT = 56 * 1024 * 1024

ROW_TILE = 512
FF_CHUNK = 256
SUBLANES = 8
HALO = SUBLANES
FFN_HALO = 2 * SUBLANES


def _rms(x, g):
    return x * lax.rsqrt(jnp.mean(x * x, axis=-1, keepdims=True) + EPS) * g


def _dot(a, b):
    return jnp.dot(a, b, preferred_element_type=F32)


def _dot_nt(a, b):
    return lax.dot_general(a, b, (((1,), (1,)), ((), ())), preferred_element_type=F32)


def _dot_tn(a, b):
    return lax.dot_general(a, b, (((0,), (0,)), ((), ())), preferred_element_type=F32)


def _split(x):
    hi = x.astype(BF16)
    return hi, (x - hi.astype(F32)).astype(BF16)


def _silu(x):
    hx = 0.5 * x
    return hx + hx * jnp.tanh(hx)


def _post_mixer_kernel(x_ref, xh_ref, a_ref, ah_ref, b_ref, bh_ref, p_ref, wo_ref, gf_ref, wup_ref, cw_ref,
                       cb_ref, wdn_ref, gg_ref, wg_ref, wp_ref, gp_ref, o_ref,
                       h1_scr, hn_scr, ug_scr, uu_scr, act_scr, *, tiles_per_seq):
    rows = x_ref.shape[0]
    halo, body = slice(0, FFN_HALO), slice(FFN_HALO, FFN_HALO + rows)
    hn_scr[halo, :HG_WIDTH], hn_scr[halo, HG_WIDTH:] = ah_ref[...], bh_ref[...]
    hn_scr[body, :HG_WIDTH], hn_scr[body, HG_WIDTH:] = a_ref[...], b_ref[...]
    mixed = _dot(hn_scr[...], wo_ref[...])
    h1_scr[halo, :] = xh_ref[...] + mixed[halo]
    h1_scr[body, :] = x_ref[...] + mixed[body]
    first = (pl.program_id(0) % tiles_per_seq) == 0
    hn_scr[halo, :] = jnp.where(first, 0.0, _rms(h1_scr[halo, :], gf_ref[...])).astype(BF16)
    hn_scr[body, :] = _rms(h1_scr[body, :], gf_ref[...]).astype(BF16)
    for c in range(0, D_FF, FF_CHUNK):
        conv = []
        for scr, off in ((ug_scr, c), (uu_scr, D_FF + c)):
            scr[...] = _dot(hn_scr[...], wup_ref[:, off:off + FF_CHUNK])
            cw = cw_ref[:, off:off + FF_CHUNK]
            conv.append(scr[FFN_HALO - 2:FFN_HALO - 2 + rows, :] * cw[0:1, :]
                        + scr[FFN_HALO - 1:FFN_HALO - 1 + rows, :] * cw[1:2, :]
                        + scr[body, :] * cw[2:3, :]
                        + cb_ref[:, off:off + FF_CHUNK])
        gate, up = conv
        act_scr[:, c:c + FF_CHUNK] = (_silu(gate) * up).astype(BF16)
    o_ref[...] = h1_scr[body, :] + _dot(act_scr[...], wdn_ref[...])
    hn_scr[body, :] = _rms(o_ref[...], gg_ref[...]).astype(BF16)
    e = _rms(_dot(p_ref[...].astype(BF16), wp_ref[...]), gp_ref[...])
    gate = jax.nn.sigmoid(_dot(hn_scr[body, :], wg_ref[...]))
    o_ref[...] = o_ref[...] + gate * e


def _post_mixer(x2, o_hg, o_nsa, p2, w_out, gf, w_up, conv_w, conv_b, w_down, gg, w_gate, w_ple, gp, seq):
    n = x2.shape[0]
    assert HG_WIDTH + NSA_WIDTH == D_MODEL
    const = lambda i: (0, 0)
    tile = lambda width: pl.BlockSpec((ROW_TILE, width), lambda i: (i, 0))
    halo_blocks = ROW_TILE // FFN_HALO
    halo = lambda width: pl.BlockSpec((FFN_HALO, width), lambda i: (jnp.maximum(i * halo_blocks - 1, 0), 0))
    weight = lambda r, c: pl.BlockSpec((r, c), const, pipeline_mode=pl.Buffered(1))
    padded = FFN_HALO + ROW_TILE
    return pl.pallas_call(
        functools.partial(_post_mixer_kernel, tiles_per_seq=seq // ROW_TILE),
        grid=(n // ROW_TILE,),
        in_specs=[tile(D_MODEL), halo(D_MODEL), tile(HG_WIDTH), halo(HG_WIDTH), tile(NSA_WIDTH), halo(NSA_WIDTH),
                  tile(PLE_DIM),
                  weight(D_MODEL, D_MODEL),
                  pl.BlockSpec((1, D_MODEL), const),
                  weight(D_MODEL, 2 * D_FF),
                  pl.BlockSpec((CONV_W, 2 * D_FF), const),
                  pl.BlockSpec((1, 2 * D_FF), const),
                  weight(D_FF, D_MODEL),
                  pl.BlockSpec((1, D_MODEL), const),
                  weight(D_MODEL, D_MODEL),
                  weight(PLE_DIM, D_MODEL),
                  pl.BlockSpec((1, D_MODEL), const)],
        out_specs=tile(D_MODEL),
        out_shape=jax.ShapeDtypeStruct((n, D_MODEL), F32),
        scratch_shapes=[pltpu.VMEM((padded, D_MODEL), F32),
                        pltpu.VMEM((padded, D_MODEL), BF16),
                        pltpu.VMEM((padded, FF_CHUNK), F32),
                        pltpu.VMEM((padded, FF_CHUNK), F32),
                        pltpu.VMEM((ROW_TILE, D_FF), BF16)],
        compiler_params=pltpu.CompilerParams(dimension_semantics=("parallel",),
                                             vmem_limit_bytes=VMEM_LIMIT),
        name="post_mixer",
    )(x2, x2, o_hg, o_hg, o_nsa, o_nsa, p2, w_out, gf, w_up, conv_w, conv_b, w_down, gg, w_gate, w_ple, gp)


HG_LEVELS = (32, 16, 8, 4, 2, 1)
HG_TB = 256


def _hgrn_level_matrices():
    c = HG_CHUNK
    r = np.arange(c)[:, None]
    u = np.arange(c)[None, :]
    rows = []
    for m in HG_LEVELS:
        r0 = (r // m) * m
        upper = (r & m) != 0
        rows.append(np.where(upper, (u >= r0) & (u <= r), (u >= r + 1) & (u <= r0 + m - 1)))
    rows.append(u <= r)
    rows.append(u > r)
    return np.concatenate(rows, 0).astype(np.float32)


def _hgrn_pair_masks():
    c = HG_CHUNK
    t = np.arange(c)[:, None]
    s = np.arange(c)[None, :]
    masks = [((t // (2 * m)) == (s // (2 * m))) & ((t & m) != 0) & ((s & m) == 0) for m in HG_LEVELS]
    masks.append(t == s)
    return np.stack(masks).astype(np.float32)


def _hgrn_kernel(q_ref, f_ref, v_ref, g_ref, lbl_ref, ng_ref, mall_ref, msk_ref, o_ref, st_ref,
                 *, layer, n_chunks):
    @pl.when(pl.program_id(1) == 0)
    def _():
        st_ref[...] = jnp.zeros_like(st_ref)

    lbl = lbl_ref[...]
    e = jnp.exp(lbl - jnp.max(lbl, axis=0, keepdims=True))
    sm = e / jnp.sum(e, axis=0, keepdims=True)
    lb = jnp.sum(sm[:layer + 1], axis=0, keepdims=True)
    mall = mall_ref[...]
    ng = ng_ref[...]
    c = HG_CHUNK
    nl = len(HG_LEVELS)

    intra, q_decayed, carry_decay, carry_add = {}, {}, {}, {}
    for ci in range(n_chunks):
        r0 = ci * c
        fg = lb + (1.0 - lb) * jax.nn.sigmoid(f_ref[r0:r0 + c, :].astype(F32))
        kk = 1.0 - fg
        ee = jnp.exp(_dot(mall, jnp.concatenate(_split(jnp.log(fg)), axis=0)))
        qf = _silu(q_ref[r0:r0 + c, :].astype(F32)) * HG_KDIM ** -0.5
        for h in range(HG_HEADS):
            sl = slice(h * HG_KDIM, (h + 1) * HG_KDIM)
            qh, kh = qf[:, sl], kk[:, sl]
            a = msk_ref[nl] * _dot_nt(qh.astype(BF16), kh.astype(BF16))
            for j in range(nl):
                ej = ee[j * c:(j + 1) * c, sl]
                a += msk_ref[j] * _dot_nt((qh * ej).astype(BF16), (kh * ej).astype(BF16))
            vh = v_ref[r0:r0 + c, sl]
            eb = ee[nl * c:(nl + 1) * c, sl]
            intra[ci, h] = _dot(a.astype(BF16), vh)
            q_decayed[ci, h] = (qh * eb).astype(BF16)
            carry_decay[ci, h] = eb[c - 1:c, :]
            carry_add[ci, h] = _dot_tn(vh, (kh * ee[(nl + 1) * c:(nl + 2) * c, sl]).astype(BF16))

    for h in range(HG_HEADS):
        sl = slice(h * HG_KDIM, (h + 1) * HG_KDIM)
        st = st_ref[h]
        states = []
        for ci in range(n_chunks):
            states.append(st.astype(BF16))
            st = carry_decay[ci, h] * st + carry_add[ci, h]
        st_ref[h] = st
        for ci in range(n_chunks):
            r0 = ci * c
            o = intra[ci, h] + _dot_nt(q_decayed[ci, h], states[ci])
            on = o * lax.rsqrt(jnp.mean(o * o, axis=-1, keepdims=True) + EPS) * ng
            o_ref[r0:r0 + c, sl] = (on * _silu(g_ref[r0:r0 + c, sl].astype(F32))).astype(o_ref.dtype)


def _hgrn(proj, lb_logits, norm_g, layer, batch, seq):
    nt = seq // HG_TB
    w = HG_HEADS * HG_KDIM
    col = lambda k: (lambda b, t: (b * nt + t, k))
    const2 = lambda b, t: (0, 0)
    return pl.pallas_call(
        functools.partial(_hgrn_kernel, layer=layer, n_chunks=HG_TB // HG_CHUNK),
        grid=(batch, nt),
        in_specs=[pl.BlockSpec((HG_TB, w), col(0)), pl.BlockSpec((HG_TB, w), col(1)),
                  pl.BlockSpec((HG_TB, w), col(2)), pl.BlockSpec((HG_TB, w), col(3)),
                  pl.BlockSpec(lb_logits.shape, const2),
                  pl.BlockSpec((1, HG_VDIM), const2),
                  pl.BlockSpec(((len(HG_LEVELS) + 2) * HG_CHUNK, 2 * HG_CHUNK), const2),
                  pl.BlockSpec((len(HG_LEVELS) + 1, HG_CHUNK, HG_CHUNK), lambda b, t: (0, 0, 0))],
        out_specs=pl.BlockSpec((HG_TB, w), lambda b, t: (b * nt + t, 0)),
        out_shape=jax.ShapeDtypeStruct((batch * seq, HG_WIDTH), BF16),
        scratch_shapes=[pltpu.VMEM((HG_HEADS, HG_VDIM, HG_KDIM), F32)],
        compiler_params=pltpu.CompilerParams(dimension_semantics=("parallel", "arbitrary"),
                                             vmem_limit_bytes=VMEM_LIMIT),
        name="hgrn2",
    )(proj, proj, proj, proj, lb_logits, norm_g,
      jnp.asarray(np.tile(_hgrn_level_matrices(), (1, 2)), BF16), jnp.asarray(_hgrn_pair_masks(), F32))


COL_NQ = 2048 // LANES
COL_KC, COL_VC, COL_KS, COL_VS, COL_KW, COL_VW, COL_GATE = (COL_NQ + 4 + k for k in range(7))
LOG2E = 1.4426950408889634
MASK_BIAS = -1e30
V_ROWS = HEAD_DIM + 2 * SUBLANES


def _rope_tables(pos):
    half = ROT_DIM // 2
    inv = np.float32(ROPE_THETA) ** (-np.arange(half, dtype=np.float32) / np.float32(half))
    ang = pos.astype(np.float32)[:, None] * inv[None, :]
    n = pos.shape[0]
    pad = HEAD_DIM - ROT_DIM
    cos = np.concatenate([np.cos(ang), np.cos(ang), np.ones((n, pad), np.float32)], axis=1)
    sin = np.concatenate([np.sin(ang), np.sin(ang), np.zeros((n, pad), np.float32)], axis=1)
    return cos.astype(np.float32), sin.astype(np.float32)


def _rot_matrix(heads):
    half = ROT_DIM // 2
    r = np.zeros((HEAD_DIM, HEAD_DIM), np.float32)
    for j in range(half):
        r[j + half, j] = -1.0
        r[j, j + half] = 1.0
    return np.kron(np.eye(heads, dtype=np.float32), r)


def _norm_rope_t(xt, gain, cos, sin):
    half = ROT_DIM // 2
    xn = xt * lax.rsqrt(jnp.mean(xt * xt, axis=0, keepdims=True) + EPS) * gain
    x1, x2 = xn[:half], xn[half:ROT_DIM]
    return jnp.concatenate([x1 * cos - x2 * sin, x2 * cos + x1 * sin, xn[ROT_DIM:]], axis=0)


def _in_proj_kernel(x_ref, g_ref, w_ref, cos_ref, sin_ref, gq_ref, gk_ref,
                    o_ref, qo_ref, kso_ref, vso_ref, kwo_ref, vwo_ref, *, tiles_per_seq):
    tb = x_ref.shape[0]
    xb = _rms(x_ref[...], g_ref[...]).astype(BF16)
    hg_cols = COL_NQ * LANES
    qp = _dot(xb, w_ref[:, hg_cols:hg_cols + NSA_WIDTH])
    o_ref[:, hg_cols:hg_cols + NSA_WIDTH] = qp.astype(o_ref.dtype)
    kv = _dot(xb, w_ref[:, COL_KC * LANES:])
    o_ref[:, COL_KC * LANES:] = kv.astype(o_ref.dtype)
    for c in range(0, hg_cols, hg_cols // 2):
        o_ref[:, c:c + hg_cols // 2] = _dot(xb, w_ref[:, c:c + hg_cols // 2]).astype(o_ref.dtype)
    part_t = lambda col: kv[:, (col - COL_KC) * LANES:(col - COL_KC + 1) * LANES].T

    cos, sin = cos_ref[...], sin_ref[...]
    qt = qp.T
    for h in range(NSA_HEADS):
        qo_ref[0, h] = _norm_rope_t(qt[h * HEAD_DIM:(h + 1) * HEAD_DIM], gq_ref[...], cos, sin).astype(qo_ref.dtype)

    t0 = (pl.program_id(0) % tiles_per_seq) * tb
    block = (t0 + lax.broadcasted_iota(jnp.int32, (HEAD_DIM, tb), 1)) // SEL_BLOCK
    row_i = lax.broadcasted_iota(jnp.int32, (HEAD_DIM, tb), 0)
    tails = (jnp.where(row_i == block, 1.0, 0.0), jnp.where(row_i == 0, 1.0, 0.0))
    row_ones = jnp.ones((V_ROWS - HEAD_DIM, tb), F32)
    for j, (kcol, vcol, ko_ref, vo_ref) in enumerate(((COL_KS, COL_VS, kso_ref, vso_ref),
                                                      (COL_KW, COL_VW, kwo_ref, vwo_ref))):
        kt, vt = part_t(kcol), part_t(vcol)
        for g in range(NSA_KV):
            kh = _norm_rope_t(kt[g * HEAD_DIM:(g + 1) * HEAD_DIM], gk_ref[j], cos, sin)
            ko_ref[0, g] = jnp.concatenate([kh, tails[j]], axis=0).T.astype(ko_ref.dtype)
        for g in range(NSA_KV):
            vo_ref[0, g] = jnp.concatenate([vt[g * HEAD_DIM:(g + 1) * HEAD_DIM], row_ones], axis=0).astype(vo_ref.dtype)


def _in_proj(x2, g, w_pad, q_norm_g, k_norm_g, batch, seq):
    n = x2.shape[0]
    nt = seq // ROW_TILE
    assert seq // SEL_BLOCK <= HEAD_DIM
    cos, sin = _rope_tables(np.arange(seq))
    half = ROT_DIM // 2
    cos_t, sin_t = jnp.asarray(cos[:, :half].T.copy()), jnp.asarray(sin[:, :half].T.copy())
    lanes = lambda gain: jnp.tile(gain[..., None], (1,) * gain.ndim + (ROW_TILE,))
    gq = lanes(q_norm_g * (HEAD_DIM ** -0.5 * LOG2E))
    gk = lanes(k_norm_g[1:3])
    const = lambda i: (0, 0)
    rows_major = lambda d: pl.BlockSpec((1, NSA_KV, ROW_TILE, d), lambda i: (i // nt, 0, i % nt, 0))
    cols_major = lambda heads, d: pl.BlockSpec((1, heads, d, ROW_TILE), lambda i: (i // nt, 0, 0, i % nt))
    sds = lambda *shape: jax.ShapeDtypeStruct((batch,) + shape, BF16)
    return pl.pallas_call(
        functools.partial(_in_proj_kernel, tiles_per_seq=nt),
        grid=(n // ROW_TILE,),
        in_specs=[pl.BlockSpec((ROW_TILE, D_MODEL), lambda i: (i, 0)),
                  pl.BlockSpec((1, D_MODEL), const),
                  pl.BlockSpec((D_MODEL, IN_PAD), const),
                  pl.BlockSpec((half, ROW_TILE), lambda i: (0, i % nt)),
                  pl.BlockSpec((half, ROW_TILE), lambda i: (0, i % nt)),
                  pl.BlockSpec((HEAD_DIM, ROW_TILE), const),
                  pl.BlockSpec((2, HEAD_DIM, ROW_TILE), lambda i: (0, 0, 0))],
        out_specs=[pl.BlockSpec((ROW_TILE, IN_PAD), lambda i: (i, 0)),
                   cols_major(NSA_HEADS, HEAD_DIM), rows_major(2 * HEAD_DIM), cols_major(NSA_KV, V_ROWS),
                   rows_major(2 * HEAD_DIM), cols_major(NSA_KV, V_ROWS)],
        out_shape=[jax.ShapeDtypeStruct((n, IN_PAD), BF16),
                   sds(NSA_HEADS, HEAD_DIM, seq), sds(NSA_KV, seq, 2 * HEAD_DIM), sds(NSA_KV, V_ROWS, seq),
                   sds(NSA_KV, seq, 2 * HEAD_DIM), sds(NSA_KV, V_ROWS, seq)],
        compiler_params=pltpu.CompilerParams(dimension_semantics=("parallel",),
                                             vmem_limit_bytes=VMEM_LIMIT),
        name="in_proj",
    )(x2, g, w_pad, cos_t, sin_t, gq, gk)


def _compress_kernel(xk_ref, xv_ref, pea_ref, peb_ref, w1_ref, w2_ref, gk_ref, cos_ref, sin_ref,
                     rot_ref, kc_ref, vc_ref, shift_scr):
    nc = xk_ref.shape[2]
    half = w1_ref.shape[1] // 2
    for j, x_ref in enumerate((xk_ref, xv_ref)):
        x = x_ref[0, 0].astype(F32)
        u = _dot((x + pea_ref[j:j + 1, :]).astype(BF16), w1_ref[j, :half, :])
        v = _dot((x + peb_ref[j:j + 1, :]).astype(BF16), w1_ref[j, half:, :])
        shift_scr[0:nc, :] = v
        shift_scr[nc:nc + HALO, :] = jnp.zeros((HALO, v.shape[1]), F32)
        hid = _silu(u + shift_scr[1:nc + 1, :])
        out = _dot(hid.astype(BF16), w2_ref[j])
        if j == 0:
            xn = _rms(out, gk_ref[...])
            out = xn * cos_ref[...] + _dot(xn.astype(BF16), rot_ref[...]) * sin_ref[...]
            kc_ref[0, 0] = out.astype(kc_ref.dtype)
        else:
            wide = jnp.concatenate([out, jnp.zeros_like(out)], axis=1)
            vc_ref[0, 0] = wide.T[:HEAD_DIM].astype(vc_ref.dtype)


def _compress(xk, xv, cmp_pe, cmp_w1, cmp_w2, gk0, batch, seq):
    nc = seq // CMP_STRIDE
    grp = CMP_STRIDE * HEAD_DIM
    cos, sin = (jnp.asarray(t) for t in _rope_tables(np.arange(nc) * CMP_STRIDE + CMP_LEN - 1))
    pea = cmp_pe[:, :CMP_STRIDE].reshape(2, grp)
    peb = cmp_pe[:, CMP_STRIDE:].reshape(2, grp)
    blk = pl.BlockSpec((1, 1, nc, grp), lambda b, g: (b, g, 0, 0))
    const2 = lambda b, g: (0, 0)
    const3 = lambda b, g: (0, 0, 0)
    return pl.pallas_call(
        _compress_kernel,
        grid=(batch, NSA_KV),
        in_specs=[blk, blk,
                  pl.BlockSpec((2, grp), const2), pl.BlockSpec((2, grp), const2),
                  pl.BlockSpec(cmp_w1.shape, const3), pl.BlockSpec(cmp_w2.shape, const3),
                  pl.BlockSpec((1, HEAD_DIM), const2),
                  pl.BlockSpec((nc, HEAD_DIM), const2), pl.BlockSpec((nc, HEAD_DIM), const2),
                  pl.BlockSpec((HEAD_DIM, HEAD_DIM), const2)],
        out_specs=[pl.BlockSpec((1, 1, nc, HEAD_DIM), lambda b, g: (b, g, 0, 0)),
                   pl.BlockSpec((1, 1, HEAD_DIM, nc), lambda b, g: (b, g, 0, 0))],
        out_shape=[jax.ShapeDtypeStruct((batch, NSA_KV, nc, HEAD_DIM), BF16),
                   jax.ShapeDtypeStruct((batch, NSA_KV, HEAD_DIM, nc), BF16)],
        scratch_shapes=[pltpu.VMEM((nc + HALO, cmp_w1.shape[2]), F32)],
        compiler_params=pltpu.CompilerParams(dimension_semantics=("parallel", "parallel"),
                                             vmem_limit_bytes=VMEM_LIMIT),
        name="nsa_compress",
    )(xk, xv, pea, peb, cmp_w1.astype(BF16), cmp_w2.astype(BF16), gk0[None, :], cos, sin,
      jnp.asarray(_rot_matrix(1), BF16))


NSA_TQ = 256


def _overlap_t(seq):
    ns, nc = seq // SEL_BLOCK, seq // CMP_STRIDE
    cs = np.arange(nc)[None, :] * CMP_STRIDE
    ss = np.arange(ns)[:, None] * SEL_BLOCK
    ov = np.clip(np.minimum(cs + CMP_LEN, ss + SEL_BLOCK) - np.maximum(cs, ss), 0, None)
    return (ov / CMP_LEN).astype(np.float32)


def _cmpsel_kernel(offset_ref, q_ref, kc_ref, vc_ref, ovt_ref, ocmp_ref, bias_ref, val_scr, cnt_scr, *, top):
    tq, nc = q_ref.shape[3], kc_ref.shape[2]
    ns = ovt_ref.shape[0]
    t0 = pl.program_id(1) * tq
    n_i = lax.broadcasted_iota(jnp.int32, (ns, tq), 0)
    t_i = t0 + lax.broadcasted_iota(jnp.int32, (ns, tq), 1)
    cur = t_i // SEL_BLOCK
    forced = jnp.where(n_i == 0, FORCE, jnp.where(n_i == cur, FORCE, jnp.where(n_i == cur - 1, FORCE, 0.0)))
    visible = n_i * SEL_BLOCK <= t_i

    def attend(g, ncv):
        c_i = lax.broadcasted_iota(jnp.int32, (ncv, NSA_GROUP * tq), 0)
        q_i = lax.broadcasted_iota(jnp.int32, (ncv, NSA_GROUP * tq), 1) & (tq - 1)
        cmask = c_i * CMP_STRIDE + (CMP_LEN - 1) <= t0 + q_i
        qg = jnp.concatenate([q_ref[0, g * NSA_GROUP + h] for h in range(NSA_GROUP)], axis=1)
        s = jnp.where(cmask, _dot(kc_ref[0, g, 0:ncv, :], qg), NEG)
        e = jnp.exp2(s - jnp.max(s, axis=0, keepdims=True))
        p = jnp.where(cmask, e * (1.0 / jnp.sum(e, axis=0, keepdims=True)), 0.0)
        o = _dot(vc_ref[0, g, :, 0:ncv], p.astype(BF16)).astype(ocmp_ref.dtype)
        psum = p[:, :tq]
        for h in range(NSA_GROUP):
            ocmp_ref[0, g * NSA_GROUP + h] = o[:, h * tq:(h + 1) * tq]
            if h:
                psum += p[:, h * tq:(h + 1) * tq]
        ovt = ovt_ref[:, 0:ncv]
        hi, lo = _split(psum)
        val_scr[...] = jnp.where(visible, _dot(ovt, hi) + _dot(ovt, lo) + forced, NEG)

    nc_quarter = nc // 4
    prefix = ((t0 + tq) // CMP_STRIDE - 1) // nc_quarter
    for g in range(NSA_KV):
        for v in range(4):
            @pl.when(prefix == v)
            def _():
                attend(g, (v + 1) * nc_quarter)
        cnt_scr[...] = jnp.zeros_like(cnt_scr)
        sub = lax.broadcasted_iota(jnp.int32, (SUBLANES, tq), 0)
        quarter = max(ns // 4, SUBLANES)
        last_quarter = (t0 + tq - 1) // (SEL_BLOCK * quarter)
        for rq in range(ns // quarter):
            for mq in range(ns // quarter):
                @pl.when(last_quarter >= max(rq, mq))
                def _():
                    for r in range(rq * quarter // SUBLANES, (rq + 1) * quarter // SUBLANES):
                        rows = pl.ds(r * SUBLANES, SUBLANES)
                        v_r = val_scr[rows, :]
                        c = cnt_scr[rows, :]
                        for m in range(mq * quarter, (mq + 1) * quarter):
                            row = val_scr[m:m + 1, :]
                            if m // SUBLANES > r:
                                hit = row > v_r
                            elif m // SUBLANES < r:
                                hit = row >= v_r
                            else:
                                hit = jnp.where(sub > m % SUBLANES, jnp.where(row >= v_r, 1.0, 0.0),
                                                jnp.where(row > v_r, 1.0, 0.0)) > 0.5
                            c = jnp.where(hit, c + 1.0, c)
                        cnt_scr[rows, :] = c
        bias = jnp.where(cnt_scr[...] < top, offset_ref[0], MASK_BIAS)
        if ns < HEAD_DIM:
            bias = jnp.concatenate([bias, jnp.zeros((HEAD_DIM - ns, tq), F32)], axis=0)
        bias_ref[0, g] = bias.astype(bias_ref.dtype)


def _cmpsel(offset, qt, kc, vct, batch, seq):
    nt = seq // NSA_TQ
    nc, ns = seq // CMP_STRIDE, seq // SEL_BLOCK
    qblk = pl.BlockSpec((1, NSA_HEADS, HEAD_DIM, NSA_TQ), lambda b, t: (b, 0, 0, t))
    return pl.pallas_call(
        functools.partial(_cmpsel_kernel, top=min(SEL_TOPK, ns)),
        grid=(batch, nt),
        in_specs=[pl.BlockSpec(memory_space=pltpu.SMEM), qblk,
                  pl.BlockSpec((1, NSA_KV, nc, HEAD_DIM), lambda b, t: (b, 0, 0, 0)),
                  pl.BlockSpec((1, NSA_KV, HEAD_DIM, nc), lambda b, t: (b, 0, 0, 0)),
                  pl.BlockSpec((ns, nc), lambda b, t: (0, 0))],
        out_specs=[qblk, pl.BlockSpec((1, NSA_KV, HEAD_DIM, NSA_TQ), lambda b, t: (b, 0, 0, t))],
        out_shape=[jax.ShapeDtypeStruct((batch, NSA_HEADS, HEAD_DIM, seq), BF16),
                   jax.ShapeDtypeStruct((batch, NSA_KV, HEAD_DIM, seq), BF16)],
        scratch_shapes=[pltpu.VMEM((ns, NSA_TQ), F32), pltpu.VMEM((ns, NSA_TQ), F32)],
        compiler_params=pltpu.CompilerParams(dimension_semantics=("parallel", "parallel"),
                                             vmem_limit_bytes=VMEM_LIMIT),
        name="nsa_cmpsel",
    )(offset, qt, kc, vct, jnp.asarray(_overlap_t(seq), BF16))


SAFE_OFFSET = 60.0


def _attn_kernel(q_ref, ks_ref, vs_ref, kw_ref, vw_ref, bias_ref, wrow_ref, ocmp_ref, gate_ref, gout_ref,
                 o_ref, acc_scr, *scratch, bounded):
    tq = q_ref.shape[3]
    i = pl.program_id(1)
    t0 = i * tq
    q_sel, q_win = [], []
    for g in range(NSA_KV):
        q_grp = jnp.concatenate([q_ref[0, g * NSA_GROUP + h] for h in range(NSA_GROUP)], axis=1)
        q_sel.append(jnp.concatenate([q_grp, jnp.concatenate([bias_ref[0, g]] * NSA_GROUP, axis=1)], axis=0))
        q_win.append(jnp.concatenate([q_grp, wrow_ref[...]], axis=0))

    def per_head(s, keep):
        return jnp.concatenate([jnp.where(keep, s[:, h * tq:(h + 1) * tq], NEG) for h in range(NSA_GROUP)], axis=1)

    def causal(s):
        key_i = lax.broadcasted_iota(jnp.int32, (tq, tq), 0)
        return per_head(s, key_i <= lax.broadcasted_iota(jnp.int32, (tq, tq), 1))

    def tile(j):
        return pl.ds(pl.multiple_of(j * tq, tq), tq)

    acc_scr[...] = jnp.zeros_like(acc_scr)
    span = WINDOW + tq
    w0 = pl.multiple_of(jnp.maximum(t0 + tq - span, 0), tq)
    dist = (t0 - w0) + (lax.broadcasted_iota(jnp.int32, (span, tq), 1)
                        - lax.broadcasted_iota(jnp.int32, (span, tq), 0))
    in_window = jnp.abs(2 * dist - (WINDOW - 1)) < WINDOW
    win_scores = lambda g: per_head(_dot(kw_ref[0, g, pl.ds(w0, span), :], q_win[g]), in_window)

    if bounded:
        p0_scr, p1_scr = scratch

        def probs(j, p_scr, diag):
            for g in range(NSA_KV):
                s = _dot(ks_ref[0, g, tile(j), :], q_sel[g])
                p_scr[g] = jnp.exp2(causal(s) if diag else s).astype(BF16)

        def accumulate(j, p_scr):
            for g in range(NSA_KV):
                acc_scr[g] += _dot(vs_ref[0, g, :, tile(j)], p_scr[g])

        @pl.when(i > 0)
        def _():
            probs(0, p0_scr, False)

        def body(jj, carry):
            probs(2 * jj + 1, p1_scr, False)
            accumulate(2 * jj, p0_scr)
            probs(2 * jj + 2, p0_scr, False)
            accumulate(2 * jj + 1, p1_scr)
            return carry

        lax.fori_loop(0, (i - 1) // 2, body, 0)

        @pl.when((i > 0) & (i % 2 == 1))
        def _():
            accumulate(i - 1, p0_scr)

        @pl.when((i > 0) & (i % 2 == 0))
        def _():
            probs(i - 1, p1_scr, False)
            accumulate(i - 2, p0_scr)
            accumulate(i - 1, p1_scr)

        probs(i, p0_scr, True)
        s_win = [win_scores(g) for g in range(NSA_KV)]
        accumulate(i, p0_scr)
        win = [_dot(vw_ref[0, g, :, pl.ds(w0, span)], jnp.exp2(s_win[g]).astype(BF16)) for g in range(NSA_KV)]
    else:
        m_scr, s0_scr, s1_scr = scratch
        m_scr[...] = jnp.full(m_scr.shape, NEG, F32)

        def scores(j, s_scr):
            for g in range(NSA_KV):
                s_scr[g] = _dot(ks_ref[0, g, tile(j), :], q_sel[g])

        def consume(j, s_scr, diag):
            for g in range(NSA_KV):
                s = causal(s_scr[g]) if diag else s_scr[g]
                m_old = m_scr[g]
                m_new = jnp.maximum(m_old, jnp.max(s, axis=0, keepdims=True))
                p = jnp.exp2(s - m_new).astype(BF16)
                acc_scr[g] = jnp.exp2(m_old - m_new) * acc_scr[g] + _dot(vs_ref[0, g, :, tile(j)], p)
                m_scr[g] = m_new

        scores(0, s0_scr)

        def body(jj, carry):
            scores(2 * jj + 1, s1_scr)
            consume(2 * jj, s0_scr, False)
            scores(2 * jj + 2, s0_scr)
            consume(2 * jj + 1, s1_scr, False)
            return carry

        lax.fori_loop(0, i // 2, body, 0)

        @pl.when(i % 2 == 0)
        def _():
            consume(i, s0_scr, True)

        @pl.when(i % 2 == 1)
        def _():
            scores(i, s1_scr)
            consume(i - 1, s0_scr, False)
            consume(i, s1_scr, True)

        win = []
        for g in range(NSA_KV):
            s = win_scores(g)
            p = jnp.exp2(s - jnp.max(s, axis=0, keepdims=True)).astype(BF16)
            win.append(_dot(vw_ref[0, g, :, pl.ds(w0, span)], p))

    gate = jax.nn.sigmoid(gate_ref[...].astype(F32).T)
    gout = gout_ref[...]
    rows = []
    for hh in range(NSA_HEADS):
        g, lanes = hh // NSA_GROUP, slice((hh % NSA_GROUP) * tq, (hh % NSA_GROUP + 1) * tq)
        a_s, a_w = acc_scr[g][:, lanes], win[g][:, lanes]
        o = (gate[hh:hh + 1] * ocmp_ref[0, hh].astype(F32)
             + gate[NSA_HEADS + hh:NSA_HEADS + hh + 1] * (a_s[:HEAD_DIM] * (1.0 / a_s[HEAD_DIM:HEAD_DIM + 1]))
             + gate[2 * NSA_HEADS + hh:2 * NSA_HEADS + hh + 1] * (a_w[:HEAD_DIM] * (1.0 / a_w[HEAD_DIM:HEAD_DIM + 1])))
        rows.append(o * lax.rsqrt(jnp.mean(o * o, axis=0, keepdims=True) + EPS) * gout)
    o_ref[...] = jnp.concatenate(rows, axis=0).T.astype(o_ref.dtype)


def _attn(qt, ks, vst, kw, vwt, bias, wrow, ocmp, proj, gout, *, bounded):
    batch, _, _, seq = qt.shape
    nt = seq // NSA_TQ
    assert WINDOW % NSA_TQ == 0 and seq >= WINDOW + NSA_TQ
    wide = NSA_GROUP * NSA_TQ
    rows_full = pl.BlockSpec((1, NSA_KV, seq, 2 * HEAD_DIM), lambda b, t: (b, 0, 0, 0))
    cols_full = pl.BlockSpec((1, NSA_KV, V_ROWS, seq), lambda b, t: (b, 0, 0, 0))
    qtile = lambda heads: pl.BlockSpec((1, heads, HEAD_DIM, NSA_TQ), lambda b, t: (b, 0, 0, t))
    if bounded:
        scratch = [pltpu.VMEM((NSA_KV, NSA_TQ, wide), BF16), pltpu.VMEM((NSA_KV, NSA_TQ, wide), BF16)]
    else:
        scratch = [pltpu.VMEM((NSA_KV, 1, wide), F32),
                   pltpu.VMEM((NSA_KV, NSA_TQ, wide), F32), pltpu.VMEM((NSA_KV, NSA_TQ, wide), F32)]
    return pl.pallas_call(
        functools.partial(_attn_kernel, bounded=bounded),
        grid=(batch, nt),
        in_specs=[qtile(NSA_HEADS), rows_full, cols_full, rows_full, cols_full, qtile(NSA_KV),
                  pl.BlockSpec((HEAD_DIM, wide), lambda b, t: (0, 0)),
                  qtile(NSA_HEADS),
                  pl.BlockSpec((NSA_TQ, LANES), lambda b, t: (b * nt + t, COL_GATE)),
                  pl.BlockSpec((HEAD_DIM, NSA_TQ), lambda b, t: (0, 0))],
        out_specs=pl.BlockSpec((NSA_TQ, NSA_WIDTH), lambda b, t: (b * nt + t, 0)),
        out_shape=jax.ShapeDtypeStruct((batch * seq, NSA_WIDTH), BF16),
        scratch_shapes=[pltpu.VMEM((NSA_KV, V_ROWS, wide), F32)] + scratch,
        compiler_params=pltpu.CompilerParams(dimension_semantics=("parallel", "parallel"),
                                             vmem_limit_bytes=VMEM_LIMIT),
        name="nsa_attn" if bounded else "nsa_attn_online",
    )(qt, ks, vst, kw, vwt, bias, wrow, ocmp, proj, gout)


def _nsa(proj, qt, ks, vst, kw, vwt, q_norm_g, k_norm_g, cmp_pe, cmp_w1, cmp_w2, out_norm_g, batch, seq):
    def groups(col):
        a = proj[:, col * LANES:(col + 1) * LANES].reshape(batch, seq, NSA_KV, HEAD_DIM)
        return a.transpose(0, 2, 1, 3).reshape(batch, NSA_KV, seq // CMP_STRIDE, CMP_STRIDE * HEAD_DIM)

    def offset(gk):
        bound = HEAD_DIM * (HEAD_DIM ** -0.5 * LOG2E) * jnp.max(jnp.abs(q_norm_g)) * jnp.max(jnp.abs(gk))
        return (1.02 * bound).astype(BF16).astype(F32)

    off_sel, off_win = offset(k_norm_g[1]), offset(k_norm_g[2])
    kc, vct = _compress(groups(COL_KC), groups(COL_VC), cmp_pe, cmp_w1, cmp_w2, k_norm_g[0], batch, seq)
    ocmp, bias = _cmpsel(-off_sel[None], qt, kc, vct, batch, seq)
    wrow = jnp.zeros((HEAD_DIM, NSA_GROUP * NSA_TQ), F32).at[0].set(-off_win).astype(BF16)
    gout = jnp.tile(out_norm_g[:, None], (1, NSA_TQ))
    return lax.cond(jnp.maximum(off_sel, off_win) <= SAFE_OFFSET,
                    functools.partial(_attn, bounded=True), functools.partial(_attn, bounded=False),
                    qt, ks, vst, kw, vwt, bias, wrow, ocmp, proj, gout)


def kernel(x, p, attn_norm_g, w_in, hg_lb_logits, hg_norm_g, nsa_q_norm_g, nsa_k_norm_g, cmp_pe,
           cmp_w1, cmp_w2, nsa_out_norm_g, w_out, ffn_norm_g, w_up, conv_w, conv_b, w_down,
           ple_gate_norm_g, w_ple_gate, w_ple, ple_norm_g):
    B, T, _ = x.shape
    n = B * T
    h = x.reshape(n, D_MODEL)
    for i in range(w_in.shape[0]):
        w_pad = jnp.pad(w_in[i], ((0, 0), (0, IN_PAD - IN_TOTAL))).astype(BF16)
        proj, *qkv = _in_proj(h, attn_norm_g[i][None, :], w_pad, nsa_q_norm_g[i], nsa_k_norm_g[i], B, T)
        o_hg = _hgrn(proj, hg_lb_logits.astype(F32), hg_norm_g[i][None, :], i, B, T)
        o_nsa = _nsa(proj, *qkv, nsa_q_norm_g[i], nsa_k_norm_g[i], cmp_pe[i], cmp_w1[i], cmp_w2[i],
                     nsa_out_norm_g[i], B, T)
        h = _post_mixer(h, o_hg, o_nsa, p[i].reshape(n, PLE_DIM), w_out[i].astype(BF16), ffn_norm_g[i][None, :],
                        w_up[i].astype(BF16), conv_w[i], conv_b[i][None, :], w_down[i].astype(BF16),
                        ple_gate_norm_g[i][None, :], w_ple_gate[i].astype(BF16), w_ple[i].astype(BF16),
                        ple_norm_g[i][None, :], T)
    return h.reshape(B, T, D_MODEL)
```

```python
import functools

import numpy as np
import jax
import jax.numpy as jnp
from jax import lax
from jax.experimental import pallas as pl
from jax.experimental.pallas import tpu as pltpu

F32 = jnp.float32
BF16 = jnp.bfloat16

D_MODEL = 1024
PLE_DIM = 256
EPS = 1e-6
NEG = -1e30
FORCE = 1e6

HG_HEADS = 4
HG_KDIM = 128
HG_VDIM = 128
HG_WIDTH = HG_HEADS * HG_VDIM
HG_CHUNK = 64

NSA_HEADS = 8
NSA_KV = 2
NSA_GROUP = NSA_HEADS // NSA_KV
HEAD_DIM = 64
NSA_WIDTH = NSA_HEADS * HEAD_DIM
KV_WIDTH = NSA_KV * HEAD_DIM
CMP_LEN = 32
CMP_STRIDE = 16
SEL_BLOCK = 64
SEL_TOPK = 16
WINDOW = 512
ROT_DIM = HEAD_DIM // 4
ROPE_THETA = 500000.0
D_FF = 2816
CONV_W = 3

IN_SIZES = (HG_HEADS * HG_KDIM, HG_HEADS * HG_KDIM, HG_WIDTH, HG_WIDTH, NSA_WIDTH,
            KV_WIDTH, KV_WIDTH, KV_WIDTH, KV_WIDTH, KV_WIDTH, KV_WIDTH, 3 * NSA_HEADS)
IN_TOTAL = sum(IN_SIZES)
LANES = 128
IN_PAD = -(-IN_TOTAL // LANES) * LANES
VMEM_LIMIT = 56 * 1024 * 1024

ROW_TILE = 512
FF_CHUNK = 256
SUBLANES = 8
HALO = SUBLANES
FFN_HALO = 2 * SUBLANES


def _rms(x, g):
    return x * lax.rsqrt(jnp.mean(x * x, axis=-1, keepdims=True) + EPS) * g


def _dot(a, b):
    return jnp.dot(a, b, preferred_element_type=F32)


def _dot_nt(a, b):
    return lax.dot_general(a, b, (((1,), (1,)), ((), ())), preferred_element_type=F32)


def _dot_tn(a, b):
    return lax.dot_general(a, b, (((0,), (0,)), ((), ())), preferred_element_type=F32)


def _split(x):
    hi = x.astype(BF16)
    return hi, (x - hi.astype(F32)).astype(BF16)


def _silu(x):
    hx = 0.5 * x
    return hx + hx * jnp.tanh(hx)


def _post_mixer_kernel(x_ref, xh_ref, a_ref, ah_ref, b_ref, bh_ref, p_ref, wo_ref, gf_ref, wup_ref, cw_ref,
                       cb_ref, wdn_ref, gg_ref, wg_ref, wp_ref, gp_ref, o_ref,
                       h1_scr, hn_scr, ug_scr, uu_scr, act_scr, *, tiles_per_seq):
    rows = x_ref.shape[0]
    halo, body = slice(0, FFN_HALO), slice(FFN_HALO, FFN_HALO + rows)
    hn_scr[halo, :HG_WIDTH], hn_scr[halo, HG_WIDTH:] = ah_ref[...], bh_ref[...]
    hn_scr[body, :HG_WIDTH], hn_scr[body, HG_WIDTH:] = a_ref[...], b_ref[...]
    mixed = _dot(hn_scr[...], wo_ref[...])
    h1_scr[halo, :] = xh_ref[...] + mixed[halo]
    h1_scr[body, :] = x_ref[...] + mixed[body]
    first = (pl.program_id(0) % tiles_per_seq) == 0
    hn_scr[halo, :] = jnp.where(first, 0.0, _rms(h1_scr[halo, :], gf_ref[...])).astype(BF16)
    hn_scr[body, :] = _rms(h1_scr[body, :], gf_ref[...]).astype(BF16)
    for c in range(0, D_FF, FF_CHUNK):
        conv = []
        for scr, off in ((ug_scr, c), (uu_scr, D_FF + c)):
            scr[...] = _dot(hn_scr[...], wup_ref[:, off:off + FF_CHUNK])
            cw = cw_ref[:, off:off + FF_CHUNK]
            conv.append(scr[FFN_HALO - 2:FFN_HALO - 2 + rows, :] * cw[0:1, :]
                        + scr[FFN_HALO - 1:FFN_HALO - 1 + rows, :] * cw[1:2, :]
                        + scr[body, :] * cw[2:3, :]
                        + cb_ref[:, off:off + FF_CHUNK])
        gate, up = conv
        act_scr[:, c:c + FF_CHUNK] = (_silu(gate) * up).astype(BF16)
    o_ref[...] = h1_scr[body, :] + _dot(act_scr[...], wdn_ref[...])
    hn_scr[body, :] = _rms(o_ref[...], gg_ref[...]).astype(BF16)
    e = _rms(_dot(p_ref[...].astype(BF16), wp_ref[...]), gp_ref[...])
    gate = jax.nn.sigmoid(_dot(hn_scr[body, :], wg_ref[...]))
    o_ref[...] = o_ref[...] + gate * e


def _post_mixer(x2, o_hg, o_nsa, p2, w_out, gf, w_up, conv_w, conv_b, w_down, gg, w_gate, w_ple, gp, seq):
    n = x2.shape[0]
    assert HG_WIDTH + NSA_WIDTH == D_MODEL
    const = lambda i: (0, 0)
    tile = lambda width: pl.BlockSpec((ROW_TILE, width), lambda i: (i, 0))
    halo_blocks = ROW_TILE // FFN_HALO
    halo = lambda width: pl.BlockSpec((FFN_HALO, width), lambda i: (jnp.maximum(i * halo_blocks - 1, 0), 0))
    weight = lambda r, c: pl.BlockSpec((r, c), const, pipeline_mode=pl.Buffered(1))
    padded = FFN_HALO + ROW_TILE
    return pl.pallas_call(
        functools.partial(_post_mixer_kernel, tiles_per_seq=seq // ROW_TILE),
        grid=(n // ROW_TILE,),
        in_specs=[tile(D_MODEL), halo(D_MODEL), tile(HG_WIDTH), halo(HG_WIDTH), tile(NSA_WIDTH), halo(NSA_WIDTH),
                  tile(PLE_DIM),
                  weight(D_MODEL, D_MODEL),
                  pl.BlockSpec((1, D_MODEL), const),
                  weight(D_MODEL, 2 * D_FF),
                  pl.BlockSpec((CONV_W, 2 * D_FF), const),
                  pl.BlockSpec((1, 2 * D_FF), const),
                  weight(D_FF, D_MODEL),
                  pl.BlockSpec((1, D_MODEL), const),
                  weight(D_MODEL, D_MODEL),
                  weight(PLE_DIM, D_MODEL),
                  pl.BlockSpec((1, D_MODEL), const)],
        out_specs=tile(D_MODEL),
        out_shape=jax.ShapeDtypeStruct((n, D_MODEL), F32),
        scratch_shapes=[pltpu.VMEM((padded, D_MODEL), F32),
                        pltpu.VMEM((padded, D_MODEL), BF16),
                        pltpu.VMEM((padded, FF_CHUNK), F32),
                        pltpu.VMEM((padded, FF_CHUNK), F32),
                        pltpu.VMEM((ROW_TILE, D_FF), BF16)],
        compiler_params=pltpu.CompilerParams(dimension_semantics=("parallel",),
                                             vmem_limit_bytes=VMEM_LIMIT),
        name="post_mixer",
    )(x2, x2, o_hg, o_hg, o_nsa, o_nsa, p2, w_out, gf, w_up, conv_w, conv_b, w_down, gg, w_gate, w_ple, gp)


HG_LEVELS = (32, 16, 8, 4, 2, 1)
HG_TB = 256


def _hgrn_level_matrices():
    c = HG_CHUNK
    r = np.arange(c)[:, None]
    u = np.arange(c)[None, :]
    rows = []
    for m in HG_LEVELS:
        r0 = (r // m) * m
        upper = (r & m) != 0
        rows.append(np.where(upper, (u >= r0) & (u <= r), (u >= r + 1) & (u <= r0 + m - 1)))
    rows.append(u <= r)
    rows.append(u > r)
    return np.concatenate(rows, 0).astype(np.float32)


def _hgrn_pair_masks():
    c = HG_CHUNK
    t = np.arange(c)[:, None]
    s = np.arange(c)[None, :]
    masks = [((t // (2 * m)) == (s // (2 * m))) & ((t & m) != 0) & ((s & m) == 0) for m in HG_LEVELS]
    masks.append(t == s)
    return np.stack(masks).astype(np.float32)


def _hgrn_kernel(q_ref, f_ref, v_ref, g_ref, lbl_ref, ng_ref, mall_ref, msk_ref, o_ref, st_ref,
                 *, layer, n_chunks):
    @pl.when(pl.program_id(1) == 0)
    def _():
        st_ref[...] = jnp.zeros_like(st_ref)

    lbl = lbl_ref[...]
    e = jnp.exp(lbl - jnp.max(lbl, axis=0, keepdims=True))
    sm = e / jnp.sum(e, axis=0, keepdims=True)
    lb = jnp.sum(sm[:layer + 1], axis=0, keepdims=True)
    mall = mall_ref[...]
    ng = ng_ref[...]
    c = HG_CHUNK
    nl = len(HG_LEVELS)

    intra, q_decayed, carry_decay, carry_add = {}, {}, {}, {}
    for ci in range(n_chunks):
        r0 = ci * c
        fg = lb + (1.0 - lb) * jax.nn.sigmoid(f_ref[r0:r0 + c, :].astype(F32))
        kk = 1.0 - fg
        ee = jnp.exp(_dot(mall, jnp.concatenate(_split(jnp.log(fg)), axis=0)))
        qf = _silu(q_ref[r0:r0 + c, :].astype(F32)) * HG_KDIM ** -0.5
        for h in range(HG_HEADS):
            sl = slice(h * HG_KDIM, (h + 1) * HG_KDIM)
            qh, kh = qf[:, sl], kk[:, sl]
            a = msk_ref[nl] * _dot_nt(qh.astype(BF16), kh.astype(BF16))
            for j in range(nl):
                ej = ee[j * c:(j + 1) * c, sl]
                a += msk_ref[j] * _dot_nt((qh * ej).astype(BF16), (kh * ej).astype(BF16))
            vh = v_ref[r0:r0 + c, sl]
            eb = ee[nl * c:(nl + 1) * c, sl]
            intra[ci, h] = _dot(a.astype(BF16), vh)
            q_decayed[ci, h] = (qh * eb).astype(BF16)
            carry_decay[ci, h] = eb[c - 1:c, :]
            carry_add[ci, h] = _dot_tn(vh, (kh * ee[(nl + 1) * c:(nl + 2) * c, sl]).astype(BF16))

    for h in range(HG_HEADS):
        sl = slice(h * HG_KDIM, (h + 1) * HG_KDIM)
        st = st_ref[h]
        states = []
        for ci in range(n_chunks):
            states.append(st.astype(BF16))
            st = carry_decay[ci, h] * st + carry_add[ci, h]
        st_ref[h] = st
        for ci in range(n_chunks):
            r0 = ci * c
            o = intra[ci, h] + _dot_nt(q_decayed[ci, h], states[ci])
            on = o * lax.rsqrt(jnp.mean(o * o, axis=-1, keepdims=True) + EPS) * ng
            o_ref[r0:r0 + c, sl] = (on * _silu(g_ref[r0:r0 + c, sl].astype(F32))).astype(o_ref.dtype)


def _hgrn(proj, lb_logits, norm_g, layer, batch, seq):
    nt = seq // HG_TB
    w = HG_HEADS * HG_KDIM
    col = lambda k: (lambda b, t: (b * nt + t, k))
    const2 = lambda b, t: (0, 0)
    return pl.pallas_call(
        functools.partial(_hgrn_kernel, layer=layer, n_chunks=HG_TB // HG_CHUNK),
        grid=(batch, nt),
        in_specs=[pl.BlockSpec((HG_TB, w), col(0)), pl.BlockSpec((HG_TB, w), col(1)),
                  pl.BlockSpec((HG_TB, w), col(2)), pl.BlockSpec((HG_TB, w), col(3)),
                  pl.BlockSpec(lb_logits.shape, const2),
                  pl.BlockSpec((1, HG_VDIM), const2),
                  pl.BlockSpec(((len(HG_LEVELS) + 2) * HG_CHUNK, 2 * HG_CHUNK), const2),
                  pl.BlockSpec((len(HG_LEVELS) + 1, HG_CHUNK, HG_CHUNK), lambda b, t: (0, 0, 0))],
        out_specs=pl.BlockSpec((HG_TB, w), lambda b, t: (b * nt + t, 0)),
        out_shape=jax.ShapeDtypeStruct((batch * seq, HG_WIDTH), BF16),
        scratch_shapes=[pltpu.VMEM((HG_HEADS, HG_VDIM, HG_KDIM), F32)],
        compiler_params=pltpu.CompilerParams(dimension_semantics=("parallel", "arbitrary"),
                                             vmem_limit_bytes=VMEM_LIMIT),
        name="hgrn2",
    )(proj, proj, proj, proj, lb_logits, norm_g,
      jnp.asarray(np.tile(_hgrn_level_matrices(), (1, 2)), BF16), jnp.asarray(_hgrn_pair_masks(), F32))


COL_NQ = 2048 // LANES
COL_KC, COL_VC, COL_KS, COL_VS, COL_KW, COL_VW, COL_GATE = (COL_NQ + 4 + k for k in range(7))
LOG2E = 1.4426950408889634
MASK_BIAS = -1e30
V_ROWS = HEAD_DIM + 2 * SUBLANES


def _rope_tables(pos):
    half = ROT_DIM // 2
    inv = np.float32(ROPE_THETA) ** (-np.arange(half, dtype=np.float32) / np.float32(half))
    ang = pos.astype(np.float32)[:, None] * inv[None, :]
    n = pos.shape[0]
    pad = HEAD_DIM - ROT_DIM
    cos = np.concatenate([np.cos(ang), np.cos(ang), np.ones((n, pad), np.float32)], axis=1)
    sin = np.concatenate([np.sin(ang), np.sin(ang), np.zeros((n, pad), np.float32)], axis=1)
    return cos.astype(np.float32), sin.astype(np.float32)


def _rot_matrix(heads):
    half = ROT_DIM // 2
    r = np.zeros((HEAD_DIM, HEAD_DIM), np.float32)
    for j in range(half):
        r[j + half, j] = -1.0
        r[j, j + half] = 1.0
    return np.kron(np.eye(heads, dtype=np.float32), r)


def _norm_rope_t(xt, gain, cos, sin):
    half = ROT_DIM // 2
    xn = xt * lax.rsqrt(jnp.mean(xt * xt, axis=0, keepdims=True) + EPS) * gain
    x1, x2 = xn[:half], xn[half:ROT_DIM]
    return jnp.concatenate([x1 * cos - x2 * sin, x2 * cos + x1 * sin, xn[ROT_DIM:]], axis=0)


def _in_proj_kernel(x_ref, g_ref, w_ref, cos_ref, sin_ref, gq_ref, gk_ref,
                    o_ref, qo_ref, kso_ref, vso_ref, kwo_ref, vwo_ref, *, tiles_per_seq):
    tb = x_ref.shape[0]
    xb = _rms(x_ref[...], g_ref[...]).astype(BF16)
    hg_cols = COL_NQ * LANES
    qp = _dot(xb, w_ref[:, hg_cols:hg_cols + NSA_WIDTH])
    o_ref[:, hg_cols:hg_cols + NSA_WIDTH] = qp.astype(o_ref.dtype)
    kv = _dot(xb, w_ref[:, COL_KC * LANES:])
    o_ref[:, COL_KC * LANES:] = kv.astype(o_ref.dtype)
    for c in range(0, hg_cols, hg_cols // 2):
        o_ref[:, c:c + hg_cols // 2] = _dot(xb, w_ref[:, c:c + hg_cols // 2]).astype(o_ref.dtype)
    part_t = lambda col: kv[:, (col - COL_KC) * LANES:(col - COL_KC + 1) * LANES].T

    cos, sin = cos_ref[...], sin_ref[...]
    qt = qp.T
    for h in range(NSA_HEADS):
        qo_ref[0, h] = _norm_rope_t(qt[h * HEAD_DIM:(h + 1) * HEAD_DIM], gq_ref[...], cos, sin).astype(qo_ref.dtype)

    t0 = (pl.program_id(0) % tiles_per_seq) * tb
    block = (t0 + lax.broadcasted_iota(jnp.int32, (HEAD_DIM, tb), 1)) // SEL_BLOCK
    row_i = lax.broadcasted_iota(jnp.int32, (HEAD_DIM, tb), 0)
    tails = (jnp.where(row_i == block, 1.0, 0.0), jnp.where(row_i == 0, 1.0, 0.0))
    row_ones = jnp.ones((V_ROWS - HEAD_DIM, tb), F32)
    for j, (kcol, vcol, ko_ref, vo_ref) in enumerate(((COL_KS, COL_VS, kso_ref, vso_ref),
                                                      (COL_KW, COL_VW, kwo_ref, vwo_ref))):
        kt, vt = part_t(kcol), part_t(vcol)
        for g in range(NSA_KV):
            kh = _norm_rope_t(kt[g * HEAD_DIM:(g + 1) * HEAD_DIM], gk_ref[j], cos, sin)
            ko_ref[0, g] = jnp.concatenate([kh, tails[j]], axis=0).T.astype(ko_ref.dtype)
        for g in range(NSA_KV):
            vo_ref[0, g] = jnp.concatenate([vt[g * HEAD_DIM:(g + 1) * HEAD_DIM], row_ones], axis=0).astype(vo_ref.dtype)


def _in_proj(x2, g, w_pad, q_norm_g, k_norm_g, batch, seq):
    n = x2.shape[0]
    nt = seq // ROW_TILE
    assert seq // SEL_BLOCK <= HEAD_DIM
    cos, sin = _rope_tables(np.arange(seq))
    half = ROT_DIM // 2
    cos_t, sin_t = jnp.asarray(cos[:, :half].T.copy()), jnp.asarray(sin[:, :half].T.copy())
    lanes = lambda gain: jnp.tile(gain[..., None], (1,) * gain.ndim + (ROW_TILE,))
    gq = lanes(q_norm_g * (HEAD_DIM ** -0.5 * LOG2E))
    gk = lanes(k_norm_g[1:3])
    const = lambda i: (0, 0)
    rows_major = lambda d: pl.BlockSpec((1, NSA_KV, ROW_TILE, d), lambda i: (i // nt, 0, i % nt, 0))
    cols_major = lambda heads, d: pl.BlockSpec((1, heads, d, ROW_TILE), lambda i: (i // nt, 0, 0, i % nt))
    sds = lambda *shape: jax.ShapeDtypeStruct((batch,) + shape, BF16)
    return pl.pallas_call(
        functools.partial(_in_proj_kernel, tiles_per_seq=nt),
        grid=(n // ROW_TILE,),
        in_specs=[pl.BlockSpec((ROW_TILE, D_MODEL), lambda i: (i, 0)),
                  pl.BlockSpec((1, D_MODEL), const),
                  pl.BlockSpec((D_MODEL, IN_PAD), const),
                  pl.BlockSpec((half, ROW_TILE), lambda i: (0, i % nt)),
                  pl.BlockSpec((half, ROW_TILE), lambda i: (0, i % nt)),
                  pl.BlockSpec((HEAD_DIM, ROW_TILE), const),
                  pl.BlockSpec((2, HEAD_DIM, ROW_TILE), lambda i: (0, 0, 0))],
        out_specs=[pl.BlockSpec((ROW_TILE, IN_PAD), lambda i: (i, 0)),
                   cols_major(NSA_HEADS, HEAD_DIM), rows_major(2 * HEAD_DIM), cols_major(NSA_KV, V_ROWS),
                   rows_major(2 * HEAD_DIM), cols_major(NSA_KV, V_ROWS)],
        out_shape=[jax.ShapeDtypeStruct((n, IN_PAD), BF16),
                   sds(NSA_HEADS, HEAD_DIM, seq), sds(NSA_KV, seq, 2 * HEAD_DIM), sds(NSA_KV, V_ROWS, seq),
                   sds(NSA_KV, seq, 2 * HEAD_DIM), sds(NSA_KV, V_ROWS, seq)],
        compiler_params=pltpu.CompilerParams(dimension_semantics=("parallel",),
                                             vmem_limit_bytes=VMEM_LIMIT),
        name="in_proj",
    )(x2, g, w_pad, cos_t, sin_t, gq, gk)


def _compress_kernel(xk_ref, xv_ref, pea_ref, peb_ref, w1_ref, w2_ref, gk_ref, cos_ref, sin_ref,
                     rot_ref, kc_ref, vc_ref, shift_scr):
    nc = xk_ref.shape[2]
    half = w1_ref.shape[1] // 2
    for j, x_ref in enumerate((xk_ref, xv_ref)):
        x = x_ref[0, 0].astype(F32)
        u = _dot((x + pea_ref[j:j + 1, :]).astype(BF16), w1_ref[j, :half, :])
        v = _dot((x + peb_ref[j:j + 1, :]).astype(BF16), w1_ref[j, half:, :])
        shift_scr[0:nc, :] = v
        shift_scr[nc:nc + HALO, :] = jnp.zeros((HALO, v.shape[1]), F32)
        hid = _silu(u + shift_scr[1:nc + 1, :])
        out = _dot(hid.astype(BF16), w2_ref[j])
        if j == 0:
            xn = _rms(out, gk_ref[...])
            out = xn * cos_ref[...] + _dot(xn.astype(BF16), rot_ref[...]) * sin_ref[...]
            kc_ref[0, 0] = out.astype(kc_ref.dtype)
        else:
            wide = jnp.concatenate([out, jnp.zeros_like(out)], axis=1)
            vc_ref[0, 0] = wide.T[:HEAD_DIM].astype(vc_ref.dtype)


def _compress(xk, xv, cmp_pe, cmp_w1, cmp_w2, gk0, batch, seq):
    nc = seq // CMP_STRIDE
    grp = CMP_STRIDE * HEAD_DIM
    cos, sin = (jnp.asarray(t) for t in _rope_tables(np.arange(nc) * CMP_STRIDE + CMP_LEN - 1))
    pea = cmp_pe[:, :CMP_STRIDE].reshape(2, grp)
    peb = cmp_pe[:, CMP_STRIDE:].reshape(2, grp)
    blk = pl.BlockSpec((1, 1, nc, grp), lambda b, g: (b, g, 0, 0))
    const2 = lambda b, g: (0, 0)
    const3 = lambda b, g: (0, 0, 0)
    return pl.pallas_call(
        _compress_kernel,
        grid=(batch, NSA_KV),
        in_specs=[blk, blk,
                  pl.BlockSpec((2, grp), const2), pl.BlockSpec((2, grp), const2),
                  pl.BlockSpec(cmp_w1.shape, const3), pl.BlockSpec(cmp_w2.shape, const3),
                  pl.BlockSpec((1, HEAD_DIM), const2),
                  pl.BlockSpec((nc, HEAD_DIM), const2), pl.BlockSpec((nc, HEAD_DIM), const2),
                  pl.BlockSpec((HEAD_DIM, HEAD_DIM), const2)],
        out_specs=[pl.BlockSpec((1, 1, nc, HEAD_DIM), lambda b, g: (b, g, 0, 0)),
                   pl.BlockSpec((1, 1, HEAD_DIM, nc), lambda b, g: (b, g, 0, 0))],
        out_shape=[jax.ShapeDtypeStruct((batch, NSA_KV, nc, HEAD_DIM), BF16),
                   jax.ShapeDtypeStruct((batch, NSA_KV, HEAD_DIM, nc), BF16)],
        scratch_shapes=[pltpu.VMEM((nc + HALO, cmp_w1.shape[2]), F32)],
        compiler_params=pltpu.CompilerParams(dimension_semantics=("parallel", "parallel"),
                                             vmem_limit_bytes=VMEM_LIMIT),
        name="nsa_compress",
    )(xk, xv, pea, peb, cmp_w1.astype(BF16), cmp_w2.astype(BF16), gk0[None, :], cos, sin,
      jnp.asarray(_rot_matrix(1), BF16))


NSA_TQ = 256


def _overlap_t(seq):
    ns, nc = seq // SEL_BLOCK, seq // CMP_STRIDE
    cs = np.arange(nc)[None, :] * CMP_STRIDE
    ss = np.arange(ns)[:, None] * SEL_BLOCK
    ov = np.clip(np.minimum(cs + CMP_LEN, ss + SEL_BLOCK) - np.maximum(cs, ss), 0, None)
    return (ov / CMP_LEN).astype(np.float32)


def _cmpsel_kernel(offset_ref, q_ref, kc_ref, vc_ref, ovt_ref, ocmp_ref, bias_ref, *, top):
    tq, nc = q_ref.shape[3], kc_ref.shape[2]
    ns = ovt_ref.shape[0]
    t0 = pl.program_id(1) * tq
    n_i = lax.broadcasted_iota(jnp.int32, (ns, tq), 0)
    t_i = t0 + lax.broadcasted_iota(jnp.int32, (ns, tq), 1)
    cur = t_i // SEL_BLOCK
    forced = jnp.where(n_i == 0, FORCE, jnp.where(n_i == cur, FORCE, jnp.where(n_i == cur - 1, FORCE, 0.0)))
    visible = n_i * SEL_BLOCK <= t_i
    sub = lax.broadcasted_iota(jnp.int32, (SUBLANES, tq), 0)

    def attend(g, quarters):
        ncv, nsv = quarters * nc // 4, quarters * ns // 4
        c_i = lax.broadcasted_iota(jnp.int32, (ncv, NSA_GROUP * tq), 0)
        q_i = lax.broadcasted_iota(jnp.int32, (ncv, NSA_GROUP * tq), 1) & (tq - 1)
        cmask = c_i * CMP_STRIDE + (CMP_LEN - 1) <= t0 + q_i
        qg = jnp.concatenate([q_ref[0, g * NSA_GROUP + h] for h in range(NSA_GROUP)], axis=1)
        s = jnp.where(cmask, _dot(kc_ref[0, g, 0:ncv, :], qg), NEG)
        e = jnp.exp2(s - jnp.max(s, axis=0, keepdims=True))
        p = jnp.where(cmask, e * (1.0 / jnp.sum(e, axis=0, keepdims=True)), 0.0)
        o = _dot(vc_ref[0, g, :, 0:ncv], p.astype(BF16)).astype(ocmp_ref.dtype)
        psum = p[:, :tq]
        for h in range(NSA_GROUP):
            ocmp_ref[0, g * NSA_GROUP + h] = o[:, h * tq:(h + 1) * tq]
            if h:
                psum += p[:, h * tq:(h + 1) * tq]
        ovt = ovt_ref[:, 0:ncv]
        hi, lo = _split(psum)
        val = jnp.where(visible, _dot(ovt, hi) + _dot(ovt, lo) + forced, NEG)
        groups = []
        for r in range(ns // SUBLANES):
            if r * SUBLANES >= nsv:
                groups.append(jnp.zeros((SUBLANES, tq), F32))
                continue
            v_r = val[r * SUBLANES:(r + 1) * SUBLANES]
            c = jnp.zeros((SUBLANES, tq), F32)
            for m in range(nsv):
                row = val[m:m + 1, :]
                if m // SUBLANES > r:
                    hit = row > v_r
                elif m // SUBLANES < r:
                    hit = row >= v_r
                else:
                    hit = jnp.where(sub > m % SUBLANES, jnp.where(row >= v_r, 1.0, 0.0),
                                    jnp.where(row > v_r, 1.0, 0.0)) > 0.5
                c = jnp.where(hit, c + 1.0, c)
            groups.append(c)
        bias = jnp.where(jnp.concatenate(groups, axis=0) < top, offset_ref[0], MASK_BIAS)
        if ns < HEAD_DIM:
            bias = jnp.concatenate([bias, jnp.zeros((HEAD_DIM - ns, tq), F32)], axis=0)
        bias_ref[0, g] = bias.astype(bias_ref.dtype)

    assert ns % (4 * SUBLANES) == 0 and nc % 4 == 0
    last_quarter = (t0 + tq - 1) // (nc // 4 * CMP_STRIDE)
    for g in range(NSA_KV):
        for v in range(4):
            @pl.when(last_quarter == v)
            def _():
                attend(g, v + 1)


def _cmpsel(offset, qt, kc, vct, batch, seq):
    nt = seq // NSA_TQ
    nc, ns = seq // CMP_STRIDE, seq // SEL_BLOCK
    qblk = pl.BlockSpec((1, NSA_HEADS, HEAD_DIM, NSA_TQ), lambda b, t: (b, 0, 0, t))
    return pl.pallas_call(
        functools.partial(_cmpsel_kernel, top=min(SEL_TOPK, ns)),
        grid=(batch, nt),
        in_specs=[pl.BlockSpec(memory_space=pltpu.SMEM), qblk,
                  pl.BlockSpec((1, NSA_KV, nc, HEAD_DIM), lambda b, t: (b, 0, 0, 0)),
                  pl.BlockSpec((1, NSA_KV, HEAD_DIM, nc), lambda b, t: (b, 0, 0, 0)),
                  pl.BlockSpec((ns, nc), lambda b, t: (0, 0))],
        out_specs=[qblk, pl.BlockSpec((1, NSA_KV, HEAD_DIM, NSA_TQ), lambda b, t: (b, 0, 0, t))],
        out_shape=[jax.ShapeDtypeStruct((batch, NSA_HEADS, HEAD_DIM, seq), BF16),
                   jax.ShapeDtypeStruct((batch, NSA_KV, HEAD_DIM, seq), BF16)],
        compiler_params=pltpu.CompilerParams(dimension_semantics=("parallel", "parallel"),
                                             vmem_limit_bytes=VMEM_LIMIT),
        name="nsa_cmpsel",
    )(offset, qt, kc, vct, jnp.asarray(_overlap_t(seq), BF16))


SAFE_OFFSET = 60.0


def _attn_kernel(q_ref, ks_ref, vs_ref, kw_ref, vw_ref, bias_ref, wrow_ref, ocmp_ref, gate_ref, gout_ref,
                 o_ref, acc_scr, *scratch, bounded):
    tq = q_ref.shape[3]
    i = pl.program_id(1)
    t0 = i * tq
    q_sel, q_win = [], []
    for g in range(NSA_KV):
        q_grp = jnp.concatenate([q_ref[0, g * NSA_GROUP + h] for h in range(NSA_GROUP)], axis=1)
        q_sel.append(jnp.concatenate([q_grp, jnp.concatenate([bias_ref[0, g]] * NSA_GROUP, axis=1)], axis=0))
        q_win.append(jnp.concatenate([q_grp, wrow_ref[...]], axis=0))

    def per_head(s, keep):
        return jnp.concatenate([jnp.where(keep, s[:, h * tq:(h + 1) * tq], NEG) for h in range(NSA_GROUP)], axis=1)

    def causal(s):
        key_i = lax.broadcasted_iota(jnp.int32, (tq, tq), 0)
        return per_head(s, key_i <= lax.broadcasted_iota(jnp.int32, (tq, tq), 1))

    def tile(j):
        return pl.ds(pl.multiple_of(j * tq, tq), tq)

    acc_scr[...] = jnp.zeros_like(acc_scr)
    span = WINDOW + tq
    w0 = pl.multiple_of(jnp.maximum(t0 + tq - span, 0), tq)
    dist = (t0 - w0) + (lax.broadcasted_iota(jnp.int32, (span, tq), 1)
                        - lax.broadcasted_iota(jnp.int32, (span, tq), 0))
    in_window = jnp.abs(2 * dist - (WINDOW - 1)) < WINDOW
    win_scores = lambda g: per_head(_dot(kw_ref[0, g, pl.ds(w0, span), :], q_win[g]), in_window)

    if bounded:
        p0_scr, p1_scr = scratch

        def probs(j, p_scr, diag):
            for g in range(NSA_KV):
                s = _dot(ks_ref[0, g, tile(j), :], q_sel[g])
                p_scr[g] = jnp.exp2(causal(s) if diag else s).astype(BF16)

        def accumulate(j, p_scr):
            for g in range(NSA_KV):
                acc_scr[g] += _dot(vs_ref[0, g, :, tile(j)], p_scr[g])

        @pl.when(i > 0)
        def _():
            probs(0, p0_scr, False)

        def body(jj, carry):
            probs(2 * jj + 1, p1_scr, False)
            accumulate(2 * jj, p0_scr)
            probs(2 * jj + 2, p0_scr, False)
            accumulate(2 * jj + 1, p1_scr)
            return carry

        lax.fori_loop(0, (i - 1) // 2, body, 0)

        @pl.when((i > 0) & (i % 2 == 1))
        def _():
            accumulate(i - 1, p0_scr)

        @pl.when((i > 0) & (i % 2 == 0))
        def _():
            probs(i - 1, p1_scr, False)
            accumulate(i - 2, p0_scr)
            accumulate(i - 1, p1_scr)

        probs(i, p0_scr, True)
        s_win = [win_scores(g) for g in range(NSA_KV)]
        accumulate(i, p0_scr)
        win = [_dot(vw_ref[0, g, :, pl.ds(w0, span)], jnp.exp2(s_win[g]).astype(BF16)) for g in range(NSA_KV)]
    else:
        m_scr, s0_scr, s1_scr = scratch
        m_scr[...] = jnp.full(m_scr.shape, NEG, F32)

        def scores(j, s_scr):
            for g in range(NSA_KV):
                s_scr[g] = _dot(ks_ref[0, g, tile(j), :], q_sel[g])

        def consume(j, s_scr, diag):
            for g in range(NSA_KV):
                s = causal(s_scr[g]) if diag else s_scr[g]
                m_old = m_scr[g]
                m_new = jnp.maximum(m_old, jnp.max(s, axis=0, keepdims=True))
                p = jnp.exp2(s - m_new).astype(BF16)
                acc_scr[g] = jnp.exp2(m_old - m_new) * acc_scr[g] + _dot(vs_ref[0, g, :, tile(j)], p)
                m_scr[g] = m_new

        scores(0, s0_scr)

        def body(jj, carry):
            scores(2 * jj + 1, s1_scr)
            consume(2 * jj, s0_scr, False)
            scores(2 * jj + 2, s0_scr)
            consume(2 * jj + 1, s1_scr, False)
            return carry

        lax.fori_loop(0, i // 2, body, 0)

        @pl.when(i % 2 == 0)
        def _():
            consume(i, s0_scr, True)

        @pl.when(i % 2 == 1)
        def _():
            scores(i, s1_scr)
            consume(i - 1, s0_scr, False)
            consume(i, s1_scr, True)

        win = []
        for g in range(NSA_KV):
            s = win_scores(g)
            p = jnp.exp2(s - jnp.max(s, axis=0, keepdims=True)).astype(BF16)
            win.append(_dot(vw_ref[0, g, :, pl.ds(w0, span)], p))

    gate = jax.nn.sigmoid(gate_ref[...].astype(F32).T)
    gout = gout_ref[...]
    rows = []
    for hh in range(NSA_HEADS):
        g, lanes = hh // NSA_GROUP, slice((hh % NSA_GROUP) * tq, (hh % NSA_GROUP + 1) * tq)
        a_s, a_w = acc_scr[g][:, lanes], win[g][:, lanes]
        o = (gate[hh:hh + 1] * ocmp_ref[0, hh].astype(F32)
             + gate[NSA_HEADS + hh:NSA_HEADS + hh + 1] * (a_s[:HEAD_DIM] * (1.0 / a_s[HEAD_DIM:HEAD_DIM + 1]))
             + gate[2 * NSA_HEADS + hh:2 * NSA_HEADS + hh + 1] * (a_w[:HEAD_DIM] * (1.0 / a_w[HEAD_DIM:HEAD_DIM + 1])))
        rows.append(o * lax.rsqrt(jnp.mean(o * o, axis=0, keepdims=True) + EPS) * gout)
    o_ref[...] = jnp.concatenate(rows, axis=0).T.astype(o_ref.dtype)


def _attn(qt, ks, vst, kw, vwt, bias, wrow, ocmp, proj, gout, *, bounded):
    batch, _, _, seq = qt.shape
    nt = seq // NSA_TQ
    assert WINDOW % NSA_TQ == 0 and seq >= WINDOW + NSA_TQ
    wide = NSA_GROUP * NSA_TQ
    rows_full = pl.BlockSpec((1, NSA_KV, seq, 2 * HEAD_DIM), lambda b, t: (b, 0, 0, 0))
    cols_full = pl.BlockSpec((1, NSA_KV, V_ROWS, seq), lambda b, t: (b, 0, 0, 0))
    qtile = lambda heads: pl.BlockSpec((1, heads, HEAD_DIM, NSA_TQ), lambda b, t: (b, 0, 0, t))
    if bounded:
        scratch = [pltpu.VMEM((NSA_KV, NSA_TQ, wide), BF16), pltpu.VMEM((NSA_KV, NSA_TQ, wide), BF16)]
    else:
        scratch = [pltpu.VMEM((NSA_KV, 1, wide), F32),
                   pltpu.VMEM((NSA_KV, NSA_TQ, wide), F32), pltpu.VMEM((NSA_KV, NSA_TQ, wide), F32)]
    return pl.pallas_call(
        functools.partial(_attn_kernel, bounded=bounded),
        grid=(batch, nt),
        in_specs=[qtile(NSA_HEADS), rows_full, cols_full, rows_full, cols_full, qtile(NSA_KV),
                  pl.BlockSpec((HEAD_DIM, wide), lambda b, t: (0, 0)),
                  qtile(NSA_HEADS),
                  pl.BlockSpec((NSA_TQ, LANES), lambda b, t: (b * nt + t, COL_GATE)),
                  pl.BlockSpec((HEAD_DIM, NSA_TQ), lambda b, t: (0, 0))],
        out_specs=pl.BlockSpec((NSA_TQ, NSA_WIDTH), lambda b, t: (b * nt + t, 0)),
        out_shape=jax.ShapeDtypeStruct((batch * seq, NSA_WIDTH), BF16),
        scratch_shapes=[pltpu.VMEM((NSA_KV, V_ROWS, wide), F32)] + scratch,
        compiler_params=pltpu.CompilerParams(dimension_semantics=("parallel", "parallel"),
                                             vmem_limit_bytes=VMEM_LIMIT),
        name="nsa_attn" if bounded else "nsa_attn_online",
    )(qt, ks, vst, kw, vwt, bias, wrow, ocmp, proj, gout)


def _nsa(proj, qt, ks, vst, kw, vwt, q_norm_g, k_norm_g, cmp_pe, cmp_w1, cmp_w2, out_norm_g, batch, seq):
    def groups(col):
        a = proj[:, col * LANES:(col + 1) * LANES].reshape(batch, seq, NSA_KV, HEAD_DIM)
        return a.transpose(0, 2, 1, 3).reshape(batch, NSA_KV, seq // CMP_STRIDE, CMP_STRIDE * HEAD_DIM)

    def offset(gk):
        bound = HEAD_DIM * (HEAD_DIM ** -0.5 * LOG2E) * jnp.max(jnp.abs(q_norm_g)) * jnp.max(jnp.abs(gk))
        return (1.02 * bound).astype(BF16).astype(F32)

    off_sel, off_win = offset(k_norm_g[1]), offset(k_norm_g[2])
    kc, vct = _compress(groups(COL_KC), groups(COL_VC), cmp_pe, cmp_w1, cmp_w2, k_norm_g[0], batch, seq)
    ocmp, bias = _cmpsel(-off_sel[None], qt, kc, vct, batch, seq)
    wrow = jnp.zeros((HEAD_DIM, NSA_GROUP * NSA_TQ), F32).at[0].set(-off_win).astype(BF16)
    gout = jnp.tile(out_norm_g[:, None], (1, NSA_TQ))
    return lax.cond(jnp.maximum(off_sel, off_win) <= SAFE_OFFSET,
                    functools.partial(_attn, bounded=True), functools.partial(_attn, bounded=False),
                    qt, ks, vst, kw, vwt, bias, wrow, ocmp, proj, gout)


def kernel(x, p, attn_norm_g, w_in, hg_lb_logits, hg_norm_g, nsa_q_norm_g, nsa_k_norm_g, cmp_pe,
           cmp_w1, cmp_w2, nsa_out_norm_g, w_out, ffn_norm_g, w_up, conv_w, conv_b, w_down,
           ple_gate_norm_g, w_ple_gate, w_ple, ple_norm_g):
    B, T, _ = x.shape
    n = B * T
    h = x.reshape(n, D_MODEL)
    for i in range(w_in.shape[0]):
        w_pad = jnp.pad(w_in[i], ((0, 0), (0, IN_PAD - IN_TOTAL))).astype(BF16)
        proj, *qkv = _in_proj(h, attn_norm_g[i][None, :], w_pad, nsa_q_norm_g[i], nsa_k_norm_g[i], B, T)
        o_hg = _hgrn(proj, hg_lb_logits.astype(F32), hg_norm_g[i][None, :], i, B, T)
        o_nsa = _nsa(proj, *qkv, nsa_q_norm_g[i], nsa_k_norm_g[i], cmp_pe[i], cmp_w1[i], cmp_w2[i],
                     nsa_out_norm_g[i], B, T)
        h = _post_mixer(h, o_hg, o_nsa, p[i].reshape(n, PLE_DIM), w_out[i].astype(BF16), ffn_norm_g[i][None, :],
                        w_up[i].astype(BF16), conv_w[i], conv_b[i][None, :], w_down[i].astype(BF16),
                        ple_gate_norm_g[i][None, :], w_ple_gate[i].astype(BF16), w_ple[i].astype(BF16),
                        ple_norm_g[i][None, :], T)
    return h.reshape(B, T, D_MODEL)
```

```python
import functools

import numpy as np
import jax
import jax.numpy as jnp
from jax import lax
from jax.experimental import pallas as pl
from jax.experimental.pallas import tpu as pltpu

F32 = jnp.float32
BF16 = jnp.bfloat16

D_MODEL = 1024
PLE_DIM = 256
EPS = 1e-6
NEG = -1e30
FORCE = 1e6

HG_HEADS = 4
HG_KDIM = 128
HG_VDIM = 128
HG_WIDTH = HG_HEADS * HG_VDIM
HG_CHUNK = 64

NSA_HEADS = 8
NSA_KV = 2
NSA_GROUP = NSA_HEADS // NSA_KV
HEAD_DIM = 64
NSA_WIDTH = NSA_HEADS * HEAD_DIM
KV_WIDTH = NSA_KV * HEAD_DIM
CMP_LEN = 32
CMP_STRIDE = 16
SEL_BLOCK = 64
SEL_TOPK = 16
WINDOW = 512
ROT_DIM = HEAD_DIM // 4
ROPE_THETA = 500000.0
D_FF = 2816
CONV_W = 3

IN_SIZES = (HG_HEADS * HG_KDIM, HG_HEADS * HG_KDIM, HG_WIDTH, HG_WIDTH, NSA_WIDTH,
            KV_WIDTH, KV_WIDTH, KV_WIDTH, KV_WIDTH, KV_WIDTH, KV_WIDTH, 3 * NSA_HEADS)
IN_TOTAL = sum(IN_SIZES)
LANES = 128
IN_PAD = -(-IN_TOTAL // LANES) * LANES
VMEM_LIMIT = 56 * 1024 * 1024

ROW_TILE = 512
FF_CHUNK = 256
SUBLANES = 8
HALO = SUBLANES
FFN_HALO = 2 * SUBLANES


def _rms(x, g):
    return x * lax.rsqrt(jnp.mean(x * x, axis=-1, keepdims=True) + EPS) * g


def _dot(a, b):
    return jnp.dot(a, b, preferred_element_type=F32)


def _dot_nt(a, b):
    return lax.dot_general(a, b, (((1,), (1,)), ((), ())), preferred_element_type=F32)


def _dot_tn(a, b):
    return lax.dot_general(a, b, (((0,), (0,)), ((), ())), preferred_element_type=F32)


def _split(x):
    hi = x.astype(BF16)
    return hi, (x - hi.astype(F32)).astype(BF16)


def _silu(x):
    hx = 0.5 * x
    return hx + hx * jnp.tanh(hx)


def _post_mixer_kernel(x_ref, xh_ref, a_ref, ah_ref, b_ref, bh_ref, p_ref, wo_ref, gf_ref, wup_ref, cw_ref,
                       cb_ref, wdn_ref, gg_ref, wg_ref, wp_ref, gp_ref, o_ref,
                       h1_scr, hn_scr, ug_scr, uu_scr, act_scr, *, tiles_per_seq):
    rows = x_ref.shape[0]
    halo, body = slice(0, FFN_HALO), slice(FFN_HALO, FFN_HALO + rows)
    hn_scr[halo, :HG_WIDTH], hn_scr[halo, HG_WIDTH:] = ah_ref[...], bh_ref[...]
    hn_scr[body, :HG_WIDTH], hn_scr[body, HG_WIDTH:] = a_ref[...], b_ref[...]
    mixed = _dot(hn_scr[...], wo_ref[...])
    h1_scr[halo, :] = xh_ref[...] + mixed[halo]
    h1_scr[body, :] = x_ref[...] + mixed[body]
    first = (pl.program_id(0) % tiles_per_seq) == 0
    hn_scr[halo, :] = jnp.where(first, 0.0, _rms(h1_scr[halo, :], gf_ref[...])).astype(BF16)
    hn_scr[body, :] = _rms(h1_scr[body, :], gf_ref[...]).astype(BF16)
    for c in range(0, D_FF, FF_CHUNK):
        conv = []
        for scr, off in ((ug_scr, c), (uu_scr, D_FF + c)):
            scr[...] = _dot(hn_scr[...], wup_ref[:, off:off + FF_CHUNK])
            cw = cw_ref[:, off:off + FF_CHUNK]
            conv.append(scr[FFN_HALO - 2:FFN_HALO - 2 + rows, :] * cw[0:1, :]
                        + scr[FFN_HALO - 1:FFN_HALO - 1 + rows, :] * cw[1:2, :]
                        + scr[body, :] * cw[2:3, :]
                        + cb_ref[:, off:off + FF_CHUNK])
        gate, up = conv
        act_scr[:, c:c + FF_CHUNK] = (_silu(gate) * up).astype(BF16)
    o_ref[...] = h1_scr[body, :] + _dot(act_scr[...], wdn_ref[...])
    hn_scr[body, :] = _rms(o_ref[...], gg_ref[...]).astype(BF16)
    e = _rms(_dot(p_ref[...].astype(BF16), wp_ref[...]), gp_ref[...])
    gate = jax.nn.sigmoid(_dot(hn_scr[body, :], wg_ref[...]))
    o_ref[...] = o_ref[...] + gate * e


def _post_mixer(x2, o_hg, o_nsa, p2, w_out, gf, w_up, conv_w, conv_b, w_down, gg, w_gate, w_ple, gp, seq):
    n = x2.shape[0]
    assert HG_WIDTH + NSA_WIDTH == D_MODEL
    const = lambda i: (0, 0)
    tile = lambda width: pl.BlockSpec((ROW_TILE, width), lambda i: (i, 0))
    halo_blocks = ROW_TILE // FFN_HALO
    halo = lambda width: pl.BlockSpec((FFN_HALO, width), lambda i: (jnp.maximum(i * halo_blocks - 1, 0), 0))
    weight = lambda r, c: pl.BlockSpec((r, c), const, pipeline_mode=pl.Buffered(1))
    padded = FFN_HALO + ROW_TILE
    return pl.pallas_call(
        functools.partial(_post_mixer_kernel, tiles_per_seq=seq // ROW_TILE),
        grid=(n // ROW_TILE,),
        in_specs=[tile(D_MODEL), halo(D_MODEL), tile(HG_WIDTH), halo(HG_WIDTH), tile(NSA_WIDTH), halo(NSA_WIDTH),
                  tile(PLE_DIM),
                  weight(D_MODEL, D_MODEL),
                  pl.BlockSpec((1, D_MODEL), const),
                  weight(D_MODEL, 2 * D_FF),
                  pl.BlockSpec((CONV_W, 2 * D_FF), const),
                  pl.BlockSpec((1, 2 * D_FF), const),
                  weight(D_FF, D_MODEL),
                  pl.BlockSpec((1, D_MODEL), const),
                  weight(D_MODEL, D_MODEL),
                  weight(PLE_DIM, D_MODEL),
                  pl.BlockSpec((1, D_MODEL), const)],
        out_specs=tile(D_MODEL),
        out_shape=jax.ShapeDtypeStruct((n, D_MODEL), F32),
        scratch_shapes=[pltpu.VMEM((padded, D_MODEL), F32),
                        pltpu.VMEM((padded, D_MODEL), BF16),
                        pltpu.VMEM((padded, FF_CHUNK), F32),
                        pltpu.VMEM((padded, FF_CHUNK), F32),
                        pltpu.VMEM((ROW_TILE, D_FF), BF16)],
        compiler_params=pltpu.CompilerParams(dimension_semantics=("parallel",),
                                             vmem_limit_bytes=VMEM_LIMIT),
        name="post_mixer",
    )(x2, x2, o_hg, o_hg, o_nsa, o_nsa, p2, w_out, gf, w_up, conv_w, conv_b, w_down, gg, w_gate, w_ple, gp)


HG_LEVELS = (32, 16, 8, 4, 2, 1)
HG_TB = 256


def _hgrn_level_matrices():
    c = HG_CHUNK
    r = np.arange(c)[:, None]
    u = np.arange(c)[None, :]
    rows = []
    for m in HG_LEVELS:
        r0 = (r // m) * m
        upper = (r & m) != 0
        rows.append(np.where(upper, (u >= r0) & (u <= r), (u >= r + 1) & (u <= r0 + m - 1)))
    rows.append(u <= r)
    rows.append(u > r)
    return np.concatenate(rows, 0).astype(np.float32)


def _hgrn_pair_masks():
    c = HG_CHUNK
    t = np.arange(c)[:, None]
    s = np.arange(c)[None, :]
    masks = [((t // (2 * m)) == (s // (2 * m))) & ((t & m) != 0) & ((s & m) == 0) for m in HG_LEVELS]
    masks.append(t == s)
    return np.stack(masks).astype(np.float32)


def _hgrn_kernel(q_ref, f_ref, v_ref, g_ref, lbl_ref, ng_ref, mall_ref, msk_ref, o_ref, st_ref,
                 *, layer, n_chunks):
    @pl.when(pl.program_id(1) == 0)
    def _():
        st_ref[...] = jnp.zeros_like(st_ref)

    lbl = lbl_ref[...]
    e = jnp.exp(lbl - jnp.max(lbl, axis=0, keepdims=True))
    sm = e / jnp.sum(e, axis=0, keepdims=True)
    lb = jnp.sum(sm[:layer + 1], axis=0, keepdims=True)
    mall = mall_ref[...]
    ng = ng_ref[...]
    c = HG_CHUNK
    nl = len(HG_LEVELS)

    intra, q_decayed, carry_decay, carry_add = {}, {}, {}, {}
    for ci in range(n_chunks):
        r0 = ci * c
        fg = lb + (1.0 - lb) * jax.nn.sigmoid(f_ref[r0:r0 + c, :].astype(F32))
        kk = 1.0 - fg
        ee = jnp.exp(_dot(mall, jnp.concatenate(_split(jnp.log(fg)), axis=0)))
        qf = _silu(q_ref[r0:r0 + c, :].astype(F32)) * HG_KDIM ** -0.5
        for h in range(HG_HEADS):
            sl = slice(h * HG_KDIM, (h + 1) * HG_KDIM)
            qh, kh = qf[:, sl], kk[:, sl]
            a = msk_ref[nl] * _dot_nt(qh.astype(BF16), kh.astype(BF16))
            for j in range(nl):
                ej = ee[j * c:(j + 1) * c, sl]
                a += msk_ref[j] * _dot_nt((qh * ej).astype(BF16), (kh * ej).astype(BF16))
            vh = v_ref[r0:r0 + c, sl]
            eb = ee[nl * c:(nl + 1) * c, sl]
            intra[ci, h] = _dot(a.astype(BF16), vh)
            q_decayed[ci, h] = (qh * eb).astype(BF16)
            carry_decay[ci, h] = eb[c - 1:c, :]
            carry_add[ci, h] = _dot_tn(vh, (kh * ee[(nl + 1) * c:(nl + 2) * c, sl]).astype(BF16))

    for h in range(HG_HEADS):
        sl = slice(h * HG_KDIM, (h + 1) * HG_KDIM)
        st = st_ref[h]
        states = []
        for ci in range(n_chunks):
            states.append(st.astype(BF16))
            st = carry_decay[ci, h] * st + carry_add[ci, h]
        st_ref[h] = st
        for ci in range(n_chunks):
            r0 = ci * c
            o = intra[ci, h] + _dot_nt(q_decayed[ci, h], states[ci])
            on = o * lax.rsqrt(jnp.mean(o * o, axis=-1, keepdims=True) + EPS) * ng
            o_ref[r0:r0 + c, sl] = (on * _silu(g_ref[r0:r0 + c, sl].astype(F32))).astype(o_ref.dtype)


def _hgrn(proj, lb_logits, norm_g, layer, batch, seq):
    nt = seq // HG_TB
    w = HG_HEADS * HG_KDIM
    col = lambda k: (lambda b, t: (b * nt + t, k))
    const2 = lambda b, t: (0, 0)
    return pl.pallas_call(
        functools.partial(_hgrn_kernel, layer=layer, n_chunks=HG_TB // HG_CHUNK),
        grid=(batch, nt),
        in_specs=[pl.BlockSpec((HG_TB, w), col(0)), pl.BlockSpec((HG_TB, w), col(1)),
                  pl.BlockSpec((HG_TB, w), col(2)), pl.BlockSpec((HG_TB, w), col(3)),
                  pl.BlockSpec(lb_logits.shape, const2),
                  pl.BlockSpec((1, HG_VDIM), const2),
                  pl.BlockSpec(((len(HG_LEVELS) + 2) * HG_CHUNK, 2 * HG_CHUNK), const2),
                  pl.BlockSpec((len(HG_LEVELS) + 1, HG_CHUNK, HG_CHUNK), lambda b, t: (0, 0, 0))],
        out_specs=pl.BlockSpec((HG_TB, w), lambda b, t: (b * nt + t, 0)),
        out_shape=jax.ShapeDtypeStruct((batch * seq, HG_WIDTH), BF16),
        scratch_shapes=[pltpu.VMEM((HG_HEADS, HG_VDIM, HG_KDIM), F32)],
        compiler_params=pltpu.CompilerParams(dimension_semantics=("parallel", "arbitrary"),
                                             vmem_limit_bytes=VMEM_LIMIT),
        name="hgrn2",
    )(proj, proj, proj, proj, lb_logits, norm_g,
      jnp.asarray(np.tile(_hgrn_level_matrices(), (1, 2)), BF16), jnp.asarray(_hgrn_pair_masks(), F32))


COL_NQ = 2048 // LANES
COL_KC, COL_VC, COL_KS, COL_VS, COL_KW, COL_VW, COL_GATE = (COL_NQ + 4 + k for k in range(7))
LOG2E = 1.4426950408889634
MASK_BIAS = -1e30
V_ROWS = HEAD_DIM + 2 * SUBLANES


def _rope_tables(pos):
    half = ROT_DIM // 2
    inv = np.float32(ROPE_THETA) ** (-np.arange(half, dtype=np.float32) / np.float32(half))
    ang = pos.astype(np.float32)[:, None] * inv[None, :]
    n = pos.shape[0]
    pad = HEAD_DIM - ROT_DIM
    cos = np.concatenate([np.cos(ang), np.cos(ang), np.ones((n, pad), np.float32)], axis=1)
    sin = np.concatenate([np.sin(ang), np.sin(ang), np.zeros((n, pad), np.float32)], axis=1)
    return cos.astype(np.float32), sin.astype(np.float32)


def _rot_matrix(heads):
    half = ROT_DIM // 2
    r = np.zeros((HEAD_DIM, HEAD_DIM), np.float32)
    for j in range(half):
        r[j + half, j] = -1.0
        r[j, j + half] = 1.0
    return np.kron(np.eye(heads, dtype=np.float32), r)


def _norm_rope_t(xt, gain, cos, sin):
    half = ROT_DIM // 2
    xn = xt * lax.rsqrt(jnp.mean(xt * xt, axis=0, keepdims=True) + EPS) * gain
    x1, x2 = xn[:half], xn[half:ROT_DIM]
    return jnp.concatenate([x1 * cos - x2 * sin, x2 * cos + x1 * sin, xn[ROT_DIM:]], axis=0)


def _in_proj_kernel(x_ref, g_ref, w_ref, cos_ref, sin_ref, gq_ref, gk_ref, perm_ref,
                    o_ref, qo_ref, kso_ref, vso_ref, kwo_ref, vwo_ref, xk_ref, xv_ref, *, tiles_per_seq):
    tb = x_ref.shape[0]
    xb = _rms(x_ref[...], g_ref[...]).astype(BF16)
    hg_cols = COL_NQ * LANES
    qp = _dot(xb, w_ref[:, hg_cols:hg_cols + NSA_WIDTH])
    o_ref[:, hg_cols:hg_cols + NSA_WIDTH] = qp.astype(o_ref.dtype)
    kv = _dot(xb, w_ref[:, COL_KC * LANES:])
    o_ref[:, COL_KC * LANES:] = kv.astype(o_ref.dtype)
    for c in range(0, hg_cols, hg_cols // 2):
        o_ref[:, c:c + hg_cols // 2] = _dot(xb, w_ref[:, c:c + hg_cols // 2]).astype(o_ref.dtype)
    part_t = lambda col: kv[:, (col - COL_KC) * LANES:(col - COL_KC + 1) * LANES].T

    groups = tb // CMP_STRIDE
    banded = _dot(perm_ref[...], kv[:, :2 * LANES].astype(BF16))
    bands = [banded[l * groups:(l + 1) * groups] for l in range(CMP_STRIDE)]
    xk_ref[0] = jnp.concatenate([r[:, :LANES] for r in bands], axis=1).astype(xk_ref.dtype)
    xv_ref[0] = jnp.concatenate([r[:, LANES:] for r in bands], axis=1).astype(xv_ref.dtype)

    cos, sin = cos_ref[...], sin_ref[...]
    qt = qp.T
    for h in range(NSA_HEADS):
        qo_ref[0, h] = _norm_rope_t(qt[h * HEAD_DIM:(h + 1) * HEAD_DIM], gq_ref[...], cos, sin).astype(qo_ref.dtype)

    t0 = (pl.program_id(0) % tiles_per_seq) * tb
    block = (t0 + lax.broadcasted_iota(jnp.int32, (HEAD_DIM, tb), 1)) // SEL_BLOCK
    row_i = lax.broadcasted_iota(jnp.int32, (HEAD_DIM, tb), 0)
    tails = (jnp.where(row_i == block, 1.0, 0.0), jnp.where(row_i == 0, 1.0, 0.0))
    row_ones = jnp.ones((V_ROWS - HEAD_DIM, tb), F32)
    for j, (kcol, vcol, ko_ref, vo_ref) in enumerate(((COL_KS, COL_VS, kso_ref, vso_ref),
                                                      (COL_KW, COL_VW, kwo_ref, vwo_ref))):
        kt, vt = part_t(kcol), part_t(vcol)
        for g in range(NSA_KV):
            kh = _norm_rope_t(kt[g * HEAD_DIM:(g + 1) * HEAD_DIM], gk_ref[j], cos, sin)
            ko_ref[0, g] = jnp.concatenate([kh, tails[j]], axis=0).T.astype(ko_ref.dtype)
        for g in range(NSA_KV):
            vo_ref[0, g] = jnp.concatenate([vt[g * HEAD_DIM:(g + 1) * HEAD_DIM], row_ones], axis=0).astype(vo_ref.dtype)


def _in_proj(x2, g, w_pad, q_norm_g, k_norm_g, batch, seq):
    n = x2.shape[0]
    nt = seq // ROW_TILE
    assert seq // SEL_BLOCK <= HEAD_DIM
    cos, sin = _rope_tables(np.arange(seq))
    half = ROT_DIM // 2
    cos_t, sin_t = jnp.asarray(cos[:, :half].T.copy()), jnp.asarray(sin[:, :half].T.copy())
    lanes = lambda gain: jnp.tile(gain[..., None], (1,) * gain.ndim + (ROW_TILE,))
    gq = lanes(q_norm_g * (HEAD_DIM ** -0.5 * LOG2E))
    gk = lanes(k_norm_g[1:3])
    row = np.arange(ROW_TILE)
    perm = (np.arange(ROW_TILE)[None, :] == ((row % (ROW_TILE // CMP_STRIDE)) * CMP_STRIDE
                                             + row // (ROW_TILE // CMP_STRIDE))[:, None]).astype(np.float32)
    const = lambda i: (0, 0)
    rows_major = lambda d: pl.BlockSpec((1, NSA_KV, ROW_TILE, d), lambda i: (i // nt, 0, i % nt, 0))
    cols_major = lambda heads, d: pl.BlockSpec((1, heads, d, ROW_TILE), lambda i: (i // nt, 0, 0, i % nt))
    sds = lambda *shape: jax.ShapeDtypeStruct((batch,) + shape, BF16)
    grouped = pl.BlockSpec((1, ROW_TILE // CMP_STRIDE, CMP_STRIDE * LANES), lambda i: (i // nt, i % nt, 0))
    return pl.pallas_call(
        functools.partial(_in_proj_kernel, tiles_per_seq=nt),
        grid=(n // ROW_TILE,),
        in_specs=[pl.BlockSpec((ROW_TILE, D_MODEL), lambda i: (i, 0)),
                  pl.BlockSpec((1, D_MODEL), const),
                  pl.BlockSpec((D_MODEL, IN_PAD), const),
                  pl.BlockSpec((half, ROW_TILE), lambda i: (0, i % nt)),
                  pl.BlockSpec((half, ROW_TILE), lambda i: (0, i % nt)),
                  pl.BlockSpec((HEAD_DIM, ROW_TILE), const),
                  pl.BlockSpec((2, HEAD_DIM, ROW_TILE), lambda i: (0, 0, 0)),
                  pl.BlockSpec((ROW_TILE, ROW_TILE), const)],
        out_specs=[pl.BlockSpec((ROW_TILE, IN_PAD), lambda i: (i, 0)),
                   cols_major(NSA_HEADS, HEAD_DIM), rows_major(2 * HEAD_DIM), cols_major(NSA_KV, V_ROWS),
                   rows_major(2 * HEAD_DIM), cols_major(NSA_KV, V_ROWS), grouped, grouped],
        out_shape=[jax.ShapeDtypeStruct((n, IN_PAD), BF16),
                   sds(NSA_HEADS, HEAD_DIM, seq), sds(NSA_KV, seq, 2 * HEAD_DIM), sds(NSA_KV, V_ROWS, seq),
                   sds(NSA_KV, seq, 2 * HEAD_DIM), sds(NSA_KV, V_ROWS, seq),
                   sds(seq // CMP_STRIDE, CMP_STRIDE * LANES), sds(seq // CMP_STRIDE, CMP_STRIDE * LANES)],
        compiler_params=pltpu.CompilerParams(dimension_semantics=("parallel",),
                                             vmem_limit_bytes=VMEM_LIMIT),
        name="in_proj",
    )(x2, g, w_pad, cos_t, sin_t, gq, gk, jnp.asarray(perm, BF16))


def _compress_kernel(xk_ref, xv_ref, pea_ref, peb_ref, w1a_ref, w1b_ref, w2_ref, gk_ref, cos_ref, sin_ref,
                     rot_ref, kc_ref, vc_ref, shift_scr):
    nc = xk_ref.shape[1]
    for j, x_ref in enumerate((xk_ref, xv_ref)):
        x = x_ref[0].astype(F32)
        u = _dot((x + pea_ref[j:j + 1, :]).astype(BF16), w1a_ref[j])
        v = _dot((x + peb_ref[j:j + 1, :]).astype(BF16), w1b_ref[j])
        shift_scr[0:nc, :] = v
        shift_scr[nc:nc + HALO, :] = jnp.zeros((HALO, v.shape[1]), F32)
        hid = _silu(u + shift_scr[1:nc + 1, :])
        out = _dot(hid.astype(BF16), w2_ref[j])
        if j == 0:
            for g in range(NSA_KV):
                og = (out if g == 0 else pltpu.roll(out, HEAD_DIM, 1))[:, :HEAD_DIM]
                xn = _rms(og, gk_ref[...])
                kc_ref[0, g] = (xn * cos_ref[...] + _dot(xn.astype(BF16), rot_ref[...]) * sin_ref[...]).astype(kc_ref.dtype)
        else:
            vc_ref[0] = out.T.reshape(NSA_KV, HEAD_DIM, nc).astype(vc_ref.dtype)


def _compress(xk, xv, cmp_pe, cmp_w1, cmp_w2, gk0, batch, seq):
    nc = seq // CMP_STRIDE
    hidden = cmp_w1.shape[2]
    cos, sin = (jnp.asarray(t) for t in _rope_tables(np.arange(nc) * CMP_STRIDE + CMP_LEN - 1))
    pe = jnp.tile(cmp_pe[:, :, None, :], (1, 1, NSA_KV, 1))
    pea = pe[:, :CMP_STRIDE].reshape(2, -1)
    peb = pe[:, CMP_STRIDE:].reshape(2, -1)
    eye = jnp.eye(NSA_KV, dtype=cmp_w1.dtype)
    w1 = jnp.einsum('jldh,gk->jlgdkh', cmp_w1.reshape(2, CMP_LEN, HEAD_DIM, hidden), eye)
    w1 = w1.reshape(2, CMP_LEN, NSA_KV * HEAD_DIM, NSA_KV * hidden).astype(BF16)
    w1a = w1[:, :CMP_STRIDE].reshape(2, -1, NSA_KV * hidden)
    w1b = w1[:, CMP_STRIDE:].reshape(2, -1, NSA_KV * hidden)
    w2 = jnp.einsum('jhd,gk->jghkd', cmp_w2, eye).reshape(2, NSA_KV * hidden, NSA_KV * HEAD_DIM).astype(BF16)
    blk = pl.BlockSpec((1, nc, CMP_STRIDE * LANES), lambda b: (b, 0, 0))
    const2 = lambda b: (0, 0)
    const3 = lambda b: (0, 0, 0)
    return pl.pallas_call(
        _compress_kernel,
        grid=(batch,),
        in_specs=[blk, blk,
                  pl.BlockSpec(pea.shape, const2), pl.BlockSpec(peb.shape, const2),
                  pl.BlockSpec(w1a.shape, const3), pl.BlockSpec(w1b.shape, const3), pl.BlockSpec(w2.shape, const3),
                  pl.BlockSpec((1, HEAD_DIM), const2),
                  pl.BlockSpec((nc, HEAD_DIM), const2), pl.BlockSpec((nc, HEAD_DIM), const2),
                  pl.BlockSpec((HEAD_DIM, HEAD_DIM), const2)],
        out_specs=[pl.BlockSpec((1, NSA_KV, nc, HEAD_DIM), lambda b: (b, 0, 0, 0)),
                   pl.BlockSpec((1, NSA_KV, HEAD_DIM, nc), lambda b: (b, 0, 0, 0))],
        out_shape=[jax.ShapeDtypeStruct((batch, NSA_KV, nc, HEAD_DIM), BF16),
                   jax.ShapeDtypeStruct((batch, NSA_KV, HEAD_DIM, nc), BF16)],
        scratch_shapes=[pltpu.VMEM((nc + HALO, NSA_KV * hidden), F32)],
        compiler_params=pltpu.CompilerParams(dimension_semantics=("parallel",),
                                             vmem_limit_bytes=VMEM_LIMIT),
        name="nsa_compress",
    )(xk, xv, pea, peb, w1a, w1b, w2, gk0[None, :], cos, sin, jnp.asarray(_rot_matrix(1), BF16))


NSA_TQ = 256


def _overlap_t(seq):
    ns, nc = seq // SEL_BLOCK, seq // CMP_STRIDE
    cs = np.arange(nc)[None, :] * CMP_STRIDE
    ss = np.arange(ns)[:, None] * SEL_BLOCK
    ov = np.clip(np.minimum(cs + CMP_LEN, ss + SEL_BLOCK) - np.maximum(cs, ss), 0, None)
    return (ov / CMP_LEN).astype(np.float32)


def _cmpsel_kernel(offset_ref, q_ref, kc_ref, vc_ref, ovt_ref, ocmp_ref, bias_ref, *, top):
    tq, nc = q_ref.shape[3], kc_ref.shape[2]
    ns = ovt_ref.shape[0]
    t0 = pl.program_id(1) * tq
    n_i = lax.broadcasted_iota(jnp.int32, (ns, tq), 0)
    t_i = t0 + lax.broadcasted_iota(jnp.int32, (ns, tq), 1)
    cur = t_i // SEL_BLOCK
    forced = jnp.where(n_i == 0, FORCE, jnp.where(n_i == cur, FORCE, jnp.where(n_i == cur - 1, FORCE, 0.0)))
    visible = n_i * SEL_BLOCK <= t_i
    sub = lax.broadcasted_iota(jnp.int32, (SUBLANES, tq), 0)

    def attend(g, quarters):
        ncv, nsv = quarters * nc // 4, quarters * ns // 4
        c_i = lax.broadcasted_iota(jnp.int32, (ncv, NSA_GROUP * tq), 0)
        q_i = lax.broadcasted_iota(jnp.int32, (ncv, NSA_GROUP * tq), 1) & (tq - 1)
        cmask = c_i * CMP_STRIDE + (CMP_LEN - 1) <= t0 + q_i
        qg = jnp.concatenate([q_ref[0, g * NSA_GROUP + h] for h in range(NSA_GROUP)], axis=1)
        s = jnp.where(cmask, _dot(kc_ref[0, g, 0:ncv, :], qg), NEG)
        e = jnp.exp2(s - jnp.max(s, axis=0, keepdims=True))
        p = jnp.where(cmask, e * (1.0 / jnp.sum(e, axis=0, keepdims=True)), 0.0)
        o = _dot(vc_ref[0, g, :, 0:ncv], p.astype(BF16)).astype(ocmp_ref.dtype)
        psum = p[:, :tq]
        for h in range(NSA_GROUP):
            ocmp_ref[0, g * NSA_GROUP + h] = o[:, h * tq:(h + 1) * tq]
            if h:
                psum += p[:, h * tq:(h + 1) * tq]
        ovt = ovt_ref[:, 0:ncv]
        hi, lo = _split(psum)
        val = jnp.where(visible, _dot(ovt, hi) + _dot(ovt, lo) + forced, NEG)
        groups = []
        for r in range(ns // SUBLANES):
            if r * SUBLANES >= nsv:
                groups.append(jnp.zeros((SUBLANES, tq), F32))
                continue
            v_r = val[r * SUBLANES:(r + 1) * SUBLANES]
            c = jnp.zeros((SUBLANES, tq), F32)
            for m in range(nsv):
                row = val[m:m + 1, :]
                if m // SUBLANES > r:
                    hit = row > v_r
                elif m // SUBLANES < r:
                    hit = row >= v_r
                else:
                    hit = jnp.where(sub > m % SUBLANES, jnp.where(row >= v_r, 1.0, 0.0),
                                    jnp.where(row > v_r, 1.0, 0.0)) > 0.5
                c = jnp.where(hit, c + 1.0, c)
            groups.append(c)
        bias = jnp.where(jnp.concatenate(groups, axis=0) < top, offset_ref[0], MASK_BIAS)
        if ns < HEAD_DIM:
            bias = jnp.concatenate([bias, jnp.zeros((HEAD_DIM - ns, tq), F32)], axis=0)
        bias_ref[0, g] = bias.astype(bias_ref.dtype)

    assert ns % (4 * SUBLANES) == 0 and nc % 4 == 0
    last_quarter = (t0 + tq - 1) // (nc // 4 * CMP_STRIDE)
    for g in range(NSA_KV):
        for v in range(4):
            @pl.when(last_quarter == v)
            def _():
                attend(g, v + 1)


def _cmpsel(offset, qt, kc, vct, batch, seq):
    nt = seq // NSA_TQ
    nc, ns = seq // CMP_STRIDE, seq // SEL_BLOCK
    qblk = pl.BlockSpec((1, NSA_HEADS, HEAD_DIM, NSA_TQ), lambda b, t: (b, 0, 0, t))
    return pl.pallas_call(
        functools.partial(_cmpsel_kernel, top=min(SEL_TOPK, ns)),
        grid=(batch, nt),
        in_specs=[pl.BlockSpec(memory_space=pltpu.SMEM), qblk,
                  pl.BlockSpec((1, NSA_KV, nc, HEAD_DIM), lambda b, t: (b, 0, 0, 0)),
                  pl.BlockSpec((1, NSA_KV, HEAD_DIM, nc), lambda b, t: (b, 0, 0, 0)),
                  pl.BlockSpec((ns, nc), lambda b, t: (0, 0))],
        out_specs=[qblk, pl.BlockSpec((1, NSA_KV, HEAD_DIM, NSA_TQ), lambda b, t: (b, 0, 0, t))],
        out_shape=[jax.ShapeDtypeStruct((batch, NSA_HEADS, HEAD_DIM, seq), BF16),
                   jax.ShapeDtypeStruct((batch, NSA_KV, HEAD_DIM, seq), BF16)],
        compiler_params=pltpu.CompilerParams(dimension_semantics=("parallel", "parallel"),
                                             vmem_limit_bytes=VMEM_LIMIT),
        name="nsa_cmpsel",
    )(offset, qt, kc, vct, jnp.asarray(_overlap_t(seq), BF16))


SAFE_OFFSET = 60.0


def _attn_kernel(q_ref, ks_ref, vs_ref, kw_ref, vw_ref, bias_ref, wrow_ref, ocmp_ref, gate_ref, gout_ref,
                 o_ref, acc_scr, *scratch, bounded):
    tq = q_ref.shape[3]
    i = pl.program_id(1)
    t0 = i * tq
    q_sel, q_win = [], []
    for g in range(NSA_KV):
        q_grp = jnp.concatenate([q_ref[0, g * NSA_GROUP + h] for h in range(NSA_GROUP)], axis=1)
        q_sel.append(jnp.concatenate([q_grp, jnp.concatenate([bias_ref[0, g]] * NSA_GROUP, axis=1)], axis=0))
        q_win.append(jnp.concatenate([q_grp, wrow_ref[...]], axis=0))

    def per_head(s, keep):
        return jnp.concatenate([jnp.where(keep, s[:, h * tq:(h + 1) * tq], NEG) for h in range(NSA_GROUP)], axis=1)

    def causal(s):
        key_i = lax.broadcasted_iota(jnp.int32, (tq, tq), 0)
        return per_head(s, key_i <= lax.broadcasted_iota(jnp.int32, (tq, tq), 1))

    def tile(j):
        return pl.ds(pl.multiple_of(j * tq, tq), tq)

    acc_scr[...] = jnp.zeros_like(acc_scr)
    span = WINDOW + tq
    w0 = pl.multiple_of(jnp.maximum(t0 + tq - span, 0), tq)
    dist = (t0 - w0) + (lax.broadcasted_iota(jnp.int32, (span, tq), 1)
                        - lax.broadcasted_iota(jnp.int32, (span, tq), 0))
    in_window = jnp.abs(2 * dist - (WINDOW - 1)) < WINDOW
    win_scores = lambda g: per_head(_dot(kw_ref[0, g, pl.ds(w0, span), :], q_win[g]), in_window)

    if bounded:
        p0_scr, p1_scr = scratch

        def probs(j, p_scr, diag):
            for g in range(NSA_KV):
                s = _dot(ks_ref[0, g, tile(j), :], q_sel[g])
                p_scr[g] = jnp.exp2(causal(s) if diag else s).astype(BF16)

        def accumulate(j, p_scr):
            for g in range(NSA_KV):
                acc_scr[g] += _dot(vs_ref[0, g, :, tile(j)], p_scr[g])

        @pl.when(i > 0)
        def _():
            probs(0, p0_scr, False)

        def body(jj, carry):
            probs(2 * jj + 1, p1_scr, False)
            accumulate(2 * jj, p0_scr)
            probs(2 * jj + 2, p0_scr, False)
            accumulate(2 * jj + 1, p1_scr)
            return carry

        lax.fori_loop(0, (i - 1) // 2, body, 0)

        @pl.when((i > 0) & (i % 2 == 1))
        def _():
            accumulate(i - 1, p0_scr)

        @pl.when((i > 0) & (i % 2 == 0))
        def _():
            probs(i - 1, p1_scr, False)
            accumulate(i - 2, p0_scr)
            accumulate(i - 1, p1_scr)

        probs(i, p0_scr, True)
        s_win = [win_scores(g) for g in range(NSA_KV)]
        accumulate(i, p0_scr)
        win = [_dot(vw_ref[0, g, :, pl.ds(w0, span)], jnp.exp2(s_win[g]).astype(BF16)) for g in range(NSA_KV)]
    else:
        m_scr, s0_scr, s1_scr = scratch
        m_scr[...] = jnp.full(m_scr.shape, NEG, F32)

        def scores(j, s_scr):
            for g in range(NSA_KV):
                s_scr[g] = _dot(ks_ref[0, g, tile(j), :], q_sel[g])

        def consume(j, s_scr, diag):
            for g in range(NSA_KV):
                s = causal(s_scr[g]) if diag else s_scr[g]
                m_old = m_scr[g]
                m_new = jnp.maximum(m_old, jnp.max(s, axis=0, keepdims=True))
                p = jnp.exp2(s - m_new).astype(BF16)
                acc_scr[g] = jnp.exp2(m_old - m_new) * acc_scr[g] + _dot(vs_ref[0, g, :, tile(j)], p)
                m_scr[g] = m_new

        scores(0, s0_scr)

        def body(jj, carry):
            scores(2 * jj + 1, s1_scr)
            consume(2 * jj, s0_scr, False)
            scores(2 * jj + 2, s0_scr)
            consume(2 * jj + 1, s1_scr, False)
            return carry

        lax.fori_loop(0, i // 2, body, 0)

        @pl.when(i % 2 == 0)
        def _():
            consume(i, s0_scr, True)

        @pl.when(i % 2 == 1)
        def _():
            scores(i, s1_scr)
            consume(i - 1, s0_scr, False)
            consume(i, s1_scr, True)

        win = []
        for g in range(NSA_KV):
            s = win_scores(g)
            p = jnp.exp2(s - jnp.max(s, axis=0, keepdims=True)).astype(BF16)
            win.append(_dot(vw_ref[0, g, :, pl.ds(w0, span)], p))

    gate = jax.nn.sigmoid(gate_ref[...].astype(F32).T)
    gout = gout_ref[...]
    rows = []
    for hh in range(NSA_HEADS):
        g, lanes = hh // NSA_GROUP, slice((hh % NSA_GROUP) * tq, (hh % NSA_GROUP + 1) * tq)
        a_s, a_w = acc_scr[g][:, lanes], win[g][:, lanes]
        o = (gate[hh:hh + 1] * ocmp_ref[0, hh].astype(F32)
             + gate[NSA_HEADS + hh:NSA_HEADS + hh + 1] * (a_s[:HEAD_DIM] * (1.0 / a_s[HEAD_DIM:HEAD_DIM + 1]))
             + gate[2 * NSA_HEADS + hh:2 * NSA_HEADS + hh + 1] * (a_w[:HEAD_DIM] * (1.0 / a_w[HEAD_DIM:HEAD_DIM + 1])))
        rows.append(o * lax.rsqrt(jnp.mean(o * o, axis=0, keepdims=True) + EPS) * gout)
    o_ref[...] = jnp.concatenate(rows, axis=0).T.astype(o_ref.dtype)


def _attn(qt, ks, vst, kw, vwt, bias, wrow, ocmp, proj, gout, *, bounded):
    batch, _, _, seq = qt.shape
    nt = seq // NSA_TQ
    assert WINDOW % NSA_TQ == 0 and seq >= WINDOW + NSA_TQ
    wide = NSA_GROUP * NSA_TQ
    rows_full = pl.BlockSpec((1, NSA_KV, seq, 2 * HEAD_DIM), lambda b, t: (b, 0, 0, 0))
    cols_full = pl.BlockSpec((1, NSA_KV, V_ROWS, seq), lambda b, t: (b, 0, 0, 0))
    qtile = lambda heads: pl.BlockSpec((1, heads, HEAD_DIM, NSA_TQ), lambda b, t: (b, 0, 0, t))
    if bounded:
        scratch = [pltpu.VMEM((NSA_KV, NSA_TQ, wide), BF16), pltpu.VMEM((NSA_KV, NSA_TQ, wide), BF16)]
    else:
        scratch = [pltpu.VMEM((NSA_KV, 1, wide), F32),
                   pltpu.VMEM((NSA_KV, NSA_TQ, wide), F32), pltpu.VMEM((NSA_KV, NSA_TQ, wide), F32)]
    return pl.pallas_call(
        functools.partial(_attn_kernel, bounded=bounded),
        grid=(batch, nt),
        in_specs=[qtile(NSA_HEADS), rows_full, cols_full, rows_full, cols_full, qtile(NSA_KV),
                  pl.BlockSpec((HEAD_DIM, wide), lambda b, t: (0, 0)),
                  qtile(NSA_HEADS),
                  pl.BlockSpec((NSA_TQ, LANES), lambda b, t: (b * nt + t, COL_GATE)),
                  pl.BlockSpec((HEAD_DIM, NSA_TQ), lambda b, t: (0, 0))],
        out_specs=pl.BlockSpec((NSA_TQ, NSA_WIDTH), lambda b, t: (b * nt + t, 0)),
        out_shape=jax.ShapeDtypeStruct((batch * seq, NSA_WIDTH), BF16),
        scratch_shapes=[pltpu.VMEM((NSA_KV, V_ROWS, wide), F32)] + scratch,
        compiler_params=pltpu.CompilerParams(dimension_semantics=("parallel", "parallel"),
                                             vmem_limit_bytes=VMEM_LIMIT),
        name="nsa_attn" if bounded else "nsa_attn_online",
    )(qt, ks, vst, kw, vwt, bias, wrow, ocmp, proj, gout)


def _nsa(proj, qt, ks, vst, kw, vwt, xk, xv, q_norm_g, k_norm_g, cmp_pe, cmp_w1, cmp_w2, out_norm_g, batch, seq):
    def offset(gk):
        bound = HEAD_DIM * (HEAD_DIM ** -0.5 * LOG2E) * jnp.max(jnp.abs(q_norm_g)) * jnp.max(jnp.abs(gk))
        return (1.02 * bound).astype(BF16).astype(F32)

    off_sel, off_win = offset(k_norm_g[1]), offset(k_norm_g[2])
    kc, vct = _compress(xk, xv, cmp_pe, cmp_w1, cmp_w2, k_norm_g[0], batch, seq)
    ocmp, bias = _cmpsel(-off_sel[None], qt, kc, vct, batch, seq)
    wrow = jnp.zeros((HEAD_DIM, NSA_GROUP * NSA_TQ), F32).at[0].set(-off_win).astype(BF16)
    gout = jnp.tile(out_norm_g[:, None], (1, NSA_TQ))
    return lax.cond(jnp.maximum(off_sel, off_win) <= SAFE_OFFSET,
                    functools.partial(_attn, bounded=True), functools.partial(_attn, bounded=False),
                    qt, ks, vst, kw, vwt, bias, wrow, ocmp, proj, gout)


def kernel(x, p, attn_norm_g, w_in, hg_lb_logits, hg_norm_g, nsa_q_norm_g, nsa_k_norm_g, cmp_pe,
           cmp_w1, cmp_w2, nsa_out_norm_g, w_out, ffn_norm_g, w_up, conv_w, conv_b, w_down,
           ple_gate_norm_g, w_ple_gate, w_ple, ple_norm_g):
    B, T, _ = x.shape
    n = B * T
    h = x.reshape(n, D_MODEL)
    for i in range(w_in.shape[0]):
        w_pad = jnp.pad(w_in[i], ((0, 0), (0, IN_PAD - IN_TOTAL))).astype(BF16)
        proj, *qkv = _in_proj(h, attn_norm_g[i][None, :], w_pad, nsa_q_norm_g[i], nsa_k_norm_g[i], B, T)
        o_hg = _hgrn(proj, hg_lb_logits.astype(F32), hg_norm_g[i][None, :], i, B, T)
        o_nsa = _nsa(proj, *qkv, nsa_q_norm_g[i], nsa_k_norm_g[i], cmp_pe[i], cmp_w1[i], cmp_w2[i],
                     nsa_out_norm_g[i], B, T)
        h = _post_mixer(h, o_hg, o_nsa, p[i].reshape(n, PLE_DIM), w_out[i].astype(BF16), ffn_norm_g[i][None, :],
                        w_up[i].astype(BF16), conv_w[i], conv_b[i][None, :], w_down[i].astype(BF16),
                        ple_gate_norm_g[i][None, :], w_ple_gate[i].astype(BF16), w_ple[i].astype(BF16),
                        ple_norm_g[i][None, :], T)
    return h.reshape(B, T, D_MODEL)
```

```python
import functools

import numpy as np
import jax
import jax.numpy as jnp
from jax import lax
from jax.experimental import pallas as pl
from jax.experimental.pallas import tpu as pltpu

F32 = jnp.float32
BF16 = jnp.bfloat16

D_MODEL = 1024
PLE_DIM = 256
EPS = 1e-6
NEG = -1e30
FORCE = 1e6

HG_HEADS = 4
HG_KDIM = 128
HG_VDIM = 128
HG_WIDTH = HG_HEADS * HG_VDIM
HG_CHUNK = 64

NSA_HEADS = 8
NSA_KV = 2
NSA_GROUP = NSA_HEADS // NSA_KV
HEAD_DIM = 64
NSA_WIDTH = NSA_HEADS * HEAD_DIM
KV_WIDTH = NSA_KV * HEAD_DIM
CMP_LEN = 32
CMP_STRIDE = 16
SEL_BLOCK = 64
SEL_TOPK = 16
WINDOW = 512
ROT_DIM = HEAD_DIM // 4
ROPE_THETA = 500000.0
D_FF = 2816
CONV_W = 3

IN_SIZES = (HG_HEADS * HG_KDIM, HG_HEADS * HG_KDIM, HG_WIDTH, HG_WIDTH, NSA_WIDTH,
            KV_WIDTH, KV_WIDTH, KV_WIDTH, KV_WIDTH, KV_WIDTH, KV_WIDTH, 3 * NSA_HEADS)
IN_TOTAL = sum(IN_SIZES)
LANES = 128
IN_PAD = -(-IN_TOTAL // LANES) * LANES
VMEM_LIMIT = 56 * 1024 * 1024

ROW_TILE = 512
FF_CHUNK = 256
SUBLANES = 8
HALO = SUBLANES
FFN_HALO = 2 * SUBLANES


def _rms(x, g):
    return x * lax.rsqrt(jnp.mean(x * x, axis=-1, keepdims=True) + EPS) * g


def _dot(a, b):
    return jnp.dot(a, b, preferred_element_type=F32)


def _dot_nt(a, b):
    return lax.dot_general(a, b, (((1,), (1,)), ((), ())), preferred_element_type=F32)


def _dot_tn(a, b):
    return lax.dot_general(a, b, (((0,), (0,)), ((), ())), preferred_element_type=F32)


def _split(x):
    hi = x.astype(BF16)
    return hi, (x - hi.astype(F32)).astype(BF16)


def _silu(x):
    hx = 0.5 * x
    return hx + hx * jnp.tanh(hx)


def _post_mixer_kernel(x_ref, xh_ref, a_ref, ah_ref, b_ref, bh_ref, p_ref, wo_ref, gf_ref, wup_ref, cw_ref,
                       cb_ref, wdn_ref, gg_ref, wg_ref, wp_ref, gp_ref, o_ref,
                       h1_scr, hn_scr, ug_scr, uu_scr, act_scr, *, tiles_per_seq):
    rows = x_ref.shape[0]
    halo, body = slice(0, FFN_HALO), slice(FFN_HALO, FFN_HALO + rows)
    hn_scr[halo, :HG_WIDTH], hn_scr[halo, HG_WIDTH:] = ah_ref[...], bh_ref[...]
    hn_scr[body, :HG_WIDTH], hn_scr[body, HG_WIDTH:] = a_ref[...], b_ref[...]
    mixed = _dot(hn_scr[...], wo_ref[...])
    h1_scr[halo, :] = xh_ref[...] + mixed[halo]
    h1_scr[body, :] = x_ref[...] + mixed[body]
    first = (pl.program_id(0) % tiles_per_seq) == 0
    hn_scr[halo, :] = jnp.where(first, 0.0, _rms(h1_scr[halo, :], gf_ref[...])).astype(BF16)
    hn_scr[body, :] = _rms(h1_scr[body, :], gf_ref[...]).astype(BF16)
    for c in range(0, D_FF, FF_CHUNK):
        conv = []
        for scr, off in ((ug_scr, c), (uu_scr, D_FF + c)):
            scr[...] = _dot(hn_scr[...], wup_ref[:, off:off + FF_CHUNK])
            cw = cw_ref[:, off:off + FF_CHUNK]
            conv.append(scr[FFN_HALO - 2:FFN_HALO - 2 + rows, :] * cw[0:1, :]
                        + scr[FFN_HALO - 1:FFN_HALO - 1 + rows, :] * cw[1:2, :]
                        + scr[body, :] * cw[2:3, :]
                        + cb_ref[:, off:off + FF_CHUNK])
        gate, up = conv
        act_scr[:, c:c + FF_CHUNK] = (_silu(gate) * up).astype(BF16)
    o_ref[...] = h1_scr[body, :] + _dot(act_scr[...], wdn_ref[...])
    hn_scr[body, :] = _rms(o_ref[...], gg_ref[...]).astype(BF16)
    e = _rms(_dot(p_ref[...].astype(BF16), wp_ref[...]), gp_ref[...])
    gate = jax.nn.sigmoid(_dot(hn_scr[body, :], wg_ref[...]))
    o_ref[...] = o_ref[...] + gate * e


def _post_mixer(x2, o_hg, o_nsa, p2, w_out, gf, w_up, conv_w, conv_b, w_down, gg, w_gate, w_ple, gp, seq):
    n = x2.shape[0]
    assert HG_WIDTH + NSA_WIDTH == D_MODEL
    const = lambda i: (0, 0)
    tile = lambda width: pl.BlockSpec((ROW_TILE, width), lambda i: (i, 0))
    halo_blocks = ROW_TILE // FFN_HALO
    halo = lambda width: pl.BlockSpec((FFN_HALO, width), lambda i: (jnp.maximum(i * halo_blocks - 1, 0), 0))
    weight = lambda r, c: pl.BlockSpec((r, c), const, pipeline_mode=pl.Buffered(1))
    padded = FFN_HALO + ROW_TILE
    return pl.pallas_call(
        functools.partial(_post_mixer_kernel, tiles_per_seq=seq // ROW_TILE),
        grid=(n // ROW_TILE,),
        in_specs=[tile(D_MODEL), halo(D_MODEL), tile(HG_WIDTH), halo(HG_WIDTH), tile(NSA_WIDTH), halo(NSA_WIDTH),
                  tile(PLE_DIM),
                  weight(D_MODEL, D_MODEL),
                  pl.BlockSpec((1, D_MODEL), const),
                  weight(D_MODEL, 2 * D_FF),
                  pl.BlockSpec((CONV_W, 2 * D_FF), const),
                  pl.BlockSpec((1, 2 * D_FF), const),
                  weight(D_FF, D_MODEL),
                  pl.BlockSpec((1, D_MODEL), const),
                  weight(D_MODEL, D_MODEL),
                  weight(PLE_DIM, D_MODEL),
                  pl.BlockSpec((1, D_MODEL), const)],
        out_specs=tile(D_MODEL),
        out_shape=jax.ShapeDtypeStruct((n, D_MODEL), F32),
        scratch_shapes=[pltpu.VMEM((padded, D_MODEL), F32),
                        pltpu.VMEM((padded, D_MODEL), BF16),
                        pltpu.VMEM((padded, FF_CHUNK), F32),
                        pltpu.VMEM((padded, FF_CHUNK), F32),
                        pltpu.VMEM((ROW_TILE, D_FF), BF16)],
        compiler_params=pltpu.CompilerParams(dimension_semantics=("parallel",),
                                             vmem_limit_bytes=VMEM_LIMIT),
        name="post_mixer",
    )(x2, x2, o_hg, o_hg, o_nsa, o_nsa, p2, w_out, gf, w_up, conv_w, conv_b, w_down, gg, w_gate, w_ple, gp)


HG_LEVELS = (32, 16, 8, 4, 2, 1)
HG_TB = 512


def _hgrn_level_matrices():
    c = HG_CHUNK
    r = np.arange(c)[:, None]
    u = np.arange(c)[None, :]
    rows = []
    for m in HG_LEVELS:
        r0 = (r // m) * m
        upper = (r & m) != 0
        rows.append(np.where(upper, (u >= r0) & (u <= r), (u >= r + 1) & (u <= r0 + m - 1)))
    rows.append(u <= r)
    rows.append(u > r)
    return np.concatenate(rows, 0).astype(np.float32)


def _hgrn_pair_masks():
    c = HG_CHUNK
    t = np.arange(c)[:, None]
    s = np.arange(c)[None, :]
    masks = [((t // (2 * m)) == (s // (2 * m))) & ((t & m) != 0) & ((s & m) == 0) for m in HG_LEVELS]
    masks.append(t == s)
    return np.stack(masks).astype(np.float32)


def _hgrn_kernel(q_ref, f_ref, v_ref, g_ref, lbl_ref, ng_ref, mall_ref, msk_ref, o_ref, st_ref,
                 *, layer, n_chunks):
    @pl.when(pl.program_id(1) == 0)
    def _():
        st_ref[...] = jnp.zeros_like(st_ref)

    lbl = lbl_ref[...]
    e = jnp.exp(lbl - jnp.max(lbl, axis=0, keepdims=True))
    sm = e / jnp.sum(e, axis=0, keepdims=True)
    lb = jnp.sum(sm[:layer + 1], axis=0, keepdims=True)
    mall = mall_ref[...]
    ng = ng_ref[...]
    c = HG_CHUNK
    nl = len(HG_LEVELS)

    intra, q_decayed, carry_decay, carry_add = {}, {}, {}, {}
    for ci in range(n_chunks):
        r0 = ci * c
        fg = lb + (1.0 - lb) * jax.nn.sigmoid(f_ref[r0:r0 + c, :].astype(F32))
        kk = 1.0 - fg
        ee = jnp.exp(_dot(mall, jnp.concatenate(_split(jnp.log(fg)), axis=0)))
        qf = _silu(q_ref[r0:r0 + c, :].astype(F32)) * HG_KDIM ** -0.5
        for h in range(HG_HEADS):
            sl = slice(h * HG_KDIM, (h + 1) * HG_KDIM)
            qh, kh = qf[:, sl], kk[:, sl]
            a = msk_ref[nl] * _dot_nt(qh.astype(BF16), kh.astype(BF16))
            for j in range(nl):
                ej = ee[j * c:(j + 1) * c, sl]
                a += msk_ref[j] * _dot_nt((qh * ej).astype(BF16), (kh * ej).astype(BF16))
            vh = v_ref[r0:r0 + c, sl]
            eb = ee[nl * c:(nl + 1) * c, sl]
            intra[ci, h] = _dot(a.astype(BF16), vh)
            q_decayed[ci, h] = (qh * eb).astype(BF16)
            carry_decay[ci, h] = eb[c - 1:c, :]
            carry_add[ci, h] = _dot_tn(vh, (kh * ee[(nl + 1) * c:(nl + 2) * c, sl]).astype(BF16))

    for h in range(HG_HEADS):
        sl = slice(h * HG_KDIM, (h + 1) * HG_KDIM)
        st = st_ref[h]
        states = []
        for ci in range(n_chunks):
            states.append(st.astype(BF16))
            st = carry_decay[ci, h] * st + carry_add[ci, h]
        st_ref[h] = st
        for ci in range(n_chunks):
            r0 = ci * c
            o = intra[ci, h] + _dot_nt(q_decayed[ci, h], states[ci])
            on = o * lax.rsqrt(jnp.mean(o * o, axis=-1, keepdims=True) + EPS) * ng
            o_ref[r0:r0 + c, sl] = (on * _silu(g_ref[r0:r0 + c, sl].astype(F32))).astype(o_ref.dtype)


def _hgrn(proj, lb_logits, norm_g, layer, batch, seq):
    nt = seq // HG_TB
    w = HG_HEADS * HG_KDIM
    col = lambda k: (lambda b, t: (b * nt + t, k))
    const2 = lambda b, t: (0, 0)
    return pl.pallas_call(
        functools.partial(_hgrn_kernel, layer=layer, n_chunks=HG_TB // HG_CHUNK),
        grid=(batch, nt),
        in_specs=[pl.BlockSpec((HG_TB, w), col(0)), pl.BlockSpec((HG_TB, w), col(1)),
                  pl.BlockSpec((HG_TB, w), col(2)), pl.BlockSpec((HG_TB, w), col(3)),
                  pl.BlockSpec(lb_logits.shape, const2),
                  pl.BlockSpec((1, HG_VDIM), const2),
                  pl.BlockSpec(((len(HG_LEVELS) + 2) * HG_CHUNK, 2 * HG_CHUNK), const2),
                  pl.BlockSpec((len(HG_LEVELS) + 1, HG_CHUNK, HG_CHUNK), lambda b, t: (0, 0, 0))],
        out_specs=pl.BlockSpec((HG_TB, w), lambda b, t: (b * nt + t, 0)),
        out_shape=jax.ShapeDtypeStruct((batch * seq, HG_WIDTH), BF16),
        scratch_shapes=[pltpu.VMEM((HG_HEADS, HG_VDIM, HG_KDIM), F32)],
        compiler_params=pltpu.CompilerParams(dimension_semantics=("parallel", "arbitrary"),
                                             vmem_limit_bytes=VMEM_LIMIT),
        name="hgrn2",
    )(proj, proj, proj, proj, lb_logits, norm_g,
      jnp.asarray(np.tile(_hgrn_level_matrices(), (1, 2)), BF16), jnp.asarray(_hgrn_pair_masks(), F32))


COL_NQ = 2048 // LANES
COL_KC, COL_VC, COL_KS, COL_VS, COL_KW, COL_VW, COL_GATE = (COL_NQ + 4 + k for k in range(7))
LOG2E = 1.4426950408889634
MASK_BIAS = -1e30
V_ROWS = HEAD_DIM + 2 * SUBLANES


def _rope_tables(pos):
    half = ROT_DIM // 2
    inv = np.float32(ROPE_THETA) ** (-np.arange(half, dtype=np.float32) / np.float32(half))
    ang = pos.astype(np.float32)[:, None] * inv[None, :]
    n = pos.shape[0]
    pad = HEAD_DIM - ROT_DIM
    cos = np.concatenate([np.cos(ang), np.cos(ang), np.ones((n, pad), np.float32)], axis=1)
    sin = np.concatenate([np.sin(ang), np.sin(ang), np.zeros((n, pad), np.float32)], axis=1)
    return cos.astype(np.float32), sin.astype(np.float32)


def _rot_matrix(heads):
    half = ROT_DIM // 2
    r = np.zeros((HEAD_DIM, HEAD_DIM), np.float32)
    for j in range(half):
        r[j + half, j] = -1.0
        r[j, j + half] = 1.0
    return np.kron(np.eye(heads, dtype=np.float32), r)


def _norm_rope_t(xt, gain, cos, sin):
    half = ROT_DIM // 2
    xn = xt * lax.rsqrt(jnp.mean(xt * xt, axis=0, keepdims=True) + EPS) * gain
    x1, x2 = xn[:half], xn[half:ROT_DIM]
    return jnp.concatenate([x1 * cos - x2 * sin, x2 * cos + x1 * sin, xn[ROT_DIM:]], axis=0)


def _in_proj_kernel(x_ref, g_ref, w_ref, cos_ref, sin_ref, gq_ref, gk_ref, perm_ref,
                    o_ref, qo_ref, kso_ref, vso_ref, kwo_ref, vwo_ref, xk_ref, xv_ref, *, tiles_per_seq):
    tb = x_ref.shape[0]
    xb = _rms(x_ref[...], g_ref[...]).astype(BF16)
    hg_cols = COL_NQ * LANES
    qp = _dot(xb, w_ref[:, hg_cols:hg_cols + NSA_WIDTH])
    o_ref[:, hg_cols:hg_cols + NSA_WIDTH] = qp.astype(o_ref.dtype)
    kv = _dot(xb, w_ref[:, COL_KC * LANES:])
    o_ref[:, COL_KC * LANES:] = kv.astype(o_ref.dtype)
    for c in range(0, hg_cols, hg_cols // 2):
        o_ref[:, c:c + hg_cols // 2] = _dot(xb, w_ref[:, c:c + hg_cols // 2]).astype(o_ref.dtype)
    part_t = lambda col: kv[:, (col - COL_KC) * LANES:(col - COL_KC + 1) * LANES].T

    groups = tb // CMP_STRIDE
    banded = _dot(perm_ref[...], kv[:, :2 * LANES].astype(BF16))
    bands = [banded[l * groups:(l + 1) * groups] for l in range(CMP_STRIDE)]
    xk_ref[0] = jnp.concatenate([r[:, :LANES] for r in bands], axis=1).astype(xk_ref.dtype)
    xv_ref[0] = jnp.concatenate([r[:, LANES:] for r in bands], axis=1).astype(xv_ref.dtype)

    cos, sin = cos_ref[...], sin_ref[...]
    qt = qp.T
    for h in range(NSA_HEADS):
        qo_ref[0, h] = _norm_rope_t(qt[h * HEAD_DIM:(h + 1) * HEAD_DIM], gq_ref[...], cos, sin).astype(qo_ref.dtype)

    t0 = (pl.program_id(0) % tiles_per_seq) * tb
    block = (t0 + lax.broadcasted_iota(jnp.int32, (HEAD_DIM, tb), 1)) // SEL_BLOCK
    row_i = lax.broadcasted_iota(jnp.int32, (HEAD_DIM, tb), 0)
    tails = (jnp.where(row_i == block, 1.0, 0.0), jnp.where(row_i == 0, 1.0, 0.0))
    row_ones = jnp.ones((V_ROWS - HEAD_DIM, tb), F32)
    for j, (kcol, vcol, ko_ref, vo_ref) in enumerate(((COL_KS, COL_VS, kso_ref, vso_ref),
                                                      (COL_KW, COL_VW, kwo_ref, vwo_ref))):
        kt, vt = part_t(kcol), part_t(vcol)
        for g in range(NSA_KV):
            kh = _norm_rope_t(kt[g * HEAD_DIM:(g + 1) * HEAD_DIM], gk_ref[j], cos, sin)
            ko_ref[0, g] = jnp.concatenate([kh, tails[j]], axis=0).T.astype(ko_ref.dtype)
        for g in range(NSA_KV):
            vo_ref[0, g] = jnp.concatenate([vt[g * HEAD_DIM:(g + 1) * HEAD_DIM], row_ones], axis=0).astype(vo_ref.dtype)


def _in_proj(x2, g, w_pad, q_norm_g, k_norm_g, batch, seq):
    n = x2.shape[0]
    nt = seq // ROW_TILE
    assert seq // SEL_BLOCK <= HEAD_DIM
    cos, sin = _rope_tables(np.arange(seq))
    half = ROT_DIM // 2
    cos_t, sin_t = jnp.asarray(cos[:, :half].T.copy()), jnp.asarray(sin[:, :half].T.copy())
    lanes = lambda gain: jnp.tile(gain[..., None], (1,) * gain.ndim + (ROW_TILE,))
    gq = lanes(q_norm_g * (HEAD_DIM ** -0.5 * LOG2E))
    gk = lanes(k_norm_g[1:3])
    row = np.arange(ROW_TILE)
    perm = (np.arange(ROW_TILE)[None, :] == ((row % (ROW_TILE // CMP_STRIDE)) * CMP_STRIDE
                                             + row // (ROW_TILE // CMP_STRIDE))[:, None]).astype(np.float32)
    const = lambda i: (0, 0)
    rows_major = lambda d: pl.BlockSpec((1, NSA_KV, ROW_TILE, d), lambda i: (i // nt, 0, i % nt, 0))
    cols_major = lambda heads, d: pl.BlockSpec((1, heads, d, ROW_TILE), lambda i: (i // nt, 0, 0, i % nt))
    sds = lambda *shape: jax.ShapeDtypeStruct((batch,) + shape, BF16)
    grouped = pl.BlockSpec((1, ROW_TILE // CMP_STRIDE, CMP_STRIDE * LANES), lambda i: (i // nt, i % nt, 0))
    return pl.pallas_call(
        functools.partial(_in_proj_kernel, tiles_per_seq=nt),
        grid=(n // ROW_TILE,),
        in_specs=[pl.BlockSpec((ROW_TILE, D_MODEL), lambda i: (i, 0)),
                  pl.BlockSpec((1, D_MODEL), const),
                  pl.BlockSpec((D_MODEL, IN_PAD), const),
                  pl.BlockSpec((half, ROW_TILE), lambda i: (0, i % nt)),
                  pl.BlockSpec((half, ROW_TILE), lambda i: (0, i % nt)),
                  pl.BlockSpec((HEAD_DIM, ROW_TILE), const),
                  pl.BlockSpec((2, HEAD_DIM, ROW_TILE), lambda i: (0, 0, 0)),
                  pl.BlockSpec((ROW_TILE, ROW_TILE), const)],
        out_specs=[pl.BlockSpec((ROW_TILE, IN_PAD), lambda i: (i, 0)),
                   cols_major(NSA_HEADS, HEAD_DIM), rows_major(2 * HEAD_DIM), cols_major(NSA_KV, V_ROWS),
                   rows_major(2 * HEAD_DIM), cols_major(NSA_KV, V_ROWS), grouped, grouped],
        out_shape=[jax.ShapeDtypeStruct((n, IN_PAD), BF16),
                   sds(NSA_HEADS, HEAD_DIM, seq), sds(NSA_KV, seq, 2 * HEAD_DIM), sds(NSA_KV, V_ROWS, seq),
                   sds(NSA_KV, seq, 2 * HEAD_DIM), sds(NSA_KV, V_ROWS, seq),
                   sds(seq // CMP_STRIDE, CMP_STRIDE * LANES), sds(seq // CMP_STRIDE, CMP_STRIDE * LANES)],
        compiler_params=pltpu.CompilerParams(dimension_semantics=("parallel",),
                                             vmem_limit_bytes=VMEM_LIMIT),
        name="in_proj",
    )(x2, g, w_pad, cos_t, sin_t, gq, gk, jnp.asarray(perm, BF16))


def _compress_kernel(xk_ref, xv_ref, pea_ref, peb_ref, w1a_ref, w1b_ref, w2_ref, gk_ref, cos_ref, sin_ref,
                     rot_ref, kc_ref, vc_ref, shift_scr):
    nc = xk_ref.shape[1]
    for j, x_ref in enumerate((xk_ref, xv_ref)):
        x = x_ref[0].astype(F32)
        u = _dot((x + pea_ref[j:j + 1, :]).astype(BF16), w1a_ref[j])
        v = _dot((x + peb_ref[j:j + 1, :]).astype(BF16), w1b_ref[j])
        shift_scr[0:nc, :] = v
        shift_scr[nc:nc + HALO, :] = jnp.zeros((HALO, v.shape[1]), F32)
        hid = _silu(u + shift_scr[1:nc + 1, :])
        out = _dot(hid.astype(BF16), w2_ref[j])
        if j == 0:
            for g in range(NSA_KV):
                og = (out if g == 0 else pltpu.roll(out, HEAD_DIM, 1))[:, :HEAD_DIM]
                xn = _rms(og, gk_ref[...])
                kc_ref[0, g] = (xn * cos_ref[...] + _dot(xn.astype(BF16), rot_ref[...]) * sin_ref[...]).astype(kc_ref.dtype)
        else:
            vc_ref[0] = out.T.reshape(NSA_KV, HEAD_DIM, nc).astype(vc_ref.dtype)


def _compress(xk, xv, cmp_pe, cmp_w1, cmp_w2, gk0, batch, seq):
    nc = seq // CMP_STRIDE
    hidden = cmp_w1.shape[2]
    cos, sin = (jnp.asarray(t) for t in _rope_tables(np.arange(nc) * CMP_STRIDE + CMP_LEN - 1))
    pe = jnp.tile(cmp_pe[:, :, None, :], (1, 1, NSA_KV, 1))
    pea = pe[:, :CMP_STRIDE].reshape(2, -1)
    peb = pe[:, CMP_STRIDE:].reshape(2, -1)
    eye = jnp.eye(NSA_KV, dtype=cmp_w1.dtype)
    w1 = jnp.einsum('jldh,gk->jlgdkh', cmp_w1.reshape(2, CMP_LEN, HEAD_DIM, hidden), eye)
    w1 = w1.reshape(2, CMP_LEN, NSA_KV * HEAD_DIM, NSA_KV * hidden).astype(BF16)
    w1a = w1[:, :CMP_STRIDE].reshape(2, -1, NSA_KV * hidden)
    w1b = w1[:, CMP_STRIDE:].reshape(2, -1, NSA_KV * hidden)
    w2 = jnp.einsum('jhd,gk->jghkd', cmp_w2, eye).reshape(2, NSA_KV * hidden, NSA_KV * HEAD_DIM).astype(BF16)
    blk = pl.BlockSpec((1, nc, CMP_STRIDE * LANES), lambda b: (b, 0, 0))
    const2 = lambda b: (0, 0)
    const3 = lambda b: (0, 0, 0)
    return pl.pallas_call(
        _compress_kernel,
        grid=(batch,),
        in_specs=[blk, blk,
                  pl.BlockSpec(pea.shape, const2), pl.BlockSpec(peb.shape, const2),
                  pl.BlockSpec(w1a.shape, const3), pl.BlockSpec(w1b.shape, const3), pl.BlockSpec(w2.shape, const3),
                  pl.BlockSpec((1, HEAD_DIM), const2),
                  pl.BlockSpec((nc, HEAD_DIM), const2), pl.BlockSpec((nc, HEAD_DIM), const2),
                  pl.BlockSpec((HEAD_DIM, HEAD_DIM), const2)],
        out_specs=[pl.BlockSpec((1, NSA_KV, nc, HEAD_DIM), lambda b: (b, 0, 0, 0)),
                   pl.BlockSpec((1, NSA_KV, HEAD_DIM, nc), lambda b: (b, 0, 0, 0))],
        out_shape=[jax.ShapeDtypeStruct((batch, NSA_KV, nc, HEAD_DIM), BF16),
                   jax.ShapeDtypeStruct((batch, NSA_KV, HEAD_DIM, nc), BF16)],
        scratch_shapes=[pltpu.VMEM((nc + HALO, NSA_KV * hidden), F32)],
        compiler_params=pltpu.CompilerParams(dimension_semantics=("parallel",),
                                             vmem_limit_bytes=VMEM_LIMIT),
        name="nsa_compress",
    )(xk, xv, pea, peb, w1a, w1b, w2, gk0[None, :], cos, sin, jnp.asarray(_rot_matrix(1), BF16))


NSA_TQ = 256


def _overlap_t(seq):
    ns, nc = seq // SEL_BLOCK, seq // CMP_STRIDE
    cs = np.arange(nc)[None, :] * CMP_STRIDE
    ss = np.arange(ns)[:, None] * SEL_BLOCK
    ov = np.clip(np.minimum(cs + CMP_LEN, ss + SEL_BLOCK) - np.maximum(cs, ss), 0, None)
    return (ov / CMP_LEN).astype(np.float32)


def _cmpsel_kernel(offset_ref, q_ref, kc_ref, vc_ref, ovt_ref, ocmp_ref, bias_ref, *, top):
    tq, nc = q_ref.shape[3], kc_ref.shape[2]
    ns = ovt_ref.shape[0]
    t0 = pl.program_id(1) * tq
    n_i = lax.broadcasted_iota(jnp.int32, (ns, tq), 0)
    t_i = t0 + lax.broadcasted_iota(jnp.int32, (ns, tq), 1)
    cur = t_i // SEL_BLOCK
    forced = jnp.where(n_i == 0, FORCE, jnp.where(n_i == cur, FORCE, jnp.where(n_i == cur - 1, FORCE, 0.0)))
    visible = n_i * SEL_BLOCK <= t_i
    sub = lax.broadcasted_iota(jnp.int32, (SUBLANES, tq), 0)

    def attend(g, quarters):
        ncv, nsv = quarters * nc // 4, quarters * ns // 4
        c_i = lax.broadcasted_iota(jnp.int32, (ncv, NSA_GROUP * tq), 0)
        q_i = lax.broadcasted_iota(jnp.int32, (ncv, NSA_GROUP * tq), 1) & (tq - 1)
        cmask = c_i * CMP_STRIDE + (CMP_LEN - 1) <= t0 + q_i
        qg = jnp.concatenate([q_ref[0, g * NSA_GROUP + h] for h in range(NSA_GROUP)], axis=1)
        s = jnp.where(cmask, _dot(kc_ref[0, g, 0:ncv, :], qg), NEG)
        e = jnp.exp2(s - jnp.max(s, axis=0, keepdims=True))
        p = jnp.where(cmask, e * (1.0 / jnp.sum(e, axis=0, keepdims=True)), 0.0)
        o = _dot(vc_ref[0, g, :, 0:ncv], p.astype(BF16)).astype(ocmp_ref.dtype)
        psum = p[:, :tq]
        for h in range(NSA_GROUP):
            ocmp_ref[0, g * NSA_GROUP + h] = o[:, h * tq:(h + 1) * tq]
            if h:
                psum += p[:, h * tq:(h + 1) * tq]
        ovt = ovt_ref[:, 0:ncv]
        hi, lo = _split(psum)
        val = jnp.where(visible, _dot(ovt, hi) + _dot(ovt, lo) + forced, NEG)
        groups = []
        for r in range(ns // SUBLANES):
            if r * SUBLANES >= nsv:
                groups.append(jnp.zeros((SUBLANES, tq), F32))
                continue
            v_r = val[r * SUBLANES:(r + 1) * SUBLANES]
            c = jnp.zeros((SUBLANES, tq), F32)
            for m in range(nsv):
                row = val[m:m + 1, :]
                if m // SUBLANES > r:
                    hit = row > v_r
                elif m // SUBLANES < r:
                    hit = row >= v_r
                else:
                    hit = jnp.where(sub > m % SUBLANES, jnp.where(row >= v_r, 1.0, 0.0),
                                    jnp.where(row > v_r, 1.0, 0.0)) > 0.5
                c = jnp.where(hit, c + 1.0, c)
            groups.append(c)
        bias = jnp.where(jnp.concatenate(groups, axis=0) < top, offset_ref[0], MASK_BIAS)
        if ns < HEAD_DIM:
            bias = jnp.concatenate([bias, jnp.zeros((HEAD_DIM - ns, tq), F32)], axis=0)
        bias_ref[0, g] = bias.astype(bias_ref.dtype)

    assert ns % (4 * SUBLANES) == 0 and nc % 4 == 0
    last_quarter = (t0 + tq - 1) // (nc // 4 * CMP_STRIDE)
    for g in range(NSA_KV):
        for v in range(4):
            @pl.when(last_quarter == v)
            def _():
                attend(g, v + 1)


def _cmpsel(offset, qt, kc, vct, batch, seq):
    nt = seq // NSA_TQ
    nc, ns = seq // CMP_STRIDE, seq // SEL_BLOCK
    qblk = pl.BlockSpec((1, NSA_HEADS, HEAD_DIM, NSA_TQ), lambda b, t: (b, 0, 0, t))
    return pl.pallas_call(
        functools.partial(_cmpsel_kernel, top=min(SEL_TOPK, ns)),
        grid=(batch, nt),
        in_specs=[pl.BlockSpec(memory_space=pltpu.SMEM), qblk,
                  pl.BlockSpec((1, NSA_KV, nc, HEAD_DIM), lambda b, t: (b, 0, 0, 0)),
                  pl.BlockSpec((1, NSA_KV, HEAD_DIM, nc), lambda b, t: (b, 0, 0, 0)),
                  pl.BlockSpec((ns, nc), lambda b, t: (0, 0))],
        out_specs=[qblk, pl.BlockSpec((1, NSA_KV, HEAD_DIM, NSA_TQ), lambda b, t: (b, 0, 0, t))],
        out_shape=[jax.ShapeDtypeStruct((batch, NSA_HEADS, HEAD_DIM, seq), BF16),
                   jax.ShapeDtypeStruct((batch, NSA_KV, HEAD_DIM, seq), BF16)],
        compiler_params=pltpu.CompilerParams(dimension_semantics=("parallel", "parallel"),
                                             vmem_limit_bytes=VMEM_LIMIT),
        name="nsa_cmpsel",
    )(offset, qt, kc, vct, jnp.asarray(_overlap_t(seq), BF16))


SAFE_OFFSET = 60.0


def _attn_kernel(q_ref, ks_ref, vs_ref, kw_ref, vw_ref, bias_ref, wrow_ref, ocmp_ref, gate_ref, gout_ref,
                 o_ref, acc_scr, *scratch, bounded):
    tq = q_ref.shape[3]
    i = pl.program_id(1)
    t0 = i * tq
    q_sel, q_win = [], []
    for g in range(NSA_KV):
        q_grp = jnp.concatenate([q_ref[0, g * NSA_GROUP + h] for h in range(NSA_GROUP)], axis=1)
        q_sel.append(jnp.concatenate([q_grp, jnp.concatenate([bias_ref[0, g]] * NSA_GROUP, axis=1)], axis=0))
        q_win.append(jnp.concatenate([q_grp, wrow_ref[...]], axis=0))

    def per_head(s, keep):
        return jnp.concatenate([jnp.where(keep, s[:, h * tq:(h + 1) * tq], NEG) for h in range(NSA_GROUP)], axis=1)

    def causal(s):
        key_i = lax.broadcasted_iota(jnp.int32, (tq, tq), 0)
        return per_head(s, key_i <= lax.broadcasted_iota(jnp.int32, (tq, tq), 1))

    def tile(j):
        return pl.ds(pl.multiple_of(j * tq, tq), tq)

    acc_scr[...] = jnp.zeros_like(acc_scr)
    span = WINDOW + tq
    w0 = pl.multiple_of(jnp.maximum(t0 + tq - span, 0), tq)
    dist = (t0 - w0) + (lax.broadcasted_iota(jnp.int32, (span, tq), 1)
                        - lax.broadcasted_iota(jnp.int32, (span, tq), 0))
    in_window = jnp.abs(2 * dist - (WINDOW - 1)) < WINDOW
    win_scores = lambda g: per_head(_dot(kw_ref[0, g, pl.ds(w0, span), :], q_win[g]), in_window)

    if bounded:
        p0_scr, p1_scr = scratch

        def probs(j, p_scr, diag):
            for g in range(NSA_KV):
                s = _dot(ks_ref[0, g, tile(j), :], q_sel[g])
                p_scr[g] = jnp.exp2(causal(s) if diag else s).astype(BF16)

        def accumulate(j, p_scr):
            for g in range(NSA_KV):
                acc_scr[g] += _dot(vs_ref[0, g, :, tile(j)], p_scr[g])

        probs(0, p0_scr, False)

        def body(jj, carry):
            probs(2 * jj + 1, p1_scr, False)
            accumulate(2 * jj, p0_scr)
            probs(2 * jj + 2, p0_scr, False)
            accumulate(2 * jj + 1, p1_scr)
            return carry

        lax.fori_loop(0, (i - 1) // 2, body, 0)

        @pl.when((i > 0) & (i % 2 == 1))
        def _():
            accumulate(i - 1, p0_scr)

        @pl.when((i > 0) & (i % 2 == 0))
        def _():
            probs(i - 1, p1_scr, False)
            accumulate(i - 2, p0_scr)
            accumulate(i - 1, p1_scr)

        probs(i, p0_scr, True)
        s_win = [win_scores(g) for g in range(NSA_KV)]
        accumulate(i, p0_scr)
        win = [_dot(vw_ref[0, g, :, pl.ds(w0, span)], jnp.exp2(s_win[g]).astype(BF16)) for g in range(NSA_KV)]
    else:
        m_scr, s0_scr, s1_scr = scratch
        m_scr[...] = jnp.full(m_scr.shape, NEG, F32)

        def scores(j, s_scr):
            for g in range(NSA_KV):
                s_scr[g] = _dot(ks_ref[0, g, tile(j), :], q_sel[g])

        def consume(j, s_scr, diag):
            for g in range(NSA_KV):
                s = causal(s_scr[g]) if diag else s_scr[g]
                m_old = m_scr[g]
                m_new = jnp.maximum(m_old, jnp.max(s, axis=0, keepdims=True))
                p = jnp.exp2(s - m_new).astype(BF16)
                acc_scr[g] = jnp.exp2(m_old - m_new) * acc_scr[g] + _dot(vs_ref[0, g, :, tile(j)], p)
                m_scr[g] = m_new

        scores(0, s0_scr)

        def body(jj, carry):
            scores(2 * jj + 1, s1_scr)
            consume(2 * jj, s0_scr, False)
            scores(2 * jj + 2, s0_scr)
            consume(2 * jj + 1, s1_scr, False)
            return carry

        lax.fori_loop(0, i // 2, body, 0)

        @pl.when(i % 2 == 0)
        def _():
            consume(i, s0_scr, True)

        @pl.when(i % 2 == 1)
        def _():
            scores(i, s1_scr)
            consume(i - 1, s0_scr, False)
            consume(i, s1_scr, True)

        win = []
        for g in range(NSA_KV):
            s = win_scores(g)
            p = jnp.exp2(s - jnp.max(s, axis=0, keepdims=True)).astype(BF16)
            win.append(_dot(vw_ref[0, g, :, pl.ds(w0, span)], p))

    gate = jax.nn.sigmoid(gate_ref[...].astype(F32).T)
    gout = gout_ref[...]
    rows = []
    for hh in range(NSA_HEADS):
        g, lanes = hh // NSA_GROUP, slice((hh % NSA_GROUP) * tq, (hh % NSA_GROUP + 1) * tq)
        a_s, a_w = acc_scr[g][:, lanes], win[g][:, lanes]
        o = (gate[hh:hh + 1] * ocmp_ref[0, hh].astype(F32)
             + gate[NSA_HEADS + hh:NSA_HEADS + hh + 1] * (a_s[:HEAD_DIM] * (1.0 / a_s[HEAD_DIM:HEAD_DIM + 1]))
             + gate[2 * NSA_HEADS + hh:2 * NSA_HEADS + hh + 1] * (a_w[:HEAD_DIM] * (1.0 / a_w[HEAD_DIM:HEAD_DIM + 1])))
        rows.append(o * lax.rsqrt(jnp.mean(o * o, axis=0, keepdims=True) + EPS) * gout)
    o_ref[...] = jnp.concatenate(rows, axis=0).T.astype(o_ref.dtype)


def _attn(qt, ks, vst, kw, vwt, bias, wrow, ocmp, proj, gout, *, bounded):
    batch, _, _, seq = qt.shape
    nt = seq // NSA_TQ
    assert WINDOW % NSA_TQ == 0 and seq >= WINDOW + NSA_TQ
    wide = NSA_GROUP * NSA_TQ
    rows_full = pl.BlockSpec((1, NSA_KV, seq, 2 * HEAD_DIM), lambda b, t: (b, 0, 0, 0))
    cols_full = pl.BlockSpec((1, NSA_KV, V_ROWS, seq), lambda b, t: (b, 0, 0, 0))
    qtile = lambda heads: pl.BlockSpec((1, heads, HEAD_DIM, NSA_TQ), lambda b, t: (b, 0, 0, t))
    if bounded:
        scratch = [pltpu.VMEM((NSA_KV, NSA_TQ, wide), BF16), pltpu.VMEM((NSA_KV, NSA_TQ, wide), BF16)]
    else:
        scratch = [pltpu.VMEM((NSA_KV, 1, wide), F32),
                   pltpu.VMEM((NSA_KV, NSA_TQ, wide), F32), pltpu.VMEM((NSA_KV, NSA_TQ, wide), F32)]
    return pl.pallas_call(
        functools.partial(_attn_kernel, bounded=bounded),
        grid=(batch, nt),
        in_specs=[qtile(NSA_HEADS), rows_full, cols_full, rows_full, cols_full, qtile(NSA_KV),
                  pl.BlockSpec((HEAD_DIM, wide), lambda b, t: (0, 0)),
                  qtile(NSA_HEADS),
                  pl.BlockSpec((NSA_TQ, LANES), lambda b, t: (b * nt + t, COL_GATE)),
                  pl.BlockSpec((HEAD_DIM, NSA_TQ), lambda b, t: (0, 0))],
        out_specs=pl.BlockSpec((NSA_TQ, NSA_WIDTH), lambda b, t: (b * nt + t, 0)),
        out_shape=jax.ShapeDtypeStruct((batch * seq, NSA_WIDTH), BF16),
        scratch_shapes=[pltpu.VMEM((NSA_KV, V_ROWS, wide), F32)] + scratch,
        compiler_params=pltpu.CompilerParams(dimension_semantics=("parallel", "parallel"),
                                             vmem_limit_bytes=VMEM_LIMIT),
        name="nsa_attn" if bounded else "nsa_attn_online",
    )(qt, ks, vst, kw, vwt, bias, wrow, ocmp, proj, gout)


def _nsa(proj, qt, ks, vst, kw, vwt, xk, xv, q_norm_g, k_norm_g, cmp_pe, cmp_w1, cmp_w2, out_norm_g, batch, seq):
    def offset(gk):
        bound = HEAD_DIM * (HEAD_DIM ** -0.5 * LOG2E) * jnp.max(jnp.abs(q_norm_g)) * jnp.max(jnp.abs(gk))
        return (1.02 * bound).astype(BF16).astype(F32)

    off_sel, off_win = offset(k_norm_g[1]), offset(k_norm_g[2])
    kc, vct = _compress(xk, xv, cmp_pe, cmp_w1, cmp_w2, k_norm_g[0], batch, seq)
    ocmp, bias = _cmpsel(-off_sel[None], qt, kc, vct, batch, seq)
    wrow = jnp.zeros((HEAD_DIM, NSA_GROUP * NSA_TQ), F32).at[0].set(-off_win).astype(BF16)
    gout = jnp.tile(out_norm_g[:, None], (1, NSA_TQ))
    return lax.cond(jnp.maximum(off_sel, off_win) <= SAFE_OFFSET,
                    functools.partial(_attn, bounded=True), functools.partial(_attn, bounded=False),
                    qt, ks, vst, kw, vwt, bias, wrow, ocmp, proj, gout)


def kernel(x, p, attn_norm_g, w_in, hg_lb_logits, hg_norm_g, nsa_q_norm_g, nsa_k_norm_g, cmp_pe,
           cmp_w1, cmp_w2, nsa_out_norm_g, w_out, ffn_norm_g, w_up, conv_w, conv_b, w_down,
           ple_gate_norm_g, w_ple_gate, w_ple, ple_norm_g):
    B, T, _ = x.shape
    n = B * T
    h = x.reshape(n, D_MODEL)
    for i in range(w_in.shape[0]):
        w_pad = jnp.pad(w_in[i], ((0, 0), (0, IN_PAD - IN_TOTAL))).astype(BF16)
        proj, *qkv = _in_proj(h, attn_norm_g[i][None, :], w_pad, nsa_q_norm_g[i], nsa_k_norm_g[i], B, T)
        o_hg = _hgrn(proj, hg_lb_logits.astype(F32), hg_norm_g[i][None, :], i, B, T)
        o_nsa = _nsa(proj, *qkv, nsa_q_norm_g[i], nsa_k_norm_g[i], cmp_pe[i], cmp_w1[i], cmp_w2[i],
                     nsa_out_norm_g[i], B, T)
        h = _post_mixer(h, o_hg, o_nsa, p[i].reshape(n, PLE_DIM), w_out[i].astype(BF16), ffn_norm_g[i][None, :],
                        w_up[i].astype(BF16), conv_w[i], conv_b[i][None, :], w_down[i].astype(BF16),
                        ple_gate_norm_g[i][None, :], w_ple_gate[i].astype(BF16), w_ple[i].astype(BF16),
                        ple_norm_g[i][None, :], T)
    return h.reshape(B, T, D_MODEL)
```

```python
import functools

import numpy as np
import jax
import jax.numpy as jnp
from jax import lax
from jax.experimental import pallas as pl
from jax.experimental.pallas import tpu as pltpu

F32 = jnp.float32
BF16 = jnp.bfloat16

D_MODEL = 1024
PLE_DIM = 256
EPS = 1e-6
NEG = -1e30
FORCE = 1e6

HG_HEADS = 4
HG_KDIM = 128
HG_VDIM = 128
HG_WIDTH = HG_HEADS * HG_VDIM
HG_CHUNK = 64

NSA_HEADS = 8
NSA_KV = 2
NSA_GROUP = NSA_HEADS // NSA_KV
HEAD_DIM = 64
NSA_WIDTH = NSA_HEADS * HEAD_DIM
KV_WIDTH = NSA_KV * HEAD_DIM
CMP_LEN = 32
CMP_STRIDE = 16
SEL_BLOCK = 64
SEL_TOPK = 16
WINDOW = 512
ROT_DIM = HEAD_DIM // 4
ROPE_THETA = 500000.0
D_FF = 2816
CONV_W = 3

IN_SIZES = (HG_HEADS * HG_KDIM, HG_HEADS * HG_KDIM, HG_WIDTH, HG_WIDTH, NSA_WIDTH,
            KV_WIDTH, KV_WIDTH, KV_WIDTH, KV_WIDTH, KV_WIDTH, KV_WIDTH, 3 * NSA_HEADS)
IN_TOTAL = sum(IN_SIZES)
LANES = 128
IN_PAD = -(-IN_TOTAL // LANES) * LANES
VMEM_LIMIT = 56 * 1024 * 1024

ROW_TILE = 512
FF_CHUNK = 256
SUBLANES = 8
HALO = SUBLANES
FFN_HALO = 2 * SUBLANES


def _rms(x, g):
    return x * lax.rsqrt(jnp.mean(x * x, axis=-1, keepdims=True) + EPS) * g


def _dot(a, b):
    return jnp.dot(a, b, preferred_element_type=F32)


def _dot_nt(a, b):
    return lax.dot_general(a, b, (((1,), (1,)), ((), ())), preferred_element_type=F32)


def _dot_tn(a, b):
    return lax.dot_general(a, b, (((0,), (0,)), ((), ())), preferred_element_type=F32)


def _split(x):
    hi = x.astype(BF16)
    return hi, (x - hi.astype(F32)).astype(BF16)


def _silu(x):
    hx = 0.5 * x
    return hx + hx * jnp.tanh(hx)


def _post_mixer_kernel(x_ref, xh_ref, a_ref, ah_ref, b_ref, bh_ref, p_ref, wo_ref, gf_ref, wup_ref, cw_ref,
                       cb_ref, wdn_ref, gg_ref, wg_ref, wp_ref, gp_ref, o_ref,
                       h1_scr, hn_scr, ug_scr, uu_scr, act_scr, *, tiles_per_seq):
    rows = x_ref.shape[0]
    halo, body = slice(0, FFN_HALO), slice(FFN_HALO, FFN_HALO + rows)
    hn_scr[halo, :HG_WIDTH], hn_scr[halo, HG_WIDTH:] = ah_ref[...], bh_ref[...]
    hn_scr[body, :HG_WIDTH], hn_scr[body, HG_WIDTH:] = a_ref[...], b_ref[...]
    mixed = _dot(hn_scr[...], wo_ref[...])
    h1_scr[halo, :] = xh_ref[...] + mixed[halo]
    h1_scr[body, :] = x_ref[...] + mixed[body]
    first = (pl.program_id(0) % tiles_per_seq) == 0
    hn_scr[halo, :] = jnp.where(first, 0.0, _rms(h1_scr[halo, :], gf_ref[...])).astype(BF16)
    hn_scr[body, :] = _rms(h1_scr[body, :], gf_ref[...]).astype(BF16)
    for c in range(0, D_FF, FF_CHUNK):
        conv = []
        for scr, off in ((ug_scr, c), (uu_scr, D_FF + c)):
            scr[...] = _dot(hn_scr[...], wup_ref[:, off:off + FF_CHUNK])
            cw = cw_ref[:, off:off + FF_CHUNK]
            conv.append(scr[FFN_HALO - 2:FFN_HALO - 2 + rows, :] * cw[0:1, :]
                        + scr[FFN_HALO - 1:FFN_HALO - 1 + rows, :] * cw[1:2, :]
                        + scr[body, :] * cw[2:3, :]
                        + cb_ref[:, off:off + FF_CHUNK])
        gate, up = conv
        act_scr[:, c:c + FF_CHUNK] = (_silu(gate) * up).astype(BF16)
    o_ref[...] = h1_scr[body, :] + _dot(act_scr[...], wdn_ref[...])
    hn_scr[body, :] = _rms(o_ref[...], gg_ref[...]).astype(BF16)
    e = _rms(_dot(p_ref[...].astype(BF16), wp_ref[...]), gp_ref[...])
    gate = jax.nn.sigmoid(_dot(hn_scr[body, :], wg_ref[...]))
    o_ref[...] = o_ref[...] + gate * e


def _post_mixer(x2, o_hg, o_nsa, p2, w_out, gf, w_up, conv_w, conv_b, w_down, gg, w_gate, w_ple, gp, seq):
    n = x2.shape[0]
    assert HG_WIDTH + NSA_WIDTH == D_MODEL
    const = lambda i: (0, 0)
    tile = lambda width: pl.BlockSpec((ROW_TILE, width), lambda i: (i, 0))
    halo_blocks = ROW_TILE // FFN_HALO
    halo = lambda width: pl.BlockSpec((FFN_HALO, width), lambda i: (jnp.maximum(i * halo_blocks - 1, 0), 0))
    weight = lambda r, c: pl.BlockSpec((r, c), const, pipeline_mode=pl.Buffered(1))
    padded = FFN_HALO + ROW_TILE
    return pl.pallas_call(
        functools.partial(_post_mixer_kernel, tiles_per_seq=seq // ROW_TILE),
        grid=(n // ROW_TILE,),
        in_specs=[tile(D_MODEL), halo(D_MODEL), tile(HG_WIDTH), halo(HG_WIDTH), tile(NSA_WIDTH), halo(NSA_WIDTH),
                  tile(PLE_DIM),
                  weight(D_MODEL, D_MODEL),
                  pl.BlockSpec((1, D_MODEL), const),
                  weight(D_MODEL, 2 * D_FF),
                  pl.BlockSpec((CONV_W, 2 * D_FF), const),
                  pl.BlockSpec((1, 2 * D_FF), const),
                  weight(D_FF, D_MODEL),
                  pl.BlockSpec((1, D_MODEL), const),
                  weight(D_MODEL, D_MODEL),
                  weight(PLE_DIM, D_MODEL),
                  pl.BlockSpec((1, D_MODEL), const)],
        out_specs=tile(D_MODEL),
        out_shape=jax.ShapeDtypeStruct((n, D_MODEL), F32),
        scratch_shapes=[pltpu.VMEM((padded, D_MODEL), F32),
                        pltpu.VMEM((padded, D_MODEL), BF16),
                        pltpu.VMEM((padded, FF_CHUNK), F32),
                        pltpu.VMEM((padded, FF_CHUNK), F32),
                        pltpu.VMEM((ROW_TILE, D_FF), BF16)],
        compiler_params=pltpu.CompilerParams(dimension_semantics=("parallel",),
                                             vmem_limit_bytes=VMEM_LIMIT),
        name="post_mixer",
    )(x2, x2, o_hg, o_hg, o_nsa, o_nsa, p2, w_out, gf, w_up, conv_w, conv_b, w_down, gg, w_gate, w_ple, gp)


HG_LEVELS = (32, 16, 8, 4, 2, 1)
HG_TB = 512


def _hgrn_level_matrices():
    c = HG_CHUNK
    r = np.arange(c)[:, None]
    u = np.arange(c)[None, :]
    rows = []
    for m in HG_LEVELS:
        r0 = (r // m) * m
        upper = (r & m) != 0
        rows.append(np.where(upper, (u >= r0) & (u <= r), (u >= r + 1) & (u <= r0 + m - 1)))
    rows.append(u <= r)
    rows.append(u > r)
    return np.concatenate(rows, 0).astype(np.float32)


def _hgrn_pair_masks():
    c = HG_CHUNK
    t = np.arange(c)[:, None]
    s = np.arange(c)[None, :]
    masks = [((t // (2 * m)) == (s // (2 * m))) & ((t & m) != 0) & ((s & m) == 0) for m in HG_LEVELS]
    masks.append(t == s)
    return np.stack(masks).astype(np.float32)


def _hgrn_kernel(q_ref, f_ref, v_ref, g_ref, lbl_ref, ng_ref, mall_ref, msk_ref, o_ref, st_ref,
                 *, layer, n_chunks):
    @pl.when(pl.program_id(1) == 0)
    def _():
        st_ref[...] = jnp.zeros_like(st_ref)

    lbl = lbl_ref[...]
    e = jnp.exp(lbl - jnp.max(lbl, axis=0, keepdims=True))
    sm = e / jnp.sum(e, axis=0, keepdims=True)
    lb = jnp.sum(sm[:layer + 1], axis=0, keepdims=True)
    mall = mall_ref[...]
    ng = ng_ref[...]
    c = HG_CHUNK
    nl = len(HG_LEVELS)

    intra, q_decayed, carry_decay, carry_add = {}, {}, {}, {}
    for ci in range(n_chunks):
        r0 = ci * c
        fg = lb + (1.0 - lb) * jax.nn.sigmoid(f_ref[r0:r0 + c, :].astype(F32))
        kk = 1.0 - fg
        ee = jnp.exp(_dot(mall, jnp.concatenate(_split(jnp.log(fg)), axis=0)))
        qf = _silu(q_ref[r0:r0 + c, :].astype(F32)) * HG_KDIM ** -0.5
        for h in range(HG_HEADS):
            sl = slice(h * HG_KDIM, (h + 1) * HG_KDIM)
            qh, kh = qf[:, sl], kk[:, sl]
            a = msk_ref[nl] * _dot_nt(qh.astype(BF16), kh.astype(BF16))
            for j in range(nl):
                ej = ee[j * c:(j + 1) * c, sl]
                a += msk_ref[j] * _dot_nt((qh * ej).astype(BF16), (kh * ej).astype(BF16))
            vh = v_ref[r0:r0 + c, sl]
            eb = ee[nl * c:(nl + 1) * c, sl]
            intra[ci, h] = _dot(a.astype(BF16), vh)
            q_decayed[ci, h] = (qh * eb).astype(BF16)
            carry_decay[ci, h] = eb[c - 1:c, :]
            carry_add[ci, h] = _dot_tn(vh, (kh * ee[(nl + 1) * c:(nl + 2) * c, sl]).astype(BF16))

    for h in range(HG_HEADS):
        sl = slice(h * HG_KDIM, (h + 1) * HG_KDIM)
        st = st_ref[h]
        states = []
        for ci in range(n_chunks):
            states.append(st.astype(BF16))
            st = carry_decay[ci, h] * st + carry_add[ci, h]
        st_ref[h] = st
        for ci in range(n_chunks):
            r0 = ci * c
            o = intra[ci, h] + _dot_nt(q_decayed[ci, h], states[ci])
            on = o * lax.rsqrt(jnp.mean(o * o, axis=-1, keepdims=True) + EPS) * ng
            o_ref[r0:r0 + c, sl] = (on * _silu(g_ref[r0:r0 + c, sl].astype(F32))).astype(o_ref.dtype)


def _hgrn(proj, lb_logits, norm_g, layer, batch, seq):
    nt = seq // HG_TB
    w = HG_HEADS * HG_KDIM
    col = lambda k: (lambda b, t: (b * nt + t, k))
    const2 = lambda b, t: (0, 0)
    return pl.pallas_call(
        functools.partial(_hgrn_kernel, layer=layer, n_chunks=HG_TB // HG_CHUNK),
        grid=(batch, nt),
        in_specs=[pl.BlockSpec((HG_TB, w), col(0)), pl.BlockSpec((HG_TB, w), col(1)),
                  pl.BlockSpec((HG_TB, w), col(2)), pl.BlockSpec((HG_TB, w), col(3)),
                  pl.BlockSpec(lb_logits.shape, const2),
                  pl.BlockSpec((1, HG_VDIM), const2),
                  pl.BlockSpec(((len(HG_LEVELS) + 2) * HG_CHUNK, 2 * HG_CHUNK), const2),
                  pl.BlockSpec((len(HG_LEVELS) + 1, HG_CHUNK, HG_CHUNK), lambda b, t: (0, 0, 0))],
        out_specs=pl.BlockSpec((HG_TB, w), lambda b, t: (b * nt + t, 0)),
        out_shape=jax.ShapeDtypeStruct((batch * seq, HG_WIDTH), BF16),
        scratch_shapes=[pltpu.VMEM((HG_HEADS, HG_VDIM, HG_KDIM), F32)],
        compiler_params=pltpu.CompilerParams(dimension_semantics=("parallel", "arbitrary"),
                                             vmem_limit_bytes=VMEM_LIMIT),
        name="hgrn2",
    )(proj, proj, proj, proj, lb_logits, norm_g,
      jnp.asarray(np.tile(_hgrn_level_matrices(), (1, 2)), BF16), jnp.asarray(_hgrn_pair_masks(), F32))


COL_NQ = 2048 // LANES
COL_KC, COL_VC, COL_KS, COL_VS, COL_KW, COL_VW, COL_GATE = (COL_NQ + 4 + k for k in range(7))
LOG2E = 1.4426950408889634
MASK_BIAS = -1e30
V_ROWS = HEAD_DIM + 2 * SUBLANES


def _rope_tables(pos):
    half = ROT_DIM // 2
    inv = np.float32(ROPE_THETA) ** (-np.arange(half, dtype=np.float32) / np.float32(half))
    ang = pos.astype(np.float32)[:, None] * inv[None, :]
    n = pos.shape[0]
    pad = HEAD_DIM - ROT_DIM
    cos = np.concatenate([np.cos(ang), np.cos(ang), np.ones((n, pad), np.float32)], axis=1)
    sin = np.concatenate([np.sin(ang), np.sin(ang), np.zeros((n, pad), np.float32)], axis=1)
    return cos.astype(np.float32), sin.astype(np.float32)


def _rot_matrix(heads):
    half = ROT_DIM // 2
    r = np.zeros((HEAD_DIM, HEAD_DIM), np.float32)
    for j in range(half):
        r[j + half, j] = -1.0
        r[j, j + half] = 1.0
    return np.kron(np.eye(heads, dtype=np.float32), r)


def _norm_rope_t(xt, gain, cos, sin):
    half = ROT_DIM // 2
    xn = xt * lax.rsqrt(jnp.mean(xt * xt, axis=0, keepdims=True) + EPS) * gain
    x1, x2 = xn[:half], xn[half:ROT_DIM]
    return jnp.concatenate([x1 * cos - x2 * sin, x2 * cos + x1 * sin, xn[ROT_DIM:]], axis=0)


def _in_proj_kernel(x_ref, g_ref, w_ref, cos_ref, sin_ref, gq_ref, gk_ref, perm_ref,
                    o_ref, qo_ref, kso_ref, vso_ref, kwo_ref, vwo_ref, xk_ref, xv_ref, *, tiles_per_seq):
    tb = x_ref.shape[0]
    xb = _rms(x_ref[...], g_ref[...]).astype(BF16)
    hg_cols = COL_NQ * LANES
    qp = _dot(xb, w_ref[:, hg_cols:hg_cols + NSA_WIDTH])
    o_ref[:, hg_cols:hg_cols + NSA_WIDTH] = qp.astype(o_ref.dtype)
    kv = _dot(xb, w_ref[:, COL_KC * LANES:])
    o_ref[:, COL_KC * LANES:] = kv.astype(o_ref.dtype)
    for c in range(0, hg_cols, hg_cols // 2):
        o_ref[:, c:c + hg_cols // 2] = _dot(xb, w_ref[:, c:c + hg_cols // 2]).astype(o_ref.dtype)
    part_t = lambda col: kv[:, (col - COL_KC) * LANES:(col - COL_KC + 1) * LANES].T

    groups = tb // CMP_STRIDE
    banded = _dot(perm_ref[...], kv[:, :2 * LANES].astype(BF16))
    bands = [banded[l * groups:(l + 1) * groups] for l in range(CMP_STRIDE)]
    xk_ref[0] = jnp.concatenate([r[:, :LANES] for r in bands], axis=1).astype(xk_ref.dtype)
    xv_ref[0] = jnp.concatenate([r[:, LANES:] for r in bands], axis=1).astype(xv_ref.dtype)

    cos, sin = cos_ref[...], sin_ref[...]
    qt = qp.T
    for h in range(NSA_HEADS):
        qo_ref[0, h] = _norm_rope_t(qt[h * HEAD_DIM:(h + 1) * HEAD_DIM], gq_ref[...], cos, sin).astype(qo_ref.dtype)

    t0 = (pl.program_id(0) % tiles_per_seq) * tb
    block = (t0 + lax.broadcasted_iota(jnp.int32, (HEAD_DIM, tb), 1)) // SEL_BLOCK
    row_i = lax.broadcasted_iota(jnp.int32, (HEAD_DIM, tb), 0)
    tails = (jnp.where(row_i == block, 1.0, 0.0), jnp.where(row_i == 0, 1.0, 0.0))
    row_ones = jnp.ones((V_ROWS - HEAD_DIM, tb), F32)
    for j, (kcol, vcol, ko_ref, vo_ref) in enumerate(((COL_KS, COL_VS, kso_ref, vso_ref),
                                                      (COL_KW, COL_VW, kwo_ref, vwo_ref))):
        kt, vt = part_t(kcol), part_t(vcol)
        for g in range(NSA_KV):
            kh = _norm_rope_t(kt[g * HEAD_DIM:(g + 1) * HEAD_DIM], gk_ref[j], cos, sin)
            ko_ref[0, g] = jnp.concatenate([kh, tails[j]], axis=0).T.astype(ko_ref.dtype)
        for g in range(NSA_KV):
            vo_ref[0, g] = jnp.concatenate([vt[g * HEAD_DIM:(g + 1) * HEAD_DIM], row_ones], axis=0).astype(vo_ref.dtype)


def _in_proj(x2, g, w_pad, q_norm_g, k_norm_g, batch, seq):
    n = x2.shape[0]
    nt = seq // ROW_TILE
    assert seq // SEL_BLOCK <= HEAD_DIM
    cos, sin = _rope_tables(np.arange(seq))
    half = ROT_DIM // 2
    cos_t, sin_t = jnp.asarray(cos[:, :half].T.copy()), jnp.asarray(sin[:, :half].T.copy())
    lanes = lambda gain: jnp.tile(gain[..., None], (1,) * gain.ndim + (ROW_TILE,))
    gq = lanes(q_norm_g * (HEAD_DIM ** -0.5 * LOG2E))
    gk = lanes(k_norm_g[1:3])
    row = np.arange(ROW_TILE)
    perm = (np.arange(ROW_TILE)[None, :] == ((row % (ROW_TILE // CMP_STRIDE)) * CMP_STRIDE
                                             + row // (ROW_TILE // CMP_STRIDE))[:, None]).astype(np.float32)
    const = lambda i: (0, 0)
    rows_major = lambda d: pl.BlockSpec((1, NSA_KV, ROW_TILE, d), lambda i: (i // nt, 0, i % nt, 0))
    cols_major = lambda heads, d: pl.BlockSpec((1, heads, d, ROW_TILE), lambda i: (i // nt, 0, 0, i % nt))
    sds = lambda *shape: jax.ShapeDtypeStruct((batch,) + shape, BF16)
    grouped = pl.BlockSpec((1, ROW_TILE // CMP_STRIDE, CMP_STRIDE * LANES), lambda i: (i // nt, i % nt, 0))
    return pl.pallas_call(
        functools.partial(_in_proj_kernel, tiles_per_seq=nt),
        grid=(n // ROW_TILE,),
        in_specs=[pl.BlockSpec((ROW_TILE, D_MODEL), lambda i: (i, 0)),
                  pl.BlockSpec((1, D_MODEL), const),
                  pl.BlockSpec((D_MODEL, IN_PAD), const),
                  pl.BlockSpec((half, ROW_TILE), lambda i: (0, i % nt)),
                  pl.BlockSpec((half, ROW_TILE), lambda i: (0, i % nt)),
                  pl.BlockSpec((HEAD_DIM, ROW_TILE), const),
                  pl.BlockSpec((2, HEAD_DIM, ROW_TILE), lambda i: (0, 0, 0)),
                  pl.BlockSpec((ROW_TILE, ROW_TILE), const)],
        out_specs=[pl.BlockSpec((ROW_TILE, IN_PAD), lambda i: (i, 0)),
                   cols_major(NSA_HEADS, HEAD_DIM), rows_major(2 * HEAD_DIM), cols_major(NSA_KV, V_ROWS),
                   rows_major(2 * HEAD_DIM), cols_major(NSA_KV, V_ROWS), grouped, grouped],
        out_shape=[jax.ShapeDtypeStruct((n, IN_PAD), BF16),
                   sds(NSA_HEADS, HEAD_DIM, seq), sds(NSA_KV, seq, 2 * HEAD_DIM), sds(NSA_KV, V_ROWS, seq),
                   sds(NSA_KV, seq, 2 * HEAD_DIM), sds(NSA_KV, V_ROWS, seq),
                   sds(seq // CMP_STRIDE, CMP_STRIDE * LANES), sds(seq // CMP_STRIDE, CMP_STRIDE * LANES)],
        compiler_params=pltpu.CompilerParams(dimension_semantics=("parallel",),
                                             vmem_limit_bytes=VMEM_LIMIT),
        name="in_proj",
    )(x2, g, w_pad, cos_t, sin_t, gq, gk, jnp.asarray(perm, BF16))


def _compress_kernel(xk_ref, xv_ref, pea_ref, peb_ref, w1a_ref, w1b_ref, w2_ref, gk_ref, cos_ref, sin_ref,
                     rot_ref, kc_ref, vc_ref, shift_scr):
    nc = xk_ref.shape[1]
    for j, x_ref in enumerate((xk_ref, xv_ref)):
        x = x_ref[0].astype(F32)
        u = _dot((x + pea_ref[j:j + 1, :]).astype(BF16), w1a_ref[j])
        v = _dot((x + peb_ref[j:j + 1, :]).astype(BF16), w1b_ref[j])
        shift_scr[0:nc, :] = v
        shift_scr[nc:nc + HALO, :] = jnp.zeros((HALO, v.shape[1]), F32)
        hid = _silu(u + shift_scr[1:nc + 1, :])
        out = _dot(hid.astype(BF16), w2_ref[j])
        if j == 0:
            for g in range(NSA_KV):
                og = (out if g == 0 else pltpu.roll(out, HEAD_DIM, 1))[:, :HEAD_DIM]
                xn = _rms(og, gk_ref[...])
                kc_ref[0, g] = (xn * cos_ref[...] + _dot(xn.astype(BF16), rot_ref[...]) * sin_ref[...]).astype(kc_ref.dtype)
        else:
            vc_ref[0] = out.T.reshape(NSA_KV, HEAD_DIM, nc).astype(vc_ref.dtype)


def _compress(xk, xv, cmp_pe, cmp_w1, cmp_w2, gk0, batch, seq):
    nc = seq // CMP_STRIDE
    hidden = cmp_w1.shape[2]
    cos, sin = (jnp.asarray(t) for t in _rope_tables(np.arange(nc) * CMP_STRIDE + CMP_LEN - 1))
    pe = jnp.tile(cmp_pe[:, :, None, :], (1, 1, NSA_KV, 1))
    pea = pe[:, :CMP_STRIDE].reshape(2, -1)
    peb = pe[:, CMP_STRIDE:].reshape(2, -1)
    eye = jnp.eye(NSA_KV, dtype=cmp_w1.dtype)
    w1 = jnp.einsum('jldh,gk->jlgdkh', cmp_w1.reshape(2, CMP_LEN, HEAD_DIM, hidden), eye)
    w1 = w1.reshape(2, CMP_LEN, NSA_KV * HEAD_DIM, NSA_KV * hidden).astype(BF16)
    w1a = w1[:, :CMP_STRIDE].reshape(2, -1, NSA_KV * hidden)
    w1b = w1[:, CMP_STRIDE:].reshape(2, -1, NSA_KV * hidden)
    w2 = jnp.einsum('jhd,gk->jghkd', cmp_w2, eye).reshape(2, NSA_KV * hidden, NSA_KV * HEAD_DIM).astype(BF16)
    blk = pl.BlockSpec((1, nc, CMP_STRIDE * LANES), lambda b: (b, 0, 0))
    const2 = lambda b: (0, 0)
    const3 = lambda b: (0, 0, 0)
    return pl.pallas_call(
        _compress_kernel,
        grid=(batch,),
        in_specs=[blk, blk,
                  pl.BlockSpec(pea.shape, const2), pl.BlockSpec(peb.shape, const2),
                  pl.BlockSpec(w1a.shape, const3), pl.BlockSpec(w1b.shape, const3), pl.BlockSpec(w2.shape, const3),
                  pl.BlockSpec((1, HEAD_DIM), const2),
                  pl.BlockSpec((nc, HEAD_DIM), const2), pl.BlockSpec((nc, HEAD_DIM), const2),
                  pl.BlockSpec((HEAD_DIM, HEAD_DIM), const2)],
        out_specs=[pl.BlockSpec((1, NSA_KV, nc, HEAD_DIM), lambda b: (b, 0, 0, 0)),
                   pl.BlockSpec((1, NSA_KV, HEAD_DIM, nc), lambda b: (b, 0, 0, 0))],
        out_shape=[jax.ShapeDtypeStruct((batch, NSA_KV, nc, HEAD_DIM), BF16),
                   jax.ShapeDtypeStruct((batch, NSA_KV, HEAD_DIM, nc), BF16)],
        scratch_shapes=[pltpu.VMEM((nc + HALO, NSA_KV * hidden), F32)],
        compiler_params=pltpu.CompilerParams(dimension_semantics=("parallel",),
                                             vmem_limit_bytes=VMEM_LIMIT),
        name="nsa_compress",
    )(xk, xv, pea, peb, w1a, w1b, w2, gk0[None, :], cos, sin, jnp.asarray(_rot_matrix(1), BF16))


NSA_TQ = 256


def _overlap_t(seq):
    ns, nc = seq // SEL_BLOCK, seq // CMP_STRIDE
    cs = np.arange(nc)[None, :] * CMP_STRIDE
    ss = np.arange(ns)[:, None] * SEL_BLOCK
    ov = np.clip(np.minimum(cs + CMP_LEN, ss + SEL_BLOCK) - np.maximum(cs, ss), 0, None)
    return (ov / CMP_LEN).astype(np.float32)


def _cmpsel_kernel(offset_ref, q_ref, kc_ref, vc_ref, ovt_ref, ocmp_ref, bias_ref, *, top):
    tq, nc = q_ref.shape[3], kc_ref.shape[2]
    ns = ovt_ref.shape[0]
    t0 = pl.program_id(1) * tq
    n_i = lax.broadcasted_iota(jnp.int32, (ns, tq), 0)
    t_i = t0 + lax.broadcasted_iota(jnp.int32, (ns, tq), 1)
    cur = t_i // SEL_BLOCK
    forced = jnp.where(n_i == 0, FORCE, jnp.where(n_i == cur, FORCE, jnp.where(n_i == cur - 1, FORCE, 0.0)))
    visible = n_i * SEL_BLOCK <= t_i
    sub = lax.broadcasted_iota(jnp.int32, (SUBLANES, tq), 0)

    def attend(g, quarters):
        ncv, nsv = quarters * nc // 4, quarters * ns // 4
        safe = max((quarters - 1) * nc // 4 - SUBLANES, 0)
        c_i = safe + lax.broadcasted_iota(jnp.int32, (ncv - safe, NSA_GROUP * tq), 0)
        q_i = lax.broadcasted_iota(jnp.int32, (ncv - safe, NSA_GROUP * tq), 1) & (tq - 1)
        cmask = c_i * CMP_STRIDE + (CMP_LEN - 1) <= t0 + q_i
        qg = jnp.concatenate([q_ref[0, g * NSA_GROUP + h] for h in range(NSA_GROUP)], axis=1)
        s = _dot(kc_ref[0, g, 0:ncv, :], qg)
        s_hi = jnp.where(cmask, s[safe:], NEG)
        m = jnp.max(s_hi, axis=0, keepdims=True)
        if safe:
            m = jnp.maximum(m, jnp.max(s[:safe], axis=0, keepdims=True))
        e_hi = jnp.exp2(s_hi - m)
        total = jnp.sum(e_hi, axis=0, keepdims=True)
        if safe:
            e_lo = jnp.exp2(s[:safe] - m)
            total += jnp.sum(e_lo, axis=0, keepdims=True)
        inv = 1.0 / total
        p = jnp.where(cmask, e_hi * inv, 0.0)
        if safe:
            p = jnp.concatenate([e_lo * inv, p], axis=0)
        o = _dot(vc_ref[0, g, :, 0:ncv], p.astype(BF16)).astype(ocmp_ref.dtype)
        psum = p[:, :tq]
        for h in range(NSA_GROUP):
            ocmp_ref[0, g * NSA_GROUP + h] = o[:, h * tq:(h + 1) * tq]
            if h:
                psum += p[:, h * tq:(h + 1) * tq]
        ovt = ovt_ref[:, 0:ncv]
        hi, lo = _split(psum)
        val = jnp.where(visible, _dot(ovt, hi) + _dot(ovt, lo) + forced, NEG)
        groups = []
        for r in range(ns // SUBLANES):
            if r * SUBLANES >= nsv:
                groups.append(jnp.zeros((SUBLANES, tq), F32))
                continue
            v_r = val[r * SUBLANES:(r + 1) * SUBLANES]
            c = jnp.zeros((SUBLANES, tq), F32)
            for m in range(nsv):
                row = val[m:m + 1, :]
                if m // SUBLANES > r:
                    hit = row > v_r
                elif m // SUBLANES < r:
                    hit = row >= v_r
                else:
                    hit = jnp.where(sub > m % SUBLANES, jnp.where(row >= v_r, 1.0, 0.0),
                                    jnp.where(row > v_r, 1.0, 0.0)) > 0.5
                c = jnp.where(hit, c + 1.0, c)
            groups.append(c)
        bias = jnp.where(jnp.concatenate(groups, axis=0) < top, offset_ref[0], MASK_BIAS)
        if ns < HEAD_DIM:
            bias = jnp.concatenate([bias, jnp.zeros((HEAD_DIM - ns, tq), F32)], axis=0)
        bias_ref[0, g] = bias.astype(bias_ref.dtype)

    assert ns % (4 * SUBLANES) == 0 and nc % 4 == 0
    last_quarter = (t0 + tq - 1) // (nc // 4 * CMP_STRIDE)
    for g in range(NSA_KV):
        for v in range(4):
            @pl.when(last_quarter == v)
            def _():
                attend(g, v + 1)


def _cmpsel(offset, qt, kc, vct, batch, seq):
    nt = seq // NSA_TQ
    nc, ns = seq // CMP_STRIDE, seq // SEL_BLOCK
    qblk = pl.BlockSpec((1, NSA_HEADS, HEAD_DIM, NSA_TQ), lambda b, t: (b, 0, 0, t))
    return pl.pallas_call(
        functools.partial(_cmpsel_kernel, top=min(SEL_TOPK, ns)),
        grid=(batch, nt),
        in_specs=[pl.BlockSpec(memory_space=pltpu.SMEM), qblk,
                  pl.BlockSpec((1, NSA_KV, nc, HEAD_DIM), lambda b, t: (b, 0, 0, 0)),
                  pl.BlockSpec((1, NSA_KV, HEAD_DIM, nc), lambda b, t: (b, 0, 0, 0)),
                  pl.BlockSpec((ns, nc), lambda b, t: (0, 0))],
        out_specs=[qblk, pl.BlockSpec((1, NSA_KV, HEAD_DIM, NSA_TQ), lambda b, t: (b, 0, 0, t))],
        out_shape=[jax.ShapeDtypeStruct((batch, NSA_HEADS, HEAD_DIM, seq), BF16),
                   jax.ShapeDtypeStruct((batch, NSA_KV, HEAD_DIM, seq), BF16)],
        compiler_params=pltpu.CompilerParams(dimension_semantics=("parallel", "parallel"),
                                             vmem_limit_bytes=VMEM_LIMIT),
        name="nsa_cmpsel",
    )(offset, qt, kc, vct, jnp.asarray(_overlap_t(seq), BF16))


SAFE_OFFSET = 60.0


def _attn_kernel(q_ref, ks_ref, vs_ref, kw_ref, vw_ref, bias_ref, wrow_ref, ocmp_ref, gate_ref, gout_ref,
                 o_ref, acc_scr, *scratch, bounded):
    tq = q_ref.shape[3]
    i = pl.program_id(1)
    t0 = i * tq
    q_sel, q_win = [], []
    for g in range(NSA_KV):
        q_grp = jnp.concatenate([q_ref[0, g * NSA_GROUP + h] for h in range(NSA_GROUP)], axis=1)
        q_sel.append(jnp.concatenate([q_grp, jnp.concatenate([bias_ref[0, g]] * NSA_GROUP, axis=1)], axis=0))
        q_win.append(jnp.concatenate([q_grp, wrow_ref[...]], axis=0))

    def per_head(s, keep):
        return jnp.concatenate([jnp.where(keep, s[:, h * tq:(h + 1) * tq], NEG) for h in range(NSA_GROUP)], axis=1)

    def causal(s):
        key_i = lax.broadcasted_iota(jnp.int32, (tq, tq), 0)
        return per_head(s, key_i <= lax.broadcasted_iota(jnp.int32, (tq, tq), 1))

    def tile(j):
        return pl.ds(pl.multiple_of(j * tq, tq), tq)

    acc_scr[...] = jnp.zeros_like(acc_scr)
    span = WINDOW + tq
    w0 = pl.multiple_of(jnp.maximum(t0 + tq - span, 0), tq)
    dist = (t0 - w0) + (lax.broadcasted_iota(jnp.int32, (span, tq), 1)
                        - lax.broadcasted_iota(jnp.int32, (span, tq), 0))
    in_window = jnp.abs(2 * dist - (WINDOW - 1)) < WINDOW
    win_scores = lambda g: per_head(_dot(kw_ref[0, g, pl.ds(w0, span), :], q_win[g]), in_window)

    if bounded:
        p0_scr, p1_scr = scratch

        def probs(j, p_scr, diag):
            for g in range(NSA_KV):
                s = _dot(ks_ref[0, g, tile(j), :], q_sel[g])
                p_scr[g] = jnp.exp2(causal(s) if diag else s).astype(BF16)

        def accumulate(j, p_scr):
            for g in range(NSA_KV):
                acc_scr[g] += _dot(vs_ref[0, g, :, tile(j)], p_scr[g])

        probs(0, p0_scr, False)

        def body(jj, carry):
            probs(2 * jj + 1, p1_scr, False)
            accumulate(2 * jj, p0_scr)
            probs(2 * jj + 2, p0_scr, False)
            accumulate(2 * jj + 1, p1_scr)
            return carry

        lax.fori_loop(0, (i - 1) // 2, body, 0)

        @pl.when((i > 0) & (i % 2 == 1))
        def _():
            accumulate(i - 1, p0_scr)

        @pl.when((i > 0) & (i % 2 == 0))
        def _():
            probs(i - 1, p1_scr, False)
            accumulate(i - 2, p0_scr)
            accumulate(i - 1, p1_scr)

        probs(i, p0_scr, True)
        s_win = [win_scores(g) for g in range(NSA_KV)]
        accumulate(i, p0_scr)
        win = [_dot(vw_ref[0, g, :, pl.ds(w0, span)], jnp.exp2(s_win[g]).astype(BF16)) for g in range(NSA_KV)]
    else:
        m_scr, s0_scr, s1_scr = scratch
        m_scr[...] = jnp.full(m_scr.shape, NEG, F32)

        def scores(j, s_scr):
            for g in range(NSA_KV):
                s_scr[g] = _dot(ks_ref[0, g, tile(j), :], q_sel[g])

        def consume(j, s_scr, diag):
            for g in range(NSA_KV):
                s = causal(s_scr[g]) if diag else s_scr[g]
                m_old = m_scr[g]
                m_new = jnp.maximum(m_old, jnp.max(s, axis=0, keepdims=True))
                p = jnp.exp2(s - m_new).astype(BF16)
                acc_scr[g] = jnp.exp2(m_old - m_new) * acc_scr[g] + _dot(vs_ref[0, g, :, tile(j)], p)
                m_scr[g] = m_new

        scores(0, s0_scr)

        def body(jj, carry):
            scores(2 * jj + 1, s1_scr)
            consume(2 * jj, s0_scr, False)
            scores(2 * jj + 2, s0_scr)
            consume(2 * jj + 1, s1_scr, False)
            return carry

        lax.fori_loop(0, i // 2, body, 0)

        @pl.when(i % 2 == 0)
        def _():
            consume(i, s0_scr, True)

        @pl.when(i % 2 == 1)
        def _():
            scores(i, s1_scr)
            consume(i - 1, s0_scr, False)
            consume(i, s1_scr, True)

        win = []
        for g in range(NSA_KV):
            s = win_scores(g)
            p = jnp.exp2(s - jnp.max(s, axis=0, keepdims=True)).astype(BF16)
            win.append(_dot(vw_ref[0, g, :, pl.ds(w0, span)], p))

    gate = jax.nn.sigmoid(gate_ref[...].astype(F32).T)
    gout = gout_ref[...]
    rows = []
    for hh in range(NSA_HEADS):
        g, lanes = hh // NSA_GROUP, slice((hh % NSA_GROUP) * tq, (hh % NSA_GROUP + 1) * tq)
        a_s, a_w = acc_scr[g][:, lanes], win[g][:, lanes]
        o = (gate[hh:hh + 1] * ocmp_ref[0, hh].astype(F32)
             + gate[NSA_HEADS + hh:NSA_HEADS + hh + 1] * (a_s[:HEAD_DIM] * (1.0 / a_s[HEAD_DIM:HEAD_DIM + 1]))
             + gate[2 * NSA_HEADS + hh:2 * NSA_HEADS + hh + 1] * (a_w[:HEAD_DIM] * (1.0 / a_w[HEAD_DIM:HEAD_DIM + 1])))
        rows.append(o * lax.rsqrt(jnp.mean(o * o, axis=0, keepdims=True) + EPS) * gout)
    o_ref[...] = jnp.concatenate(rows, axis=0).T.astype(o_ref.dtype)


def _attn(qt, ks, vst, kw, vwt, bias, wrow, ocmp, proj, gout, *, bounded):
    batch, _, _, seq = qt.shape
    nt = seq // NSA_TQ
    assert WINDOW % NSA_TQ == 0 and seq >= WINDOW + NSA_TQ
    wide = NSA_GROUP * NSA_TQ
    rows_full = pl.BlockSpec((1, NSA_KV, seq, 2 * HEAD_DIM), lambda b, t: (b, 0, 0, 0))
    cols_full = pl.BlockSpec((1, NSA_KV, V_ROWS, seq), lambda b, t: (b, 0, 0, 0))
    qtile = lambda heads: pl.BlockSpec((1, heads, HEAD_DIM, NSA_TQ), lambda b, t: (b, 0, 0, t))
    if bounded:
        scratch = [pltpu.VMEM((NSA_KV, NSA_TQ, wide), BF16), pltpu.VMEM((NSA_KV, NSA_TQ, wide), BF16)]
    else:
        scratch = [pltpu.VMEM((NSA_KV, 1, wide), F32),
                   pltpu.VMEM((NSA_KV, NSA_TQ, wide), F32), pltpu.VMEM((NSA_KV, NSA_TQ, wide), F32)]
    return pl.pallas_call(
        functools.partial(_attn_kernel, bounded=bounded),
        grid=(batch, nt),
        in_specs=[qtile(NSA_HEADS), rows_full, cols_full, rows_full, cols_full, qtile(NSA_KV),
                  pl.BlockSpec((HEAD_DIM, wide), lambda b, t: (0, 0)),
                  qtile(NSA_HEADS),
                  pl.BlockSpec((NSA_TQ, LANES), lambda b, t: (b * nt + t, COL_GATE)),
                  pl.BlockSpec((HEAD_DIM, NSA_TQ), lambda b, t: (0, 0))],
        out_specs=pl.BlockSpec((NSA_TQ, NSA_WIDTH), lambda b, t: (b * nt + t, 0)),
        out_shape=jax.ShapeDtypeStruct((batch * seq, NSA_WIDTH), BF16),
        scratch_shapes=[pltpu.VMEM((NSA_KV, V_ROWS, wide), F32)] + scratch,
        compiler_params=pltpu.CompilerParams(dimension_semantics=("parallel", "parallel"),
                                             vmem_limit_bytes=VMEM_LIMIT),
        name="nsa_attn" if bounded else "nsa_attn_online",
    )(qt, ks, vst, kw, vwt, bias, wrow, ocmp, proj, gout)


def _nsa(proj, qt, ks, vst, kw, vwt, xk, xv, q_norm_g, k_norm_g, cmp_pe, cmp_w1, cmp_w2, out_norm_g, batch, seq):
    def offset(gk):
        bound = HEAD_DIM * (HEAD_DIM ** -0.5 * LOG2E) * jnp.max(jnp.abs(q_norm_g)) * jnp.max(jnp.abs(gk))
        return (1.02 * bound).astype(BF16).astype(F32)

    off_sel, off_win = offset(k_norm_g[1]), offset(k_norm_g[2])
    kc, vct = _compress(xk, xv, cmp_pe, cmp_w1, cmp_w2, k_norm_g[0], batch, seq)
    ocmp, bias = _cmpsel(-off_sel[None], qt, kc, vct, batch, seq)
    wrow = jnp.zeros((HEAD_DIM, NSA_GROUP * NSA_TQ), F32).at[0].set(-off_win).astype(BF16)
    gout = jnp.tile(out_norm_g[:, None], (1, NSA_TQ))
    return lax.cond(jnp.maximum(off_sel, off_win) <= SAFE_OFFSET,
                    functools.partial(_attn, bounded=True), functools.partial(_attn, bounded=False),
                    qt, ks, vst, kw, vwt, bias, wrow, ocmp, proj, gout)


def kernel(x, p, attn_norm_g, w_in, hg_lb_logits, hg_norm_g, nsa_q_norm_g, nsa_k_norm_g, cmp_pe,
           cmp_w1, cmp_w2, nsa_out_norm_g, w_out, ffn_norm_g, w_up, conv_w, conv_b, w_down,
           ple_gate_norm_g, w_ple_gate, w_ple, ple_norm_g):
    B, T, _ = x.shape
    n = B * T
    h = x.reshape(n, D_MODEL)
    for i in range(w_in.shape[0]):
        w_pad = jnp.pad(w_in[i], ((0, 0), (0, IN_PAD - IN_TOTAL))).astype(BF16)
        proj, *qkv = _in_proj(h, attn_norm_g[i][None, :], w_pad, nsa_q_norm_g[i], nsa_k_norm_g[i], B, T)
        o_hg = _hgrn(proj, hg_lb_logits.astype(F32), hg_norm_g[i][None, :], i, B, T)
        o_nsa = _nsa(proj, *qkv, nsa_q_norm_g[i], nsa_k_norm_g[i], cmp_pe[i], cmp_w1[i], cmp_w2[i],
                     nsa_out_norm_g[i], B, T)
        h = _post_mixer(h, o_hg, o_nsa, p[i].reshape(n, PLE_DIM), w_out[i].astype(BF16), ffn_norm_g[i][None, :],
                        w_up[i].astype(BF16), conv_w[i], conv_b[i][None, :], w_down[i].astype(BF16),
                        ple_gate_norm_g[i][None, :], w_ple_gate[i].astype(BF16), w_ple[i].astype(BF16),
                        ple_norm_g[i][None, :], T)
    return h.reshape(B, T, D_MODEL)
```

```python
import functools

import numpy as np
import jax
import jax.numpy as jnp
from jax import lax
from jax.experimental import pallas as pl
from jax.experimental.pallas import tpu as pltpu

F32 = jnp.float32
BF16 = jnp.bfloat16

D_MODEL = 1024
PLE_DIM = 256
EPS = 1e-6
NEG = -1e30
FORCE = 1e6

HG_HEADS = 4
HG_KDIM = 128
HG_VDIM = 128
HG_WIDTH = HG_HEADS * HG_VDIM
HG_CHUNK = 64

NSA_HEADS = 8
NSA_KV = 2
NSA_GROUP = NSA_HEADS // NSA_KV
HEAD_DIM = 64
NSA_WIDTH = NSA_HEADS * HEAD_DIM
KV_WIDTH = NSA_KV * HEAD_DIM
CMP_LEN = 32
CMP_STRIDE = 16
SEL_BLOCK = 64
SEL_TOPK = 16
WINDOW = 512
ROT_DIM = HEAD_DIM // 4
ROPE_THETA = 500000.0
D_FF = 2816
CONV_W = 3

IN_SIZES = (HG_HEADS * HG_KDIM, HG_HEADS * HG_KDIM, HG_WIDTH, HG_WIDTH, NSA_WIDTH,
            KV_WIDTH, KV_WIDTH, KV_WIDTH, KV_WIDTH, KV_WIDTH, KV_WIDTH, 3 * NSA_HEADS)
IN_TOTAL = sum(IN_SIZES)
LANES = 128
IN_PAD = -(-IN_TOTAL // LANES) * LANES
VMEM_LIMIT = 56 * 1024 * 1024

ROW_TILE = 512
FF_CHUNK = 256
SUBLANES = 8
HALO = SUBLANES
FFN_HALO = 2 * SUBLANES


def _rms(x, g):
    return x * lax.rsqrt(jnp.mean(x * x, axis=-1, keepdims=True) + EPS) * g


def _dot(a, b):
    return jnp.dot(a, b, preferred_element_type=F32)


def _dot_nt(a, b):
    return lax.dot_general(a, b, (((1,), (1,)), ((), ())), preferred_element_type=F32)


def _dot_tn(a, b):
    return lax.dot_general(a, b, (((0,), (0,)), ((), ())), preferred_element_type=F32)


def _split(x):
    hi = x.astype(BF16)
    return hi, (x - hi.astype(F32)).astype(BF16)


def _silu(x):
    hx = 0.5 * x
    return hx + hx * jnp.tanh(hx)


def _post_mixer_kernel(x_ref, xh_ref, a_ref, ah_ref, b_ref, bh_ref, p_ref, wo_ref, gf_ref, wup_ref, cw_ref,
                       cb_ref, wdn_ref, gg_ref, wg_ref, wp_ref, gp_ref, o_ref,
                       h1_scr, hn_scr, ug_scr, uu_scr, act_scr, *, tiles_per_seq):
    rows = x_ref.shape[0]
    halo, body = slice(0, FFN_HALO), slice(FFN_HALO, FFN_HALO + rows)
    hn_scr[halo, :HG_WIDTH], hn_scr[halo, HG_WIDTH:] = ah_ref[...], bh_ref[...]
    hn_scr[body, :HG_WIDTH], hn_scr[body, HG_WIDTH:] = a_ref[...], b_ref[...]
    mixed = _dot(hn_scr[...], wo_ref[...])
    h1_scr[halo, :] = xh_ref[...] + mixed[halo]
    h1_scr[body, :] = x_ref[...] + mixed[body]
    first = (pl.program_id(0) % tiles_per_seq) == 0
    hn_scr[halo, :] = jnp.where(first, 0.0, _rms(h1_scr[halo, :], gf_ref[...])).astype(BF16)
    hn_scr[body, :] = _rms(h1_scr[body, :], gf_ref[...]).astype(BF16)
    for c in range(0, D_FF, FF_CHUNK):
        conv = []
        for scr, off in ((ug_scr, c), (uu_scr, D_FF + c)):
            scr[...] = _dot(hn_scr[...], wup_ref[:, off:off + FF_CHUNK])
            cw = cw_ref[:, off:off + FF_CHUNK]
            conv.append(scr[FFN_HALO - 2:FFN_HALO - 2 + rows, :] * cw[0:1, :]
                        + scr[FFN_HALO - 1:FFN_HALO - 1 + rows, :] * cw[1:2, :]
                        + scr[body, :] * cw[2:3, :]
                        + cb_ref[:, off:off + FF_CHUNK])
        gate, up = conv
        act_scr[:, c:c + FF_CHUNK] = (_silu(gate) * up).astype(BF16)
    o_ref[...] = h1_scr[body, :] + _dot(act_scr[...], wdn_ref[...])
    hn_scr[body, :] = _rms(o_ref[...], gg_ref[...]).astype(BF16)
    e = _rms(_dot(p_ref[...].astype(BF16), wp_ref[...]), gp_ref[...])
    gate = jax.nn.sigmoid(_dot(hn_scr[body, :], wg_ref[...]))
    o_ref[...] = o_ref[...] + gate * e


def _post_mixer(x2, o_hg, o_nsa, p2, w_out, gf, w_up, conv_w, conv_b, w_down, gg, w_gate, w_ple, gp, seq):
    n = x2.shape[0]
    assert HG_WIDTH + NSA_WIDTH == D_MODEL
    const = lambda i: (0, 0)
    tile = lambda width: pl.BlockSpec((ROW_TILE, width), lambda i: (i, 0))
    halo_blocks = ROW_TILE // FFN_HALO
    halo = lambda width: pl.BlockSpec((FFN_HALO, width), lambda i: (jnp.maximum(i * halo_blocks - 1, 0), 0))
    weight = lambda r, c: pl.BlockSpec((r, c), const, pipeline_mode=pl.Buffered(1))
    padded = FFN_HALO + ROW_TILE
    return pl.pallas_call(
        functools.partial(_post_mixer_kernel, tiles_per_seq=seq // ROW_TILE),
        grid=(n // ROW_TILE,),
        in_specs=[tile(D_MODEL), halo(D_MODEL), tile(HG_WIDTH), halo(HG_WIDTH), tile(NSA_WIDTH), halo(NSA_WIDTH),
                  tile(PLE_DIM),
                  weight(D_MODEL, D_MODEL),
                  pl.BlockSpec((1, D_MODEL), const),
                  weight(D_MODEL, 2 * D_FF),
                  pl.BlockSpec((CONV_W, 2 * D_FF), const),
                  pl.BlockSpec((1, 2 * D_FF), const),
                  weight(D_FF, D_MODEL),
                  pl.BlockSpec((1, D_MODEL), const),
                  weight(D_MODEL, D_MODEL),
                  weight(PLE_DIM, D_MODEL),
                  pl.BlockSpec((1, D_MODEL), const)],
        out_specs=tile(D_MODEL),
        out_shape=jax.ShapeDtypeStruct((n, D_MODEL), F32),
        scratch_shapes=[pltpu.VMEM((padded, D_MODEL), F32),
                        pltpu.VMEM((padded, D_MODEL), BF16),
                        pltpu.VMEM((padded, FF_CHUNK), F32),
                        pltpu.VMEM((padded, FF_CHUNK), F32),
                        pltpu.VMEM((ROW_TILE, D_FF), BF16)],
        compiler_params=pltpu.CompilerParams(dimension_semantics=("parallel",),
                                             vmem_limit_bytes=VMEM_LIMIT),
        name="post_mixer",
    )(x2, x2, o_hg, o_hg, o_nsa, o_nsa, p2, w_out, gf, w_up, conv_w, conv_b, w_down, gg, w_gate, w_ple, gp)


HG_LEVELS = (32, 16, 8, 4, 2, 1)
HG_TB = 512


def _hgrn_level_matrices():
    c = HG_CHUNK
    r = np.arange(c)[:, None]
    u = np.arange(c)[None, :]
    rows = []
    for m in HG_LEVELS:
        r0 = (r // m) * m
        upper = (r & m) != 0
        rows.append(np.where(upper, (u >= r0) & (u <= r), (u >= r + 1) & (u <= r0 + m - 1)))
    rows.append(u <= r)
    rows.append(u > r)
    return np.concatenate(rows, 0).astype(np.float32)


def _hgrn_pair_masks():
    c = HG_CHUNK
    t = np.arange(c)[:, None]
    s = np.arange(c)[None, :]
    masks = [((t // (2 * m)) == (s // (2 * m))) & ((t & m) != 0) & ((s & m) == 0) for m in HG_LEVELS]
    masks.append(t == s)
    return np.stack(masks).astype(np.float32)


def _hgrn_kernel(q_ref, f_ref, v_ref, g_ref, lbl_ref, ng_ref, mall_ref, msk_ref, o_ref, st_ref,
                 *, layer, n_chunks):
    @pl.when(pl.program_id(1) == 0)
    def _():
        st_ref[...] = jnp.zeros_like(st_ref)

    lbl = lbl_ref[...]
    e = jnp.exp(lbl - jnp.max(lbl, axis=0, keepdims=True))
    sm = e / jnp.sum(e, axis=0, keepdims=True)
    lb = jnp.sum(sm[:layer + 1], axis=0, keepdims=True)
    mall = mall_ref[...]
    ng = ng_ref[...]
    c = HG_CHUNK
    nl = len(HG_LEVELS)

    intra, q_decayed, carry_decay, carry_add = {}, {}, {}, {}
    for ci in range(n_chunks):
        r0 = ci * c
        fg = lb + (1.0 - lb) * jax.nn.sigmoid(f_ref[r0:r0 + c, :].astype(F32))
        kk = 1.0 - fg
        ee = jnp.exp(_dot(mall, jnp.concatenate(_split(jnp.log(fg)), axis=0)))
        qf = _silu(q_ref[r0:r0 + c, :].astype(F32)) * HG_KDIM ** -0.5
        for h in range(HG_HEADS):
            sl = slice(h * HG_KDIM, (h + 1) * HG_KDIM)
            qh, kh = qf[:, sl], kk[:, sl]
            a = msk_ref[nl] * _dot_nt(qh.astype(BF16), kh.astype(BF16))
            for j in range(nl):
                ej = ee[j * c:(j + 1) * c, sl]
                a += msk_ref[j] * _dot_nt((qh * ej).astype(BF16), (kh * ej).astype(BF16))
            vh = v_ref[r0:r0 + c, sl]
            eb = ee[nl * c:(nl + 1) * c, sl]
            intra[ci, h] = _dot(a.astype(BF16), vh)
            q_decayed[ci, h] = (qh * eb).astype(BF16)
            carry_decay[ci, h] = eb[c - 1:c, :]
            carry_add[ci, h] = _dot_tn(vh, (kh * ee[(nl + 1) * c:(nl + 2) * c, sl]).astype(BF16))

    for h in range(HG_HEADS):
        sl = slice(h * HG_KDIM, (h + 1) * HG_KDIM)
        st = st_ref[h]
        states = []
        for ci in range(n_chunks):
            states.append(st.astype(BF16))
            st = carry_decay[ci, h] * st + carry_add[ci, h]
        st_ref[h] = st
        for ci in range(n_chunks):
            r0 = ci * c
            o = intra[ci, h] + _dot_nt(q_decayed[ci, h], states[ci])
            on = o * lax.rsqrt(jnp.mean(o * o, axis=-1, keepdims=True) + EPS) * ng
            o_ref[r0:r0 + c, sl] = (on * _silu(g_ref[r0:r0 + c, sl].astype(F32))).astype(o_ref.dtype)


def _hgrn(proj, lb_logits, norm_g, layer, batch, seq):
    nt = seq // HG_TB
    w = HG_HEADS * HG_KDIM
    col = lambda k: (lambda b, t: (b * nt + t, k))
    const2 = lambda b, t: (0, 0)
    return pl.pallas_call(
        functools.partial(_hgrn_kernel, layer=layer, n_chunks=HG_TB // HG_CHUNK),
        grid=(batch, nt),
        in_specs=[pl.BlockSpec((HG_TB, w), col(0)), pl.BlockSpec((HG_TB, w), col(1)),
                  pl.BlockSpec((HG_TB, w), col(2)), pl.BlockSpec((HG_TB, w), col(3)),
                  pl.BlockSpec(lb_logits.shape, const2),
                  pl.BlockSpec((1, HG_VDIM), const2),
                  pl.BlockSpec(((len(HG_LEVELS) + 2) * HG_CHUNK, 2 * HG_CHUNK), const2),
                  pl.BlockSpec((len(HG_LEVELS) + 1, HG_CHUNK, HG_CHUNK), lambda b, t: (0, 0, 0))],
        out_specs=pl.BlockSpec((HG_TB, w), lambda b, t: (b * nt + t, 0)),
        out_shape=jax.ShapeDtypeStruct((batch * seq, HG_WIDTH), BF16),
        scratch_shapes=[pltpu.VMEM((HG_HEADS, HG_VDIM, HG_KDIM), F32)],
        compiler_params=pltpu.CompilerParams(dimension_semantics=("parallel", "arbitrary"),
                                             vmem_limit_bytes=VMEM_LIMIT),
        name="hgrn2",
    )(proj, proj, proj, proj, lb_logits, norm_g,
      jnp.asarray(np.tile(_hgrn_level_matrices(), (1, 2)), BF16), jnp.asarray(_hgrn_pair_masks(), F32))


COL_NQ = 2048 // LANES
COL_KC, COL_VC, COL_KS, COL_VS, COL_KW, COL_VW, COL_GATE = (COL_NQ + 4 + k for k in range(7))
LOG2E = 1.4426950408889634
MASK_BIAS = -1e30
V_ROWS = HEAD_DIM + 2 * SUBLANES


def _rope_tables(pos):
    half = ROT_DIM // 2
    inv = np.float32(ROPE_THETA) ** (-np.arange(half, dtype=np.float32) / np.float32(half))
    ang = pos.astype(np.float32)[:, None] * inv[None, :]
    n = pos.shape[0]
    pad = HEAD_DIM - ROT_DIM
    cos = np.concatenate([np.cos(ang), np.cos(ang), np.ones((n, pad), np.float32)], axis=1)
    sin = np.concatenate([np.sin(ang), np.sin(ang), np.zeros((n, pad), np.float32)], axis=1)
    return cos.astype(np.float32), sin.astype(np.float32)


def _rot_matrix(heads):
    half = ROT_DIM // 2
    r = np.zeros((HEAD_DIM, HEAD_DIM), np.float32)
    for j in range(half):
        r[j + half, j] = -1.0
        r[j, j + half] = 1.0
    return np.kron(np.eye(heads, dtype=np.float32), r)


def _norm_rope_t(xt, gain, cos, sin):
    half = ROT_DIM // 2
    xn = xt * lax.rsqrt(jnp.mean(xt * xt, axis=0, keepdims=True) + EPS) * gain
    x1, x2 = xn[:half], xn[half:ROT_DIM]
    return jnp.concatenate([x1 * cos - x2 * sin, x2 * cos + x1 * sin, xn[ROT_DIM:]], axis=0)


def _in_proj_kernel(x_ref, g_ref, w_ref, cos_ref, sin_ref, gq_ref, gk_ref, perm_ref,
                    o_ref, qo_ref, kso_ref, vso_ref, kwo_ref, vwo_ref, xk_ref, xv_ref, *, tiles_per_seq):
    tb = x_ref.shape[0]
    xb = _rms(x_ref[...], g_ref[...]).astype(BF16)
    hg_cols = COL_NQ * LANES
    qp = _dot(xb, w_ref[:, hg_cols:hg_cols + NSA_WIDTH])
    o_ref[:, hg_cols:hg_cols + NSA_WIDTH] = qp.astype(o_ref.dtype)
    kv = _dot(xb, w_ref[:, COL_KC * LANES:])
    o_ref[:, COL_KC * LANES:] = kv.astype(o_ref.dtype)
    for c in range(0, hg_cols, hg_cols // 2):
        o_ref[:, c:c + hg_cols // 2] = _dot(xb, w_ref[:, c:c + hg_cols // 2]).astype(o_ref.dtype)
    part_t = lambda col: kv[:, (col - COL_KC) * LANES:(col - COL_KC + 1) * LANES].T

    groups = tb // CMP_STRIDE
    banded = _dot(perm_ref[...], kv[:, :2 * LANES].astype(BF16))
    bands = [banded[l * groups:(l + 1) * groups] for l in range(CMP_STRIDE)]
    xk_ref[0] = jnp.concatenate([r[:, :LANES] for r in bands], axis=1).astype(xk_ref.dtype)
    xv_ref[0] = jnp.concatenate([r[:, LANES:] for r in bands], axis=1).astype(xv_ref.dtype)

    cos, sin = cos_ref[...], sin_ref[...]
    qt = qp.T
    for h in range(NSA_HEADS):
        qo_ref[0, h] = _norm_rope_t(qt[h * HEAD_DIM:(h + 1) * HEAD_DIM], gq_ref[...], cos, sin).astype(qo_ref.dtype)

    t0 = (pl.program_id(0) % tiles_per_seq) * tb
    block = (t0 + lax.broadcasted_iota(jnp.int32, (HEAD_DIM, tb), 1)) // SEL_BLOCK
    row_i = lax.broadcasted_iota(jnp.int32, (HEAD_DIM, tb), 0)
    tails = (jnp.where(row_i == block, 1.0, 0.0), jnp.where(row_i == 0, 1.0, 0.0))
    row_ones = jnp.ones((V_ROWS - HEAD_DIM, tb), F32)
    for j, (kcol, vcol, ko_ref, vo_ref) in enumerate(((COL_KS, COL_VS, kso_ref, vso_ref),
                                                      (COL_KW, COL_VW, kwo_ref, vwo_ref))):
        kt, vt = part_t(kcol), part_t(vcol)
        for g in range(NSA_KV):
            kh = _norm_rope_t(kt[g * HEAD_DIM:(g + 1) * HEAD_DIM], gk_ref[j], cos, sin)
            ko_ref[0, g] = jnp.concatenate([kh, tails[j]], axis=0).T.astype(ko_ref.dtype)
        for g in range(NSA_KV):
            vo_ref[0, g] = jnp.concatenate([vt[g * HEAD_DIM:(g + 1) * HEAD_DIM], row_ones], axis=0).astype(vo_ref.dtype)


def _in_proj(x2, g, w_pad, q_norm_g, k_norm_g, batch, seq):
    n = x2.shape[0]
    nt = seq // ROW_TILE
    assert seq // SEL_BLOCK <= HEAD_DIM
    cos, sin = _rope_tables(np.arange(seq))
    half = ROT_DIM // 2
    cos_t, sin_t = jnp.asarray(cos[:, :half].T.copy()), jnp.asarray(sin[:, :half].T.copy())
    lanes = lambda gain: jnp.tile(gain[..., None], (1,) * gain.ndim + (ROW_TILE,))
    gq = lanes(q_norm_g * (HEAD_DIM ** -0.5 * LOG2E))
    gk = lanes(k_norm_g[1:3])
    row = np.arange(ROW_TILE)
    perm = (np.arange(ROW_TILE)[None, :] == ((row % (ROW_TILE // CMP_STRIDE)) * CMP_STRIDE
                                             + row // (ROW_TILE // CMP_STRIDE))[:, None]).astype(np.float32)
    const = lambda i: (0, 0)
    rows_major = lambda d: pl.BlockSpec((1, NSA_KV, ROW_TILE, d), lambda i: (i // nt, 0, i % nt, 0))
    cols_major = lambda heads, d: pl.BlockSpec((1, heads, d, ROW_TILE), lambda i: (i // nt, 0, 0, i % nt))
    sds = lambda *shape: jax.ShapeDtypeStruct((batch,) + shape, BF16)
    grouped = pl.BlockSpec((1, ROW_TILE // CMP_STRIDE, CMP_STRIDE * LANES), lambda i: (i // nt, i % nt, 0))
    return pl.pallas_call(
        functools.partial(_in_proj_kernel, tiles_per_seq=nt),
        grid=(n // ROW_TILE,),
        in_specs=[pl.BlockSpec((ROW_TILE, D_MODEL), lambda i: (i, 0)),
                  pl.BlockSpec((1, D_MODEL), const),
                  pl.BlockSpec((D_MODEL, IN_PAD), const),
                  pl.BlockSpec((half, ROW_TILE), lambda i: (0, i % nt)),
                  pl.BlockSpec((half, ROW_TILE), lambda i: (0, i % nt)),
                  pl.BlockSpec((HEAD_DIM, ROW_TILE), const),
                  pl.BlockSpec((2, HEAD_DIM, ROW_TILE), lambda i: (0, 0, 0)),
                  pl.BlockSpec((ROW_TILE, ROW_TILE), const)],
        out_specs=[pl.BlockSpec((ROW_TILE, IN_PAD), lambda i: (i, 0)),
                   cols_major(NSA_HEADS, HEAD_DIM), rows_major(2 * HEAD_DIM), cols_major(NSA_KV, V_ROWS),
                   rows_major(2 * HEAD_DIM), cols_major(NSA_KV, V_ROWS), grouped, grouped],
        out_shape=[jax.ShapeDtypeStruct((n, IN_PAD), BF16),
                   sds(NSA_HEADS, HEAD_DIM, seq), sds(NSA_KV, seq, 2 * HEAD_DIM), sds(NSA_KV, V_ROWS, seq),
                   sds(NSA_KV, seq, 2 * HEAD_DIM), sds(NSA_KV, V_ROWS, seq),
                   sds(seq // CMP_STRIDE, CMP_STRIDE * LANES), sds(seq // CMP_STRIDE, CMP_STRIDE * LANES)],
        compiler_params=pltpu.CompilerParams(dimension_semantics=("parallel",),
                                             vmem_limit_bytes=VMEM_LIMIT),
        name="in_proj",
    )(x2, g, w_pad, cos_t, sin_t, gq, gk, jnp.asarray(perm, BF16))


def _compress_kernel(xk_ref, xv_ref, pea_ref, peb_ref, w1a_ref, w1b_ref, w2_ref, gk_ref, cos_ref, sin_ref,
                     rot_ref, kc_ref, vc_ref, shift_scr):
    nc = xk_ref.shape[1]
    for j, x_ref in enumerate((xk_ref, xv_ref)):
        x = x_ref[0].astype(F32)
        u = _dot((x + pea_ref[j:j + 1, :]).astype(BF16), w1a_ref[j])
        v = _dot((x + peb_ref[j:j + 1, :]).astype(BF16), w1b_ref[j])
        shift_scr[0:nc, :] = v
        shift_scr[nc:nc + HALO, :] = jnp.zeros((HALO, v.shape[1]), F32)
        hid = _silu(u + shift_scr[1:nc + 1, :])
        out = _dot(hid.astype(BF16), w2_ref[j])
        if j == 0:
            for g in range(NSA_KV):
                og = (out if g == 0 else pltpu.roll(out, HEAD_DIM, 1))[:, :HEAD_DIM]
                xn = _rms(og, gk_ref[...])
                kc_ref[0, g] = (xn * cos_ref[...] + _dot(xn.astype(BF16), rot_ref[...]) * sin_ref[...]).astype(kc_ref.dtype)
        else:
            vc_ref[0] = out.T.reshape(NSA_KV, HEAD_DIM, nc).astype(vc_ref.dtype)


def _compress(xk, xv, cmp_pe, cmp_w1, cmp_w2, gk0, batch, seq):
    nc = seq // CMP_STRIDE
    hidden = cmp_w1.shape[2]
    cos, sin = (jnp.asarray(t) for t in _rope_tables(np.arange(nc) * CMP_STRIDE + CMP_LEN - 1))
    pe = jnp.tile(cmp_pe[:, :, None, :], (1, 1, NSA_KV, 1))
    pea = pe[:, :CMP_STRIDE].reshape(2, -1)
    peb = pe[:, CMP_STRIDE:].reshape(2, -1)
    eye = jnp.eye(NSA_KV, dtype=cmp_w1.dtype)
    w1 = jnp.einsum('jldh,gk->jlgdkh', cmp_w1.reshape(2, CMP_LEN, HEAD_DIM, hidden), eye)
    w1 = w1.reshape(2, CMP_LEN, NSA_KV * HEAD_DIM, NSA_KV * hidden).astype(BF16)
    w1a = w1[:, :CMP_STRIDE].reshape(2, -1, NSA_KV * hidden)
    w1b = w1[:, CMP_STRIDE:].reshape(2, -1, NSA_KV * hidden)
    w2 = jnp.einsum('jhd,gk->jghkd', cmp_w2, eye).reshape(2, NSA_KV * hidden, NSA_KV * HEAD_DIM).astype(BF16)
    blk = pl.BlockSpec((1, nc, CMP_STRIDE * LANES), lambda b: (b, 0, 0))
    const2 = lambda b: (0, 0)
    const3 = lambda b: (0, 0, 0)
    return pl.pallas_call(
        _compress_kernel,
        grid=(batch,),
        in_specs=[blk, blk,
                  pl.BlockSpec(pea.shape, const2), pl.BlockSpec(peb.shape, const2),
                  pl.BlockSpec(w1a.shape, const3), pl.BlockSpec(w1b.shape, const3), pl.BlockSpec(w2.shape, const3),
                  pl.BlockSpec((1, HEAD_DIM), const2),
                  pl.BlockSpec((nc, HEAD_DIM), const2), pl.BlockSpec((nc, HEAD_DIM), const2),
                  pl.BlockSpec((HEAD_DIM, HEAD_DIM), const2)],
        out_specs=[pl.BlockSpec((1, NSA_KV, nc, HEAD_DIM), lambda b: (b, 0, 0, 0)),
                   pl.BlockSpec((1, NSA_KV, HEAD_DIM, nc), lambda b: (b, 0, 0, 0))],
        out_shape=[jax.ShapeDtypeStruct((batch, NSA_KV, nc, HEAD_DIM), BF16),
                   jax.ShapeDtypeStruct((batch, NSA_KV, HEAD_DIM, nc), BF16)],
        scratch_shapes=[pltpu.VMEM((nc + HALO, NSA_KV * hidden), F32)],
        compiler_params=pltpu.CompilerParams(dimension_semantics=("parallel",),
                                             vmem_limit_bytes=VMEM_LIMIT),
        name="nsa_compress",
    )(xk, xv, pea, peb, w1a, w1b, w2, gk0[None, :], cos, sin, jnp.asarray(_rot_matrix(1), BF16))


NSA_TQ = 256


def _overlap_t(seq):
    ns, nc = seq // SEL_BLOCK, seq // CMP_STRIDE
    cs = np.arange(nc)[None, :] * CMP_STRIDE
    ss = np.arange(ns)[:, None] * SEL_BLOCK
    ov = np.clip(np.minimum(cs + CMP_LEN, ss + SEL_BLOCK) - np.maximum(cs, ss), 0, None)
    return (ov / CMP_LEN).astype(np.float32)


def _cmpsel_kernel(offset_ref, q_ref, kc_ref, vc_ref, ovt_ref, ocmp_ref, bias_ref, *, top):
    tq, nc = q_ref.shape[3], kc_ref.shape[2]
    ns = ovt_ref.shape[0]
    t0 = pl.program_id(1) * tq
    n_i = lax.broadcasted_iota(jnp.int32, (ns, tq), 0)
    t_i = t0 + lax.broadcasted_iota(jnp.int32, (ns, tq), 1)
    cur = t_i // SEL_BLOCK
    forced = jnp.where(n_i == 0, FORCE, jnp.where(n_i == cur, FORCE, jnp.where(n_i == cur - 1, FORCE, 0.0)))
    visible = n_i * SEL_BLOCK <= t_i
    sub = lax.broadcasted_iota(jnp.int32, (SUBLANES, tq), 0)

    def attend(g, quarters):
        ncv, nsv = quarters * nc // 4, quarters * ns // 4
        safe = max((quarters - 1) * nc // 4 - SUBLANES, 0)
        c_i = safe + lax.broadcasted_iota(jnp.int32, (ncv - safe, NSA_GROUP * tq), 0)
        q_i = lax.broadcasted_iota(jnp.int32, (ncv - safe, NSA_GROUP * tq), 1) & (tq - 1)
        cmask = c_i * CMP_STRIDE + (CMP_LEN - 1) <= t0 + q_i
        qg = jnp.concatenate([q_ref[0, g * NSA_GROUP + h] for h in range(NSA_GROUP)], axis=1)
        s = _dot(kc_ref[0, g, 0:ncv, :], qg)
        s_hi = jnp.where(cmask, s[safe:], NEG)
        m = jnp.max(s_hi, axis=0, keepdims=True)
        if safe:
            m = jnp.maximum(m, jnp.max(s[:safe], axis=0, keepdims=True))
        e_hi = jnp.exp2(s_hi - m)
        total = jnp.sum(e_hi, axis=0, keepdims=True)
        if safe:
            e_lo = jnp.exp2(s[:safe] - m)
            total += jnp.sum(e_lo, axis=0, keepdims=True)
        inv = 1.0 / total
        p = jnp.where(cmask, e_hi * inv, 0.0)
        if safe:
            p = jnp.concatenate([e_lo * inv, p], axis=0)
        o = _dot(vc_ref[0, g, :, 0:ncv], p.astype(BF16)).astype(ocmp_ref.dtype)
        psum = p[:, :tq]
        for h in range(NSA_GROUP):
            ocmp_ref[0, g * NSA_GROUP + h] = o[:, h * tq:(h + 1) * tq]
            if h:
                psum += p[:, h * tq:(h + 1) * tq]
        ovt = ovt_ref[:, 0:ncv]
        hi, lo = _split(psum)
        val = jnp.where(visible, _dot(ovt, hi) + _dot(ovt, lo) + forced, NEG)
        groups = []
        for r in range(ns // SUBLANES):
            if r * SUBLANES >= nsv:
                groups.append(jnp.zeros((SUBLANES, tq), F32))
                continue
            v_r = val[r * SUBLANES:(r + 1) * SUBLANES]
            c = jnp.zeros((SUBLANES, tq), F32)
            for m in range(nsv):
                row = val[m:m + 1, :]
                if m // SUBLANES > r:
                    hit = row > v_r
                elif m // SUBLANES < r:
                    hit = row >= v_r
                else:
                    hit = jnp.where(sub > m % SUBLANES, jnp.where(row >= v_r, 1.0, 0.0),
                                    jnp.where(row > v_r, 1.0, 0.0)) > 0.5
                c = jnp.where(hit, c + 1.0, c)
            groups.append(c)
        bias = jnp.where(jnp.concatenate(groups, axis=0) < top, offset_ref[0], MASK_BIAS)
        if ns < HEAD_DIM:
            bias = jnp.concatenate([bias, jnp.zeros((HEAD_DIM - ns, tq), F32)], axis=0)
        bias_ref[0, g] = bias.astype(bias_ref.dtype)

    assert ns % (4 * SUBLANES) == 0 and nc % 4 == 0
    last_quarter = (t0 + tq - 1) // (nc // 4 * CMP_STRIDE)
    for g in range(NSA_KV):
        for v in range(4):
            @pl.when(last_quarter == v)
            def _():
                attend(g, v + 1)


def _cmpsel(offset, qt, kc, vct, batch, seq):
    nt = seq // NSA_TQ
    nc, ns = seq // CMP_STRIDE, seq // SEL_BLOCK
    qblk = pl.BlockSpec((1, NSA_HEADS, HEAD_DIM, NSA_TQ), lambda b, t: (b, 0, 0, t))
    return pl.pallas_call(
        functools.partial(_cmpsel_kernel, top=min(SEL_TOPK, ns)),
        grid=(batch, nt),
        in_specs=[pl.BlockSpec(memory_space=pltpu.SMEM), qblk,
                  pl.BlockSpec((1, NSA_KV, nc, HEAD_DIM), lambda b, t: (b, 0, 0, 0)),
                  pl.BlockSpec((1, NSA_KV, HEAD_DIM, nc), lambda b, t: (b, 0, 0, 0)),
                  pl.BlockSpec((ns, nc), lambda b, t: (0, 0))],
        out_specs=[qblk, pl.BlockSpec((1, NSA_KV, HEAD_DIM, NSA_TQ), lambda b, t: (b, 0, 0, t))],
        out_shape=[jax.ShapeDtypeStruct((batch, NSA_HEADS, HEAD_DIM, seq), BF16),
                   jax.ShapeDtypeStruct((batch, NSA_KV, HEAD_DIM, seq), BF16)],
        compiler_params=pltpu.CompilerParams(dimension_semantics=("parallel", "parallel"),
                                             vmem_limit_bytes=VMEM_LIMIT),
        name="nsa_cmpsel",
    )(offset, qt, kc, vct, jnp.asarray(_overlap_t(seq), BF16))


SAFE_OFFSET = 60.0


def _attn_kernel(q_ref, ks_ref, vs_ref, kw_ref, vw_ref, bias_ref, wrow_ref, ocmp_ref, gate_ref, gout_ref,
                 o_ref, acc_scr, *scratch, bounded):
    tq = q_ref.shape[3]
    i = pl.program_id(1)
    t0 = i * tq
    q_sel, q_win = [], []
    for g in range(NSA_KV):
        q_grp = jnp.concatenate([q_ref[0, g * NSA_GROUP + h] for h in range(NSA_GROUP)], axis=1)
        q_sel.append(jnp.concatenate([q_grp, jnp.concatenate([bias_ref[0, g]] * NSA_GROUP, axis=1)], axis=0))
        q_win.append(jnp.concatenate([q_grp, wrow_ref[...]], axis=0))

    def per_head(s, keep):
        return jnp.concatenate([jnp.where(keep, s[:, h * tq:(h + 1) * tq], NEG) for h in range(NSA_GROUP)], axis=1)

    def causal(s):
        key_i = lax.broadcasted_iota(jnp.int32, (tq, tq), 0)
        return per_head(s, key_i <= lax.broadcasted_iota(jnp.int32, (tq, tq), 1))

    def tile(j):
        return pl.ds(pl.multiple_of(j * tq, tq), tq)

    acc_scr[...] = jnp.zeros_like(acc_scr)
    span = WINDOW + tq
    w0 = pl.multiple_of(jnp.maximum(t0 + tq - span, 0), tq)
    dist = (t0 - w0) + (lax.broadcasted_iota(jnp.int32, (span, tq), 1)
                        - lax.broadcasted_iota(jnp.int32, (span, tq), 0))
    in_window = jnp.abs(2 * dist - (WINDOW - 1)) < WINDOW
    win_scores = lambda g: per_head(_dot(kw_ref[0, g, pl.ds(w0, span), :], q_win[g]), in_window)

    def finish(win):
        gate = jax.nn.sigmoid(gate_ref[...].astype(F32).T)
        gout = gout_ref[...]
        rows = []
        for hh in range(NSA_HEADS):
            g, lanes = hh // NSA_GROUP, slice((hh % NSA_GROUP) * tq, (hh % NSA_GROUP + 1) * tq)
            a_s, a_w = acc_scr[g][:, lanes], win[g][:, lanes]
            o = (gate[hh:hh + 1] * ocmp_ref[0, hh].astype(F32)
                 + gate[NSA_HEADS + hh:NSA_HEADS + hh + 1] * (a_s[:HEAD_DIM] * (1.0 / a_s[HEAD_DIM:HEAD_DIM + 1]))
                 + gate[2 * NSA_HEADS + hh:2 * NSA_HEADS + hh + 1] * (a_w[:HEAD_DIM] * (1.0 / a_w[HEAD_DIM:HEAD_DIM + 1])))
            rows.append(o * lax.rsqrt(jnp.mean(o * o, axis=0, keepdims=True) + EPS) * gout)
        o_ref[...] = jnp.concatenate(rows, axis=0).T.astype(o_ref.dtype)

    if bounded:
        p0_scr, p1_scr = scratch

        def probs(j, p_scr, diag):
            for g in range(NSA_KV):
                s = _dot(ks_ref[0, g, tile(j), :], q_sel[g])
                p_scr[g] = jnp.exp2(causal(s) if diag else s).astype(BF16)

        def accumulate(j, p_scr):
            for g in range(NSA_KV):
                acc_scr[g] += _dot(vs_ref[0, g, :, tile(j)], p_scr[g])

        probs(0, p0_scr, False)

        def body(jj, carry):
            probs(2 * jj + 1, p1_scr, False)
            accumulate(2 * jj, p0_scr)
            probs(2 * jj + 2, p0_scr, False)
            accumulate(2 * jj + 1, p1_scr)
            return carry

        lax.fori_loop(0, (i - 1) // 2, body, 0)

        def last(p_scr):
            probs(i, p_scr, True)
            s_win = [win_scores(g) for g in range(NSA_KV)]
            accumulate(i, p_scr)
            finish([_dot(vw_ref[0, g, :, pl.ds(w0, span)], jnp.exp2(s_win[g]).astype(BF16)) for g in range(NSA_KV)])

        @pl.when(i == 0)
        def _():
            last(p0_scr)

        @pl.when(i % 2 == 1)
        def _():
            accumulate(i - 1, p0_scr)
            last(p1_scr)

        @pl.when((i > 0) & (i % 2 == 0))
        def _():
            probs(i - 1, p1_scr, False)
            accumulate(i - 2, p0_scr)
            accumulate(i - 1, p1_scr)
            last(p0_scr)
    else:
        m_scr, s0_scr, s1_scr = scratch
        m_scr[...] = jnp.full(m_scr.shape, NEG, F32)

        def scores(j, s_scr):
            for g in range(NSA_KV):
                s_scr[g] = _dot(ks_ref[0, g, tile(j), :], q_sel[g])

        def consume(j, s_scr, diag):
            for g in range(NSA_KV):
                s = causal(s_scr[g]) if diag else s_scr[g]
                m_old = m_scr[g]
                m_new = jnp.maximum(m_old, jnp.max(s, axis=0, keepdims=True))
                p = jnp.exp2(s - m_new).astype(BF16)
                acc_scr[g] = jnp.exp2(m_old - m_new) * acc_scr[g] + _dot(vs_ref[0, g, :, tile(j)], p)
                m_scr[g] = m_new

        scores(0, s0_scr)

        def body(jj, carry):
            scores(2 * jj + 1, s1_scr)
            consume(2 * jj, s0_scr, False)
            scores(2 * jj + 2, s0_scr)
            consume(2 * jj + 1, s1_scr, False)
            return carry

        lax.fori_loop(0, i // 2, body, 0)

        @pl.when(i % 2 == 0)
        def _():
            consume(i, s0_scr, True)

        @pl.when(i % 2 == 1)
        def _():
            scores(i, s1_scr)
            consume(i - 1, s0_scr, False)
            consume(i, s1_scr, True)

        win = []
        for g in range(NSA_KV):
            s = win_scores(g)
            p = jnp.exp2(s - jnp.max(s, axis=0, keepdims=True)).astype(BF16)
            win.append(_dot(vw_ref[0, g, :, pl.ds(w0, span)], p))
        finish(win)


def _attn(qt, ks, vst, kw, vwt, bias, wrow, ocmp, proj, gout, *, bounded):
    batch, _, _, seq = qt.shape
    nt = seq // NSA_TQ
    assert WINDOW % NSA_TQ == 0 and seq >= WINDOW + NSA_TQ
    wide = NSA_GROUP * NSA_TQ
    rows_full = pl.BlockSpec((1, NSA_KV, seq, 2 * HEAD_DIM), lambda b, t: (b, 0, 0, 0))
    cols_full = pl.BlockSpec((1, NSA_KV, V_ROWS, seq), lambda b, t: (b, 0, 0, 0))
    qtile = lambda heads: pl.BlockSpec((1, heads, HEAD_DIM, NSA_TQ), lambda b, t: (b, 0, 0, t))
    if bounded:
        scratch = [pltpu.VMEM((NSA_KV, NSA_TQ, wide), BF16), pltpu.VMEM((NSA_KV, NSA_TQ, wide), BF16)]
    else:
        scratch = [pltpu.VMEM((NSA_KV, 1, wide), F32),
                   pltpu.VMEM((NSA_KV, NSA_TQ, wide), F32), pltpu.VMEM((NSA_KV, NSA_TQ, wide), F32)]
    return pl.pallas_call(
        functools.partial(_attn_kernel, bounded=bounded),
        grid=(batch, nt),
        in_specs=[qtile(NSA_HEADS), rows_full, cols_full, rows_full, cols_full, qtile(NSA_KV),
                  pl.BlockSpec((HEAD_DIM, wide), lambda b, t: (0, 0)),
                  qtile(NSA_HEADS),
                  pl.BlockSpec((NSA_TQ, LANES), lambda b, t: (b * nt + t, COL_GATE)),
                  pl.BlockSpec((HEAD_DIM, NSA_TQ), lambda b, t: (0, 0))],
        out_specs=pl.BlockSpec((NSA_TQ, NSA_WIDTH), lambda b, t: (b * nt + t, 0)),
        out_shape=jax.ShapeDtypeStruct((batch * seq, NSA_WIDTH), BF16),
        scratch_shapes=[pltpu.VMEM((NSA_KV, V_ROWS, wide), F32)] + scratch,
        compiler_params=pltpu.CompilerParams(dimension_semantics=("parallel", "parallel"),
                                             vmem_limit_bytes=VMEM_LIMIT),
        name="nsa_attn" if bounded else "nsa_attn_online",
    )(qt, ks, vst, kw, vwt, bias, wrow, ocmp, proj, gout)


def _nsa(proj, qt, ks, vst, kw, vwt, xk, xv, q_norm_g, k_norm_g, cmp_pe, cmp_w1, cmp_w2, out_norm_g, batch, seq):
    def offset(gk):
        bound = HEAD_DIM * (HEAD_DIM ** -0.5 * LOG2E) * jnp.max(jnp.abs(q_norm_g)) * jnp.max(jnp.abs(gk))
        return (1.02 * bound).astype(BF16).astype(F32)

    off_sel, off_win = offset(k_norm_g[1]), offset(k_norm_g[2])
    kc, vct = _compress(xk, xv, cmp_pe, cmp_w1, cmp_w2, k_norm_g[0], batch, seq)
    ocmp, bias = _cmpsel(-off_sel[None], qt, kc, vct, batch, seq)
    wrow = jnp.zeros((HEAD_DIM, NSA_GROUP * NSA_TQ), F32).at[0].set(-off_win).astype(BF16)
    gout = jnp.tile(out_norm_g[:, None], (1, NSA_TQ))
    return lax.cond(jnp.maximum(off_sel, off_win) <= SAFE_OFFSET,
                    functools.partial(_attn, bounded=True), functools.partial(_attn, bounded=False),
                    qt, ks, vst, kw, vwt, bias, wrow, ocmp, proj, gout)


def kernel(x, p, attn_norm_g, w_in, hg_lb_logits, hg_norm_g, nsa_q_norm_g, nsa_k_norm_g, cmp_pe,
           cmp_w1, cmp_w2, nsa_out_norm_g, w_out, ffn_norm_g, w_up, conv_w, conv_b, w_down,
           ple_gate_norm_g, w_ple_gate, w_ple, ple_norm_g):
    B, T, _ = x.shape
    n = B * T
    h = x.reshape(n, D_MODEL)
    for i in range(w_in.shape[0]):
        w_pad = jnp.pad(w_in[i], ((0, 0), (0, IN_PAD - IN_TOTAL))).astype(BF16)
        proj, *qkv = _in_proj(h, attn_norm_g[i][None, :], w_pad, nsa_q_norm_g[i], nsa_k_norm_g[i], B, T)
        o_hg = _hgrn(proj, hg_lb_logits.astype(F32), hg_norm_g[i][None, :], i, B, T)
        o_nsa = _nsa(proj, *qkv, nsa_q_norm_g[i], nsa_k_norm_g[i], cmp_pe[i], cmp_w1[i], cmp_w2[i],
                     nsa_out_norm_g[i], B, T)
        h = _post_mixer(h, o_hg, o_nsa, p[i].reshape(n, PLE_DIM), w_out[i].astype(BF16), ffn_norm_g[i][None, :],
                        w_up[i].astype(BF16), conv_w[i], conv_b[i][None, :], w_down[i].astype(BF16),
                        ple_gate_norm_g[i][None, :], w_ple_gate[i].astype(BF16), w_ple[i].astype(BF16),
                        ple_norm_g[i][None, :], T)
    return h.reshape(B, T, D_MODEL)
```

```python
import functools

import numpy as np
import jax
import jax.numpy as jnp
from jax import lax
from jax.experimental import pallas as pl
from jax.experimental.pallas import tpu as pltpu

F32 = jnp.float32
BF16 = jnp.bfloat16

D_MODEL = 1024
PLE_DIM = 256
EPS = 1e-6
NEG = -1e30
FORCE = 1e6

HG_HEADS = 4
HG_KDIM = 128
HG_VDIM = 128
HG_WIDTH = HG_HEADS * HG_VDIM
HG_CHUNK = 64

NSA_HEADS = 8
NSA_KV = 2
NSA_GROUP = NSA_HEADS // NSA_KV
HEAD_DIM = 64
NSA_WIDTH = NSA_HEADS * HEAD_DIM
KV_WIDTH = NSA_KV * HEAD_DIM
CMP_LEN = 32
CMP_STRIDE = 16
SEL_BLOCK = 64
SEL_TOPK = 16
WINDOW = 512
ROT_DIM = HEAD_DIM // 4
ROPE_THETA = 500000.0
D_FF = 2816
CONV_W = 3

IN_SIZES = (HG_HEADS * HG_KDIM, HG_HEADS * HG_KDIM, HG_WIDTH, HG_WIDTH, NSA_WIDTH,
            KV_WIDTH, KV_WIDTH, KV_WIDTH, KV_WIDTH, KV_WIDTH, KV_WIDTH, 3 * NSA_HEADS)
IN_TOTAL = sum(IN_SIZES)
LANES = 128
IN_PAD = -(-IN_TOTAL // LANES) * LANES
VMEM_LIMIT = 56 * 1024 * 1024

ROW_TILE = 512
FF_CHUNK = 256
SUBLANES = 8
HALO = SUBLANES
FFN_HALO = 2 * SUBLANES


def _rms(x, g):
    return x * lax.rsqrt(jnp.mean(x * x, axis=-1, keepdims=True) + EPS) * g


def _dot(a, b):
    return jnp.dot(a, b, preferred_element_type=F32)


def _dot_nt(a, b):
    return lax.dot_general(a, b, (((1,), (1,)), ((), ())), preferred_element_type=F32)


def _dot_tn(a, b):
    return lax.dot_general(a, b, (((0,), (0,)), ((), ())), preferred_element_type=F32)


def _split(x):
    hi = x.astype(BF16)
    return hi, (x - hi.astype(F32)).astype(BF16)


def _silu(x):
    hx = 0.5 * x
    return hx + hx * jnp.tanh(hx)


def _post_mixer_kernel(x_ref, xh_ref, a_ref, ah_ref, b_ref, bh_ref, p_ref, wo_ref, gf_ref, wup_ref, cw_ref,
                       cb_ref, wdn_ref, gg_ref, wg_ref, wp_ref, gp_ref, o_ref,
                       h1_scr, hn_scr, ug_scr, uu_scr, act_scr, *, tiles_per_seq):
    rows = x_ref.shape[0]
    halo, body = slice(0, FFN_HALO), slice(FFN_HALO, FFN_HALO + rows)
    hn_scr[halo, :HG_WIDTH], hn_scr[halo, HG_WIDTH:] = ah_ref[...], bh_ref[...]
    hn_scr[body, :HG_WIDTH], hn_scr[body, HG_WIDTH:] = a_ref[...], b_ref[...]
    mixed = _dot(hn_scr[...], wo_ref[...])
    h1_scr[halo, :] = xh_ref[...] + mixed[halo]
    h1_scr[body, :] = x_ref[...] + mixed[body]
    first = (pl.program_id(0) % tiles_per_seq) == 0
    hn_scr[halo, :] = jnp.where(first, 0.0, _rms(h1_scr[halo, :], gf_ref[...])).astype(BF16)
    hn_scr[body, :] = _rms(h1_scr[body, :], gf_ref[...]).astype(BF16)
    for c in range(0, D_FF, FF_CHUNK):
        conv = []
        for scr, off in ((ug_scr, c), (uu_scr, D_FF + c)):
            scr[...] = _dot(hn_scr[...], wup_ref[:, off:off + FF_CHUNK])
            cw = cw_ref[:, off:off + FF_CHUNK]
            conv.append(scr[FFN_HALO - 2:FFN_HALO - 2 + rows, :] * cw[0:1, :]
                        + scr[FFN_HALO - 1:FFN_HALO - 1 + rows, :] * cw[1:2, :]
                        + scr[body, :] * cw[2:3, :]
                        + cb_ref[:, off:off + FF_CHUNK])
        gate, up = conv
        act_scr[:, c:c + FF_CHUNK] = (_silu(gate) * up).astype(BF16)
    o_ref[...] = h1_scr[body, :] + _dot(act_scr[...], wdn_ref[...])
    hn_scr[body, :] = _rms(o_ref[...], gg_ref[...]).astype(BF16)
    e = _rms(_dot(p_ref[...].astype(BF16), wp_ref[...]), gp_ref[...])
    gate = jax.nn.sigmoid(_dot(hn_scr[body, :], wg_ref[...]))
    o_ref[...] = o_ref[...] + gate * e


def _post_mixer(x2, o_hg, o_nsa, p2, w_out, gf, w_up, conv_w, conv_b, w_down, gg, w_gate, w_ple, gp, seq):
    n = x2.shape[0]
    assert HG_WIDTH + NSA_WIDTH == D_MODEL
    const = lambda i: (0, 0)
    tile = lambda width: pl.BlockSpec((ROW_TILE, width), lambda i: (i, 0))
    halo_blocks = ROW_TILE // FFN_HALO
    halo = lambda width: pl.BlockSpec((FFN_HALO, width), lambda i: (jnp.maximum(i * halo_blocks - 1, 0), 0))
    weight = lambda r, c: pl.BlockSpec((r, c), const, pipeline_mode=pl.Buffered(1))
    padded = FFN_HALO + ROW_TILE
    return pl.pallas_call(
        functools.partial(_post_mixer_kernel, tiles_per_seq=seq // ROW_TILE),
        grid=(n // ROW_TILE,),
        in_specs=[tile(D_MODEL), halo(D_MODEL), tile(HG_WIDTH), halo(HG_WIDTH), tile(NSA_WIDTH), halo(NSA_WIDTH),
                  tile(PLE_DIM),
                  weight(D_MODEL, D_MODEL),
                  pl.BlockSpec((1, D_MODEL), const),
                  weight(D_MODEL, 2 * D_FF),
                  pl.BlockSpec((CONV_W, 2 * D_FF), const),
                  pl.BlockSpec((1, 2 * D_FF), const),
                  weight(D_FF, D_MODEL),
                  pl.BlockSpec((1, D_MODEL), const),
                  weight(D_MODEL, D_MODEL),
                  weight(PLE_DIM, D_MODEL),
                  pl.BlockSpec((1, D_MODEL), const)],
        out_specs=tile(D_MODEL),
        out_shape=jax.ShapeDtypeStruct((n, D_MODEL), F32),
        scratch_shapes=[pltpu.VMEM((padded, D_MODEL), F32),
                        pltpu.VMEM((padded, D_MODEL), BF16),
                        pltpu.VMEM((padded, FF_CHUNK), F32),
                        pltpu.VMEM((padded, FF_CHUNK), F32),
                        pltpu.VMEM((ROW_TILE, D_FF), BF16)],
        compiler_params=pltpu.CompilerParams(dimension_semantics=("parallel",),
                                             vmem_limit_bytes=VMEM_LIMIT),
        name="post_mixer",
    )(x2, x2, o_hg, o_hg, o_nsa, o_nsa, p2, w_out, gf, w_up, conv_w, conv_b, w_down, gg, w_gate, w_ple, gp)


HG_LEVELS = (32, 16, 8, 4, 2, 1)
HG_TB = 1024


def _hgrn_level_matrices():
    c = HG_CHUNK
    r = np.arange(c)[:, None]
    u = np.arange(c)[None, :]
    rows = []
    for m in HG_LEVELS:
        r0 = (r // m) * m
        upper = (r & m) != 0
        rows.append(np.where(upper, (u >= r0) & (u <= r), (u >= r + 1) & (u <= r0 + m - 1)))
    rows.append(u <= r)
    rows.append(u > r)
    return np.concatenate(rows, 0).astype(np.float32)


def _hgrn_pair_masks():
    c = HG_CHUNK
    t = np.arange(c)[:, None]
    s = np.arange(c)[None, :]
    masks = [((t // (2 * m)) == (s // (2 * m))) & ((t & m) != 0) & ((s & m) == 0) for m in HG_LEVELS]
    masks.append(t == s)
    return np.stack(masks).astype(np.float32)


def _hgrn_kernel(q_ref, f_ref, v_ref, g_ref, lbl_ref, ng_ref, mall_ref, msk_ref, o_ref, st_ref,
                 *, layer, n_chunks):
    @pl.when(pl.program_id(1) == 0)
    def _():
        st_ref[...] = jnp.zeros_like(st_ref)

    lbl = lbl_ref[...]
    e = jnp.exp(lbl - jnp.max(lbl, axis=0, keepdims=True))
    sm = e / jnp.sum(e, axis=0, keepdims=True)
    lb = jnp.sum(sm[:layer + 1], axis=0, keepdims=True)
    mall = mall_ref[...]
    ng = ng_ref[...]
    c = HG_CHUNK
    nl = len(HG_LEVELS)

    intra, q_decayed, carry_decay, carry_add = {}, {}, {}, {}
    for ci in range(n_chunks):
        r0 = ci * c
        fg = lb + (1.0 - lb) * jax.nn.sigmoid(f_ref[r0:r0 + c, :].astype(F32))
        kk = 1.0 - fg
        ee = jnp.exp(_dot(mall, jnp.concatenate(_split(jnp.log(fg)), axis=0)))
        qf = _silu(q_ref[r0:r0 + c, :].astype(F32)) * HG_KDIM ** -0.5
        for h in range(HG_HEADS):
            sl = slice(h * HG_KDIM, (h + 1) * HG_KDIM)
            qh, kh = qf[:, sl], kk[:, sl]
            a = msk_ref[nl] * _dot_nt(qh.astype(BF16), kh.astype(BF16))
            for j in range(nl):
                ej = ee[j * c:(j + 1) * c, sl]
                a += msk_ref[j] * _dot_nt((qh * ej).astype(BF16), (kh * ej).astype(BF16))
            vh = v_ref[r0:r0 + c, sl]
            eb = ee[nl * c:(nl + 1) * c, sl]
            intra[ci, h] = _dot(a.astype(BF16), vh)
            q_decayed[ci, h] = (qh * eb).astype(BF16)
            carry_decay[ci, h] = eb[c - 1:c, :]
            carry_add[ci, h] = _dot_tn(vh, (kh * ee[(nl + 1) * c:(nl + 2) * c, sl]).astype(BF16))

    for h in range(HG_HEADS):
        sl = slice(h * HG_KDIM, (h + 1) * HG_KDIM)
        st = st_ref[h]
        states = []
        for ci in range(n_chunks):
            states.append(st.astype(BF16))
            st = carry_decay[ci, h] * st + carry_add[ci, h]
        st_ref[h] = st
        for ci in range(n_chunks):
            r0 = ci * c
            o = intra[ci, h] + _dot_nt(q_decayed[ci, h], states[ci])
            on = o * lax.rsqrt(jnp.mean(o * o, axis=-1, keepdims=True) + EPS) * ng
            o_ref[r0:r0 + c, sl] = (on * _silu(g_ref[r0:r0 + c, sl].astype(F32))).astype(o_ref.dtype)


def _hgrn(proj, lb_logits, norm_g, layer, batch, seq):
    nt = seq // HG_TB
    w = HG_HEADS * HG_KDIM
    col = lambda k: (lambda b, t: (b * nt + t, k))
    const2 = lambda b, t: (0, 0)
    return pl.pallas_call(
        functools.partial(_hgrn_kernel, layer=layer, n_chunks=HG_TB // HG_CHUNK),
        grid=(batch, nt),
        in_specs=[pl.BlockSpec((HG_TB, w), col(0)), pl.BlockSpec((HG_TB, w), col(1)),
                  pl.BlockSpec((HG_TB, w), col(2)), pl.BlockSpec((HG_TB, w), col(3)),
                  pl.BlockSpec(lb_logits.shape, const2),
                  pl.BlockSpec((1, HG_VDIM), const2),
                  pl.BlockSpec(((len(HG_LEVELS) + 2) * HG_CHUNK, 2 * HG_CHUNK), const2),
                  pl.BlockSpec((len(HG_LEVELS) + 1, HG_CHUNK, HG_CHUNK), lambda b, t: (0, 0, 0))],
        out_specs=pl.BlockSpec((HG_TB, w), lambda b, t: (b * nt + t, 0)),
        out_shape=jax.ShapeDtypeStruct((batch * seq, HG_WIDTH), BF16),
        scratch_shapes=[pltpu.VMEM((HG_HEADS, HG_VDIM, HG_KDIM), F32)],
        compiler_params=pltpu.CompilerParams(dimension_semantics=("parallel", "arbitrary"),
                                             vmem_limit_bytes=VMEM_LIMIT),
        name="hgrn2",
    )(proj, proj, proj, proj, lb_logits, norm_g,
      jnp.asarray(np.tile(_hgrn_level_matrices(), (1, 2)), BF16), jnp.asarray(_hgrn_pair_masks(), F32))


COL_NQ = 2048 // LANES
COL_KC, COL_VC, COL_KS, COL_VS, COL_KW, COL_VW, COL_GATE = (COL_NQ + 4 + k for k in range(7))
LOG2E = 1.4426950408889634
MASK_BIAS = -1e30
V_ROWS = HEAD_DIM + 2 * SUBLANES


def _rope_tables(pos):
    half = ROT_DIM // 2
    inv = np.float32(ROPE_THETA) ** (-np.arange(half, dtype=np.float32) / np.float32(half))
    ang = pos.astype(np.float32)[:, None] * inv[None, :]
    n = pos.shape[0]
    pad = HEAD_DIM - ROT_DIM
    cos = np.concatenate([np.cos(ang), np.cos(ang), np.ones((n, pad), np.float32)], axis=1)
    sin = np.concatenate([np.sin(ang), np.sin(ang), np.zeros((n, pad), np.float32)], axis=1)
    return cos.astype(np.float32), sin.astype(np.float32)


def _rot_matrix(heads):
    half = ROT_DIM // 2
    r = np.zeros((HEAD_DIM, HEAD_DIM), np.float32)
    for j in range(half):
        r[j + half, j] = -1.0
        r[j, j + half] = 1.0
    return np.kron(np.eye(heads, dtype=np.float32), r)


def _norm_rope_t(xt, gain, cos, sin):
    half = ROT_DIM // 2
    xn = xt * lax.rsqrt(jnp.mean(xt * xt, axis=0, keepdims=True) + EPS) * gain
    x1, x2 = xn[:half], xn[half:ROT_DIM]
    return jnp.concatenate([x1 * cos - x2 * sin, x2 * cos + x1 * sin, xn[ROT_DIM:]], axis=0)


def _in_proj_kernel(x_ref, g_ref, w_ref, cos_ref, sin_ref, gq_ref, gk_ref, perm_ref,
                    o_ref, qo_ref, kso_ref, vso_ref, kwo_ref, vwo_ref, xk_ref, xv_ref, *, tiles_per_seq):
    tb = x_ref.shape[0]
    xb = _rms(x_ref[...], g_ref[...]).astype(BF16)
    hg_cols = COL_NQ * LANES
    qp = _dot(xb, w_ref[:, hg_cols:hg_cols + NSA_WIDTH])
    o_ref[:, hg_cols:hg_cols + NSA_WIDTH] = qp.astype(o_ref.dtype)
    kv = _dot(xb, w_ref[:, COL_KC * LANES:])
    o_ref[:, COL_KC * LANES:] = kv.astype(o_ref.dtype)
    for c in range(0, hg_cols, hg_cols // 2):
        o_ref[:, c:c + hg_cols // 2] = _dot(xb, w_ref[:, c:c + hg_cols // 2]).astype(o_ref.dtype)
    part_t = lambda col: kv[:, (col - COL_KC) * LANES:(col - COL_KC + 1) * LANES].T

    groups = tb // CMP_STRIDE
    banded = _dot(perm_ref[...], kv[:, :2 * LANES].astype(BF16))
    bands = [banded[l * groups:(l + 1) * groups] for l in range(CMP_STRIDE)]
    xk_ref[0] = jnp.concatenate([r[:, :LANES] for r in bands], axis=1).astype(xk_ref.dtype)
    xv_ref[0] = jnp.concatenate([r[:, LANES:] for r in bands], axis=1).astype(xv_ref.dtype)

    cos, sin = cos_ref[...], sin_ref[...]
    qt = qp.T
    for h in range(NSA_HEADS):
        qo_ref[0, h] = _norm_rope_t(qt[h * HEAD_DIM:(h + 1) * HEAD_DIM], gq_ref[...], cos, sin).astype(qo_ref.dtype)

    t0 = (pl.program_id(0) % tiles_per_seq) * tb
    block = (t0 + lax.broadcasted_iota(jnp.int32, (HEAD_DIM, tb), 1)) // SEL_BLOCK
    row_i = lax.broadcasted_iota(jnp.int32, (HEAD_DIM, tb), 0)
    tails = (jnp.where(row_i == block, 1.0, 0.0), jnp.where(row_i == 0, 1.0, 0.0))
    row_ones = jnp.ones((V_ROWS - HEAD_DIM, tb), F32)
    for j, (kcol, vcol, ko_ref, vo_ref) in enumerate(((COL_KS, COL_VS, kso_ref, vso_ref),
                                                      (COL_KW, COL_VW, kwo_ref, vwo_ref))):
        kt, vt = part_t(kcol), part_t(vcol)
        for g in range(NSA_KV):
            kh = _norm_rope_t(kt[g * HEAD_DIM:(g + 1) * HEAD_DIM], gk_ref[j], cos, sin)
            ko_ref[0, g] = jnp.concatenate([kh, tails[j]], axis=0).T.astype(ko_ref.dtype)
        for g in range(NSA_KV):
            vo_ref[0, g] = jnp.concatenate([vt[g * HEAD_DIM:(g + 1) * HEAD_DIM], row_ones], axis=0).astype(vo_ref.dtype)


def _in_proj(x2, g, w_pad, q_norm_g, k_norm_g, batch, seq):
    n = x2.shape[0]
    nt = seq // ROW_TILE
    assert seq // SEL_BLOCK <= HEAD_DIM
    cos, sin = _rope_tables(np.arange(seq))
    half = ROT_DIM // 2
    cos_t, sin_t = jnp.asarray(cos[:, :half].T.copy()), jnp.asarray(sin[:, :half].T.copy())
    lanes = lambda gain: jnp.tile(gain[..., None], (1,) * gain.ndim + (ROW_TILE,))
    gq = lanes(q_norm_g * (HEAD_DIM ** -0.5 * LOG2E))
    gk = lanes(k_norm_g[1:3])
    row = np.arange(ROW_TILE)
    perm = (np.arange(ROW_TILE)[None, :] == ((row % (ROW_TILE // CMP_STRIDE)) * CMP_STRIDE
                                             + row // (ROW_TILE // CMP_STRIDE))[:, None]).astype(np.float32)
    const = lambda i: (0, 0)
    rows_major = lambda d: pl.BlockSpec((1, NSA_KV, ROW_TILE, d), lambda i: (i // nt, 0, i % nt, 0))
    cols_major = lambda heads, d: pl.BlockSpec((1, heads, d, ROW_TILE), lambda i: (i // nt, 0, 0, i % nt))
    sds = lambda *shape: jax.ShapeDtypeStruct((batch,) + shape, BF16)
    grouped = pl.BlockSpec((1, ROW_TILE // CMP_STRIDE, CMP_STRIDE * LANES), lambda i: (i // nt, i % nt, 0))
    return pl.pallas_call(
        functools.partial(_in_proj_kernel, tiles_per_seq=nt),
        grid=(n // ROW_TILE,),
        in_specs=[pl.BlockSpec((ROW_TILE, D_MODEL), lambda i: (i, 0)),
                  pl.BlockSpec((1, D_MODEL), const),
                  pl.BlockSpec((D_MODEL, IN_PAD), const),
                  pl.BlockSpec((half, ROW_TILE), lambda i: (0, i % nt)),
                  pl.BlockSpec((half, ROW_TILE), lambda i: (0, i % nt)),
                  pl.BlockSpec((HEAD_DIM, ROW_TILE), const),
                  pl.BlockSpec((2, HEAD_DIM, ROW_TILE), lambda i: (0, 0, 0)),
                  pl.BlockSpec((ROW_TILE, ROW_TILE), const)],
        out_specs=[pl.BlockSpec((ROW_TILE, IN_PAD), lambda i: (i, 0)),
                   cols_major(NSA_HEADS, HEAD_DIM), rows_major(2 * HEAD_DIM), cols_major(NSA_KV, V_ROWS),
                   rows_major(2 * HEAD_DIM), cols_major(NSA_KV, V_ROWS), grouped, grouped],
        out_shape=[jax.ShapeDtypeStruct((n, IN_PAD), BF16),
                   sds(NSA_HEADS, HEAD_DIM, seq), sds(NSA_KV, seq, 2 * HEAD_DIM), sds(NSA_KV, V_ROWS, seq),
                   sds(NSA_KV, seq, 2 * HEAD_DIM), sds(NSA_KV, V_ROWS, seq),
                   sds(seq // CMP_STRIDE, CMP_STRIDE * LANES), sds(seq // CMP_STRIDE, CMP_STRIDE * LANES)],
        compiler_params=pltpu.CompilerParams(dimension_semantics=("parallel",),
                                             vmem_limit_bytes=VMEM_LIMIT),
        name="in_proj",
    )(x2, g, w_pad, cos_t, sin_t, gq, gk, jnp.asarray(perm, BF16))


def _compress_kernel(xk_ref, xv_ref, pea_ref, peb_ref, w1a_ref, w1b_ref, w2_ref, gk_ref, cos_ref, sin_ref,
                     rot_ref, kc_ref, vc_ref, shift_scr):
    nc = xk_ref.shape[1]
    for j, x_ref in enumerate((xk_ref, xv_ref)):
        x = x_ref[0].astype(F32)
        u = _dot((x + pea_ref[j:j + 1, :]).astype(BF16), w1a_ref[j])
        v = _dot((x + peb_ref[j:j + 1, :]).astype(BF16), w1b_ref[j])
        shift_scr[0:nc, :] = v
        shift_scr[nc:nc + HALO, :] = jnp.zeros((HALO, v.shape[1]), F32)
        hid = _silu(u + shift_scr[1:nc + 1, :])
        out = _dot(hid.astype(BF16), w2_ref[j])
        if j == 0:
            for g in range(NSA_KV):
                og = (out if g == 0 else pltpu.roll(out, HEAD_DIM, 1))[:, :HEAD_DIM]
                xn = _rms(og, gk_ref[...])
                kc_ref[0, g] = (xn * cos_ref[...] + _dot(xn.astype(BF16), rot_ref[...]) * sin_ref[...]).astype(kc_ref.dtype)
        else:
            vc_ref[0] = out.T.reshape(NSA_KV, HEAD_DIM, nc).astype(vc_ref.dtype)


def _compress(xk, xv, cmp_pe, cmp_w1, cmp_w2, gk0, batch, seq):
    nc = seq // CMP_STRIDE
    hidden = cmp_w1.shape[2]
    cos, sin = (jnp.asarray(t) for t in _rope_tables(np.arange(nc) * CMP_STRIDE + CMP_LEN - 1))
    pe = jnp.tile(cmp_pe[:, :, None, :], (1, 1, NSA_KV, 1))
    pea = pe[:, :CMP_STRIDE].reshape(2, -1)
    peb = pe[:, CMP_STRIDE:].reshape(2, -1)
    eye = jnp.eye(NSA_KV, dtype=cmp_w1.dtype)
    w1 = jnp.einsum('jldh,gk->jlgdkh', cmp_w1.reshape(2, CMP_LEN, HEAD_DIM, hidden), eye)
    w1 = w1.reshape(2, CMP_LEN, NSA_KV * HEAD_DIM, NSA_KV * hidden).astype(BF16)
    w1a = w1[:, :CMP_STRIDE].reshape(2, -1, NSA_KV * hidden)
    w1b = w1[:, CMP_STRIDE:].reshape(2, -1, NSA_KV * hidden)
    w2 = jnp.einsum('jhd,gk->jghkd', cmp_w2, eye).reshape(2, NSA_KV * hidden, NSA_KV * HEAD_DIM).astype(BF16)
    blk = pl.BlockSpec((1, nc, CMP_STRIDE * LANES), lambda b: (b, 0, 0))
    const2 = lambda b: (0, 0)
    const3 = lambda b: (0, 0, 0)
    return pl.pallas_call(
        _compress_kernel,
        grid=(batch,),
        in_specs=[blk, blk,
                  pl.BlockSpec(pea.shape, const2), pl.BlockSpec(peb.shape, const2),
                  pl.BlockSpec(w1a.shape, const3), pl.BlockSpec(w1b.shape, const3), pl.BlockSpec(w2.shape, const3),
                  pl.BlockSpec((1, HEAD_DIM), const2),
                  pl.BlockSpec((nc, HEAD_DIM), const2), pl.BlockSpec((nc, HEAD_DIM), const2),
                  pl.BlockSpec((HEAD_DIM, HEAD_DIM), const2)],
        out_specs=[pl.BlockSpec((1, NSA_KV, nc, HEAD_DIM), lambda b: (b, 0, 0, 0)),
                   pl.BlockSpec((1, NSA_KV, HEAD_DIM, nc), lambda b: (b, 0, 0, 0))],
        out_shape=[jax.ShapeDtypeStruct((batch, NSA_KV, nc, HEAD_DIM), BF16),
                   jax.ShapeDtypeStruct((batch, NSA_KV, HEAD_DIM, nc), BF16)],
        scratch_shapes=[pltpu.VMEM((nc + HALO, NSA_KV * hidden), F32)],
        compiler_params=pltpu.CompilerParams(dimension_semantics=("parallel",),
                                             vmem_limit_bytes=VMEM_LIMIT),
        name="nsa_compress",
    )(xk, xv, pea, peb, w1a, w1b, w2, gk0[None, :], cos, sin, jnp.asarray(_rot_matrix(1), BF16))


NSA_TQ = 256
CMP_TQ = 1024


def _overlap_t(seq):
    ns, nc = seq // SEL_BLOCK, seq // CMP_STRIDE
    cs = np.arange(nc)[None, :] * CMP_STRIDE
    ss = np.arange(ns)[:, None] * SEL_BLOCK
    ov = np.clip(np.minimum(cs + CMP_LEN, ss + SEL_BLOCK) - np.maximum(cs, ss), 0, None)
    return (ov / CMP_LEN).astype(np.float32)


def _cmpsel_kernel(offset_ref, q_ref, kc_ref, vc_ref, ovt_ref, ocmp_ref, bias_ref, *, top):
    tq, nc = q_ref.shape[3], kc_ref.shape[2]
    ns = ovt_ref.shape[0]
    t0 = pl.program_id(1) * tq
    n_i = lax.broadcasted_iota(jnp.int32, (ns, tq), 0)
    t_i = t0 + lax.broadcasted_iota(jnp.int32, (ns, tq), 1)
    cur = t_i // SEL_BLOCK
    forced = jnp.where(n_i == 0, FORCE, jnp.where(n_i == cur, FORCE, jnp.where(n_i == cur - 1, FORCE, 0.0)))
    visible = n_i * SEL_BLOCK <= t_i
    sub = lax.broadcasted_iota(jnp.int32, (SUBLANES, tq), 0)

    def attend(g, quarters):
        ncv, nsv = quarters * nc // 4, quarters * ns // 4
        safe = max((quarters - 1) * nc // 4 - SUBLANES, 0)
        c_i = safe + lax.broadcasted_iota(jnp.int32, (ncv - safe, NSA_GROUP * tq), 0)
        q_i = lax.broadcasted_iota(jnp.int32, (ncv - safe, NSA_GROUP * tq), 1) & (tq - 1)
        cmask = c_i * CMP_STRIDE + (CMP_LEN - 1) <= t0 + q_i
        qg = jnp.concatenate([q_ref[0, g * NSA_GROUP + h] for h in range(NSA_GROUP)], axis=1)
        s = _dot(kc_ref[0, g, 0:ncv, :], qg)
        s_hi = jnp.where(cmask, s[safe:], NEG)
        m = jnp.max(s_hi, axis=0, keepdims=True)
        if safe:
            m = jnp.maximum(m, jnp.max(s[:safe], axis=0, keepdims=True))
        e_hi = jnp.exp2(s_hi - m)
        total = jnp.sum(e_hi, axis=0, keepdims=True)
        if safe:
            e_lo = jnp.exp2(s[:safe] - m)
            total += jnp.sum(e_lo, axis=0, keepdims=True)
        inv = 1.0 / total
        p = jnp.where(cmask, e_hi * inv, 0.0)
        if safe:
            p = jnp.concatenate([e_lo * inv, p], axis=0)
        o = _dot(vc_ref[0, g, :, 0:ncv], p.astype(BF16)).astype(ocmp_ref.dtype)
        psum = p[:, :tq]
        for h in range(NSA_GROUP):
            ocmp_ref[0, g * NSA_GROUP + h] = o[:, h * tq:(h + 1) * tq]
            if h:
                psum += p[:, h * tq:(h + 1) * tq]
        ovt = ovt_ref[:, 0:ncv]
        hi, lo = _split(psum)
        val = jnp.where(visible, _dot(ovt, hi) + _dot(ovt, lo) + forced, NEG)
        groups = []
        for r in range(ns // SUBLANES):
            if r * SUBLANES >= nsv:
                groups.append(jnp.zeros((SUBLANES, tq), F32))
                continue
            v_r = val[r * SUBLANES:(r + 1) * SUBLANES]
            c = jnp.zeros((SUBLANES, tq), F32)
            for m in range(nsv):
                row = val[m:m + 1, :]
                if m // SUBLANES > r:
                    hit = row > v_r
                elif m // SUBLANES < r:
                    hit = row >= v_r
                else:
                    hit = jnp.where(sub > m % SUBLANES, jnp.where(row >= v_r, 1.0, 0.0),
                                    jnp.where(row > v_r, 1.0, 0.0)) > 0.5
                c = jnp.where(hit, c + 1.0, c)
            groups.append(c)
        bias = jnp.where(jnp.concatenate(groups, axis=0) < top, offset_ref[0], MASK_BIAS)
        if ns < HEAD_DIM:
            bias = jnp.concatenate([bias, jnp.zeros((HEAD_DIM - ns, tq), F32)], axis=0)
        bias_ref[0, g] = bias.astype(bias_ref.dtype)

    assert ns % (4 * SUBLANES) == 0 and nc % 4 == 0
    last_quarter = (t0 + tq - 1) // (nc // 4 * CMP_STRIDE)
    for g in range(NSA_KV):
        for v in range(4):
            @pl.when(last_quarter == v)
            def _():
                attend(g, v + 1)


def _cmpsel(offset, qt, kc, vct, batch, seq):
    nt = seq // CMP_TQ
    assert seq % (4 * CMP_TQ) == 0
    nc, ns = seq // CMP_STRIDE, seq // SEL_BLOCK
    qblk = pl.BlockSpec((1, NSA_HEADS, HEAD_DIM, CMP_TQ), lambda b, t: (b, 0, 0, t))
    return pl.pallas_call(
        functools.partial(_cmpsel_kernel, top=min(SEL_TOPK, ns)),
        grid=(batch, nt),
        in_specs=[pl.BlockSpec(memory_space=pltpu.SMEM), qblk,
                  pl.BlockSpec((1, NSA_KV, nc, HEAD_DIM), lambda b, t: (b, 0, 0, 0)),
                  pl.BlockSpec((1, NSA_KV, HEAD_DIM, nc), lambda b, t: (b, 0, 0, 0)),
                  pl.BlockSpec((ns, nc), lambda b, t: (0, 0))],
        out_specs=[qblk, pl.BlockSpec((1, NSA_KV, HEAD_DIM, CMP_TQ), lambda b, t: (b, 0, 0, t))],
        out_shape=[jax.ShapeDtypeStruct((batch, NSA_HEADS, HEAD_DIM, seq), BF16),
                   jax.ShapeDtypeStruct((batch, NSA_KV, HEAD_DIM, seq), BF16)],
        compiler_params=pltpu.CompilerParams(dimension_semantics=("parallel", "parallel"),
                                             vmem_limit_bytes=VMEM_LIMIT),
        name="nsa_cmpsel",
    )(offset, qt, kc, vct, jnp.asarray(_overlap_t(seq), BF16))


SAFE_OFFSET = 60.0


def _attn_kernel(q_ref, ks_ref, vs_ref, kw_ref, vw_ref, bias_ref, wrow_ref, ocmp_ref, gate_ref, gout_ref,
                 o_ref, acc_scr, *scratch, bounded):
    tq = q_ref.shape[3]
    i = pl.program_id(1)
    t0 = i * tq
    q_sel, q_win = [], []
    for g in range(NSA_KV):
        q_grp = jnp.concatenate([q_ref[0, g * NSA_GROUP + h] for h in range(NSA_GROUP)], axis=1)
        q_sel.append(jnp.concatenate([q_grp, jnp.concatenate([bias_ref[0, g]] * NSA_GROUP, axis=1)], axis=0))
        q_win.append(jnp.concatenate([q_grp, wrow_ref[...]], axis=0))

    def per_head(s, keep):
        return jnp.concatenate([jnp.where(keep, s[:, h * tq:(h + 1) * tq], NEG) for h in range(NSA_GROUP)], axis=1)

    def causal(s):
        key_i = lax.broadcasted_iota(jnp.int32, (tq, tq), 0)
        return per_head(s, key_i <= lax.broadcasted_iota(jnp.int32, (tq, tq), 1))

    def tile(j):
        return pl.ds(pl.multiple_of(j * tq, tq), tq)

    acc_scr[...] = jnp.zeros_like(acc_scr)
    span = WINDOW + tq
    w0 = pl.multiple_of(jnp.maximum(t0 + tq - span, 0), tq)
    dist = (t0 - w0) + (lax.broadcasted_iota(jnp.int32, (span, tq), 1)
                        - lax.broadcasted_iota(jnp.int32, (span, tq), 0))
    in_window = jnp.abs(2 * dist - (WINDOW - 1)) < WINDOW
    win_scores = lambda g: per_head(_dot(kw_ref[0, g, pl.ds(w0, span), :], q_win[g]), in_window)

    def finish(win):
        gate = jax.nn.sigmoid(gate_ref[...].astype(F32).T)
        gout = gout_ref[...]
        rows = []
        for hh in range(NSA_HEADS):
            g, lanes = hh // NSA_GROUP, slice((hh % NSA_GROUP) * tq, (hh % NSA_GROUP + 1) * tq)
            a_s, a_w = acc_scr[g][:, lanes], win[g][:, lanes]
            o = (gate[hh:hh + 1] * ocmp_ref[0, hh].astype(F32)
                 + gate[NSA_HEADS + hh:NSA_HEADS + hh + 1] * (a_s[:HEAD_DIM] * (1.0 / a_s[HEAD_DIM:HEAD_DIM + 1]))
                 + gate[2 * NSA_HEADS + hh:2 * NSA_HEADS + hh + 1] * (a_w[:HEAD_DIM] * (1.0 / a_w[HEAD_DIM:HEAD_DIM + 1])))
            rows.append(o * lax.rsqrt(jnp.mean(o * o, axis=0, keepdims=True) + EPS) * gout)
        o_ref[...] = jnp.concatenate(rows, axis=0).T.astype(o_ref.dtype)

    if bounded:
        p0_scr, p1_scr = scratch

        def probs(j, p_scr, diag):
            for g in range(NSA_KV):
                s = _dot(ks_ref[0, g, tile(j), :], q_sel[g])
                p_scr[g] = jnp.exp2(causal(s) if diag else s).astype(BF16)

        def accumulate(j, p_scr):
            for g in range(NSA_KV):
                acc_scr[g] += _dot(vs_ref[0, g, :, tile(j)], p_scr[g])

        probs(0, p0_scr, False)

        def body(jj, carry):
            probs(2 * jj + 1, p1_scr, False)
            accumulate(2 * jj, p0_scr)
            probs(2 * jj + 2, p0_scr, False)
            accumulate(2 * jj + 1, p1_scr)
            return carry

        lax.fori_loop(0, (i - 1) // 2, body, 0)

        def last(p_scr):
            probs(i, p_scr, True)
            s_win = [win_scores(g) for g in range(NSA_KV)]
            accumulate(i, p_scr)
            finish([_dot(vw_ref[0, g, :, pl.ds(w0, span)], jnp.exp2(s_win[g]).astype(BF16)) for g in range(NSA_KV)])

        @pl.when(i == 0)
        def _():
            last(p0_scr)

        @pl.when(i % 2 == 1)
        def _():
            accumulate(i - 1, p0_scr)
            last(p1_scr)

        @pl.when((i > 0) & (i % 2 == 0))
        def _():
            probs(i - 1, p1_scr, False)
            accumulate(i - 2, p0_scr)
            accumulate(i - 1, p1_scr)
            last(p0_scr)
    else:
        m_scr, s0_scr, s1_scr = scratch
        m_scr[...] = jnp.full(m_scr.shape, NEG, F32)

        def scores(j, s_scr):
            for g in range(NSA_KV):
                s_scr[g] = _dot(ks_ref[0, g, tile(j), :], q_sel[g])

        def consume(j, s_scr, diag):
            for g in range(NSA_KV):
                s = causal(s_scr[g]) if diag else s_scr[g]
                m_old = m_scr[g]
                m_new = jnp.maximum(m_old, jnp.max(s, axis=0, keepdims=True))
                p = jnp.exp2(s - m_new).astype(BF16)
                acc_scr[g] = jnp.exp2(m_old - m_new) * acc_scr[g] + _dot(vs_ref[0, g, :, tile(j)], p)
                m_scr[g] = m_new

        scores(0, s0_scr)

        def body(jj, carry):
            scores(2 * jj + 1, s1_scr)
            consume(2 * jj, s0_scr, False)
            scores(2 * jj + 2, s0_scr)
            consume(2 * jj + 1, s1_scr, False)
            return carry

        lax.fori_loop(0, i // 2, body, 0)

        @pl.when(i % 2 == 0)
        def _():
            consume(i, s0_scr, True)

        @pl.when(i % 2 == 1)
        def _():
            scores(i, s1_scr)
            consume(i - 1, s0_scr, False)
            consume(i, s1_scr, True)

        win = []
        for g in range(NSA_KV):
            s = win_scores(g)
            p = jnp.exp2(s - jnp.max(s, axis=0, keepdims=True)).astype(BF16)
            win.append(_dot(vw_ref[0, g, :, pl.ds(w0, span)], p))
        finish(win)


def _attn(qt, ks, vst, kw, vwt, bias, wrow, ocmp, proj, gout, *, bounded):
    batch, _, _, seq = qt.shape
    nt = seq // NSA_TQ
    assert WINDOW % NSA_TQ == 0 and seq >= WINDOW + NSA_TQ
    wide = NSA_GROUP * NSA_TQ
    rows_full = pl.BlockSpec((1, NSA_KV, seq, 2 * HEAD_DIM), lambda b, t: (b, 0, 0, 0))
    cols_full = pl.BlockSpec((1, NSA_KV, V_ROWS, seq), lambda b, t: (b, 0, 0, 0))
    qtile = lambda heads: pl.BlockSpec((1, heads, HEAD_DIM, NSA_TQ), lambda b, t: (b, 0, 0, t))
    if bounded:
        scratch = [pltpu.VMEM((NSA_KV, NSA_TQ, wide), BF16), pltpu.VMEM((NSA_KV, NSA_TQ, wide), BF16)]
    else:
        scratch = [pltpu.VMEM((NSA_KV, 1, wide), F32),
                   pltpu.VMEM((NSA_KV, NSA_TQ, wide), F32), pltpu.VMEM((NSA_KV, NSA_TQ, wide), F32)]
    return pl.pallas_call(
        functools.partial(_attn_kernel, bounded=bounded),
        grid=(batch, nt),
        in_specs=[qtile(NSA_HEADS), rows_full, cols_full, rows_full, cols_full, qtile(NSA_KV),
                  pl.BlockSpec((HEAD_DIM, wide), lambda b, t: (0, 0)),
                  qtile(NSA_HEADS),
                  pl.BlockSpec((NSA_TQ, LANES), lambda b, t: (b * nt + t, COL_GATE)),
                  pl.BlockSpec((HEAD_DIM, NSA_TQ), lambda b, t: (0, 0))],
        out_specs=pl.BlockSpec((NSA_TQ, NSA_WIDTH), lambda b, t: (b * nt + t, 0)),
        out_shape=jax.ShapeDtypeStruct((batch * seq, NSA_WIDTH), BF16),
        scratch_shapes=[pltpu.VMEM((NSA_KV, V_ROWS, wide), F32)] + scratch,
        compiler_params=pltpu.CompilerParams(dimension_semantics=("parallel", "parallel"),
                                             vmem_limit_bytes=VMEM_LIMIT),
        name="nsa_attn" if bounded else "nsa_attn_online",
    )(qt, ks, vst, kw, vwt, bias, wrow, ocmp, proj, gout)


def _nsa(proj, qt, ks, vst, kw, vwt, xk, xv, q_norm_g, k_norm_g, cmp_pe, cmp_w1, cmp_w2, out_norm_g, batch, seq):
    def offset(gk):
        bound = HEAD_DIM * (HEAD_DIM ** -0.5 * LOG2E) * jnp.max(jnp.abs(q_norm_g)) * jnp.max(jnp.abs(gk))
        return (1.02 * bound).astype(BF16).astype(F32)

    off_sel, off_win = offset(k_norm_g[1]), offset(k_norm_g[2])
    kc, vct = _compress(xk, xv, cmp_pe, cmp_w1, cmp_w2, k_norm_g[0], batch, seq)
    ocmp, bias = _cmpsel(-off_sel[None], qt, kc, vct, batch, seq)
    wrow = jnp.zeros((HEAD_DIM, NSA_GROUP * NSA_TQ), F32).at[0].set(-off_win).astype(BF16)
    gout = jnp.tile(out_norm_g[:, None], (1, NSA_TQ))
    return lax.cond(jnp.maximum(off_sel, off_win) <= SAFE_OFFSET,
                    functools.partial(_attn, bounded=True), functools.partial(_attn, bounded=False),
                    qt, ks, vst, kw, vwt, bias, wrow, ocmp, proj, gout)


def kernel(x, p, attn_norm_g, w_in, hg_lb_logits, hg_norm_g, nsa_q_norm_g, nsa_k_norm_g, cmp_pe,
           cmp_w1, cmp_w2, nsa_out_norm_g, w_out, ffn_norm_g, w_up, conv_w, conv_b, w_down,
           ple_gate_norm_g, w_ple_gate, w_ple, ple_norm_g):
    B, T, _ = x.shape
    n = B * T
    h = x.reshape(n, D_MODEL)
    for i in range(w_in.shape[0]):
        w_pad = jnp.pad(w_in[i], ((0, 0), (0, IN_PAD - IN_TOTAL))).astype(BF16)
        proj, *qkv = _in_proj(h, attn_norm_g[i][None, :], w_pad, nsa_q_norm_g[i], nsa_k_norm_g[i], B, T)
        o_hg = _hgrn(proj, hg_lb_logits.astype(F32), hg_norm_g[i][None, :], i, B, T)
        o_nsa = _nsa(proj, *qkv, nsa_q_norm_g[i], nsa_k_norm_g[i], cmp_pe[i], cmp_w1[i], cmp_w2[i],
                     nsa_out_norm_g[i], B, T)
        h = _post_mixer(h, o_hg, o_nsa, p[i].reshape(n, PLE_DIM), w_out[i].astype(BF16), ffn_norm_g[i][None, :],
                        w_up[i].astype(BF16), conv_w[i], conv_b[i][None, :], w_down[i].astype(BF16),
                        ple_gate_norm_g[i][None, :], w_ple_gate[i].astype(BF16), w_ple[i].astype(BF16),
                        ple_norm_g[i][None, :], T)
    return h.reshape(B, T, D_MODEL)
```

```python
import functools

import numpy as np
import jax
import jax.numpy as jnp
from jax import lax
from jax.experimental import pallas as pl
from jax.experimental.pallas import tpu as pltpu

F32 = jnp.float32
BF16 = jnp.bfloat16

D_MODEL = 1024
PLE_DIM = 256
EPS = 1e-6
NEG = -1e30
FORCE = 1e6

HG_HEADS = 4
HG_KDIM = 128
HG_VDIM = 128
HG_WIDTH = HG_HEADS * HG_VDIM
HG_CHUNK = 64

NSA_HEADS = 8
NSA_KV = 2
NSA_GROUP = NSA_HEADS // NSA_KV
HEAD_DIM = 64
NSA_WIDTH = NSA_HEADS * HEAD_DIM
KV_WIDTH = NSA_KV * HEAD_DIM
CMP_LEN = 32
CMP_STRIDE = 16
SEL_BLOCK = 64
SEL_TOPK = 16
WINDOW = 512
ROT_DIM = HEAD_DIM // 4
ROPE_THETA = 500000.0
D_FF = 2816
CONV_W = 3

IN_SIZES = (HG_HEADS * HG_KDIM, HG_HEADS * HG_KDIM, HG_WIDTH, HG_WIDTH, NSA_WIDTH,
            KV_WIDTH, KV_WIDTH, KV_WIDTH, KV_WIDTH, KV_WIDTH, KV_WIDTH, 3 * NSA_HEADS)
IN_TOTAL = sum(IN_SIZES)
LANES = 128
IN_PAD = -(-IN_TOTAL // LANES) * LANES
VMEM_LIMIT = 56 * 1024 * 1024

ROW_TILE = 512
FF_CHUNK = 256
SUBLANES = 8
HALO = SUBLANES
FFN_HALO = 2 * SUBLANES


def _rms(x, g):
    return x * lax.rsqrt(jnp.mean(x * x, axis=-1, keepdims=True) + EPS) * g


def _dot(a, b):
    return jnp.dot(a, b, preferred_element_type=F32)


def _dot_nt(a, b):
    return lax.dot_general(a, b, (((1,), (1,)), ((), ())), preferred_element_type=F32)


def _dot_tn(a, b):
    return lax.dot_general(a, b, (((0,), (0,)), ((), ())), preferred_element_type=F32)


def _split(x):
    hi = x.astype(BF16)
    return hi, (x - hi.astype(F32)).astype(BF16)


def _silu(x):
    hx = 0.5 * x
    return hx + hx * jnp.tanh(hx)


def _post_mixer_kernel(x_ref, xh_ref, a_ref, ah_ref, b_ref, bh_ref, p_ref, wo_ref, gf_ref, wup_ref, cw_ref,
                       cb_ref, wdn_ref, gg_ref, wg_ref, wp_ref, gp_ref, o_ref,
                       h1_scr, hn_scr, ug_scr, uu_scr, act_scr, *, tiles_per_seq):
    rows = x_ref.shape[0]
    halo, body = slice(0, FFN_HALO), slice(FFN_HALO, FFN_HALO + rows)
    hn_scr[halo, :HG_WIDTH], hn_scr[halo, HG_WIDTH:] = ah_ref[...], bh_ref[...]
    hn_scr[body, :HG_WIDTH], hn_scr[body, HG_WIDTH:] = a_ref[...], b_ref[...]
    mixed = _dot(hn_scr[...], wo_ref[...])
    h1_scr[halo, :] = xh_ref[...] + mixed[halo]
    h1_scr[body, :] = x_ref[...] + mixed[body]
    first = (pl.program_id(0) % tiles_per_seq) == 0
    hn_scr[halo, :] = jnp.where(first, 0.0, _rms(h1_scr[halo, :], gf_ref[...])).astype(BF16)
    hn_scr[body, :] = _rms(h1_scr[body, :], gf_ref[...]).astype(BF16)
    for c in range(0, D_FF, FF_CHUNK):
        conv = []
        for scr, off in ((ug_scr, c), (uu_scr, D_FF + c)):
            scr[...] = _dot(hn_scr[...], wup_ref[:, off:off + FF_CHUNK])
            cw = cw_ref[:, off:off + FF_CHUNK]
            conv.append(scr[FFN_HALO - 2:FFN_HALO - 2 + rows, :] * cw[0:1, :]
                        + scr[FFN_HALO - 1:FFN_HALO - 1 + rows, :] * cw[1:2, :]
                        + scr[body, :] * cw[2:3, :]
                        + cb_ref[:, off:off + FF_CHUNK])
        gate, up = conv
        act_scr[:, c:c + FF_CHUNK] = (_silu(gate) * up).astype(BF16)
    o_ref[...] = h1_scr[body, :] + _dot(act_scr[...], wdn_ref[...])
    hn_scr[body, :] = _rms(o_ref[...], gg_ref[...]).astype(BF16)
    e = _rms(_dot(p_ref[...].astype(BF16), wp_ref[...]), gp_ref[...])
    gate = jax.nn.sigmoid(_dot(hn_scr[body, :], wg_ref[...]))
    o_ref[...] = o_ref[...] + gate * e


def _post_mixer(x2, o_hg, o_nsa, p2, w_out, gf, w_up, conv_w, conv_b, w_down, gg, w_gate, w_ple, gp, seq):
    n = x2.shape[0]
    assert HG_WIDTH + NSA_WIDTH == D_MODEL
    const = lambda i: (0, 0)
    tile = lambda width: pl.BlockSpec((ROW_TILE, width), lambda i: (i, 0))
    halo_blocks = ROW_TILE // FFN_HALO
    halo = lambda width: pl.BlockSpec((FFN_HALO, width), lambda i: (jnp.maximum(i * halo_blocks - 1, 0), 0))
    weight = lambda r, c: pl.BlockSpec((r, c), const, pipeline_mode=pl.Buffered(1))
    padded = FFN_HALO + ROW_TILE
    return pl.pallas_call(
        functools.partial(_post_mixer_kernel, tiles_per_seq=seq // ROW_TILE),
        grid=(n // ROW_TILE,),
        in_specs=[tile(D_MODEL), halo(D_MODEL), tile(HG_WIDTH), halo(HG_WIDTH), tile(NSA_WIDTH), halo(NSA_WIDTH),
                  tile(PLE_DIM),
                  weight(D_MODEL, D_MODEL),
                  pl.BlockSpec((1, D_MODEL), const),
                  weight(D_MODEL, 2 * D_FF),
                  pl.BlockSpec((CONV_W, 2 * D_FF), const),
                  pl.BlockSpec((1, 2 * D_FF), const),
                  weight(D_FF, D_MODEL),
                  pl.BlockSpec((1, D_MODEL), const),
                  weight(D_MODEL, D_MODEL),
                  weight(PLE_DIM, D_MODEL),
                  pl.BlockSpec((1, D_MODEL), const)],
        out_specs=tile(D_MODEL),
        out_shape=jax.ShapeDtypeStruct((n, D_MODEL), F32),
        scratch_shapes=[pltpu.VMEM((padded, D_MODEL), F32),
                        pltpu.VMEM((padded, D_MODEL), BF16),
                        pltpu.VMEM((padded, FF_CHUNK), F32),
                        pltpu.VMEM((padded, FF_CHUNK), F32),
                        pltpu.VMEM((ROW_TILE, D_FF), BF16)],
        compiler_params=pltpu.CompilerParams(dimension_semantics=("parallel",),
                                             vmem_limit_bytes=VMEM_LIMIT),
        name="post_mixer",
    )(x2, x2, o_hg, o_hg, o_nsa, o_nsa, p2, w_out, gf, w_up, conv_w, conv_b, w_down, gg, w_gate, w_ple, gp)


HG_LEVELS = (32, 16, 8, 4, 2, 1)
HG_TB = 1024


def _hgrn_level_matrices():
    c = HG_CHUNK
    r = np.arange(c)[:, None]
    u = np.arange(c)[None, :]
    rows = []
    for m in HG_LEVELS:
        r0 = (r // m) * m
        upper = (r & m) != 0
        rows.append(np.where(upper, (u >= r0) & (u <= r), (u >= r + 1) & (u <= r0 + m - 1)))
    rows.append(u <= r)
    rows.append(u > r)
    return np.concatenate(rows, 0).astype(np.float32)


def _hgrn_pair_masks():
    c = HG_CHUNK
    t = np.arange(c)[:, None]
    s = np.arange(c)[None, :]
    masks = [((t // (2 * m)) == (s // (2 * m))) & ((t & m) != 0) & ((s & m) == 0) for m in HG_LEVELS]
    masks.append(t == s)
    return np.stack(masks).astype(np.float32)


def _hgrn_kernel(q_ref, f_ref, v_ref, g_ref, lbl_ref, ng_ref, mall_ref, msk_ref, o_ref, st_ref,
                 *, layer, n_chunks):
    @pl.when(pl.program_id(1) == 0)
    def _():
        st_ref[...] = jnp.zeros_like(st_ref)

    lbl = lbl_ref[...]
    e = jnp.exp(lbl - jnp.max(lbl, axis=0, keepdims=True))
    sm = e / jnp.sum(e, axis=0, keepdims=True)
    lb = jnp.sum(sm[:layer + 1], axis=0, keepdims=True)
    mall = mall_ref[...]
    ng = ng_ref[...]
    c = HG_CHUNK
    nl = len(HG_LEVELS)

    intra, q_decayed, carry_decay, carry_add = {}, {}, {}, {}
    for ci in range(n_chunks):
        r0 = ci * c
        fg = lb + (1.0 - lb) * jax.nn.sigmoid(f_ref[r0:r0 + c, :].astype(F32))
        kk = 1.0 - fg
        ee = jnp.exp(_dot(mall, jnp.concatenate(_split(jnp.log(fg)), axis=0)))
        qf = _silu(q_ref[r0:r0 + c, :].astype(F32)) * HG_KDIM ** -0.5
        for h in range(HG_HEADS):
            sl = slice(h * HG_KDIM, (h + 1) * HG_KDIM)
            qh, kh = qf[:, sl], kk[:, sl]
            a = msk_ref[nl] * _dot_nt(qh.astype(BF16), kh.astype(BF16))
            for j in range(nl):
                ej = ee[j * c:(j + 1) * c, sl]
                a += msk_ref[j] * _dot_nt((qh * ej).astype(BF16), (kh * ej).astype(BF16))
            vh = v_ref[r0:r0 + c, sl]
            eb = ee[nl * c:(nl + 1) * c, sl]
            intra[ci, h] = _dot(a.astype(BF16), vh)
            q_decayed[ci, h] = (qh * eb).astype(BF16)
            carry_decay[ci, h] = eb[c - 1:c, :]
            carry_add[ci, h] = _dot_tn(vh, (kh * ee[(nl + 1) * c:(nl + 2) * c, sl]).astype(BF16))

    for h in range(HG_HEADS):
        sl = slice(h * HG_KDIM, (h + 1) * HG_KDIM)
        st = st_ref[h]
        states = []
        for ci in range(n_chunks):
            states.append(st.astype(BF16))
            st = carry_decay[ci, h] * st + carry_add[ci, h]
        st_ref[h] = st
        for ci in range(n_chunks):
            r0 = ci * c
            o = intra[ci, h] + _dot_nt(q_decayed[ci, h], states[ci])
            on = o * lax.rsqrt(jnp.mean(o * o, axis=-1, keepdims=True) + EPS) * ng
            o_ref[r0:r0 + c, sl] = (on * _silu(g_ref[r0:r0 + c, sl].astype(F32))).astype(o_ref.dtype)


def _hgrn(proj, lb_logits, norm_g, layer, batch, seq):
    nt = seq // HG_TB
    w = HG_HEADS * HG_KDIM
    col = lambda k: (lambda b, t: (b * nt + t, k))
    const2 = lambda b, t: (0, 0)
    return pl.pallas_call(
        functools.partial(_hgrn_kernel, layer=layer, n_chunks=HG_TB // HG_CHUNK),
        grid=(batch, nt),
        in_specs=[pl.BlockSpec((HG_TB, w), col(0)), pl.BlockSpec((HG_TB, w), col(1)),
                  pl.BlockSpec((HG_TB, w), col(2)), pl.BlockSpec((HG_TB, w), col(3)),
                  pl.BlockSpec(lb_logits.shape, const2),
                  pl.BlockSpec((1, HG_VDIM), const2),
                  pl.BlockSpec(((len(HG_LEVELS) + 2) * HG_CHUNK, 2 * HG_CHUNK), const2),
                  pl.BlockSpec((len(HG_LEVELS) + 1, HG_CHUNK, HG_CHUNK), lambda b, t: (0, 0, 0))],
        out_specs=pl.BlockSpec((HG_TB, w), lambda b, t: (b * nt + t, 0)),
        out_shape=jax.ShapeDtypeStruct((batch * seq, HG_WIDTH), BF16),
        scratch_shapes=[pltpu.VMEM((HG_HEADS, HG_VDIM, HG_KDIM), F32)],
        compiler_params=pltpu.CompilerParams(dimension_semantics=("parallel", "arbitrary"),
                                             vmem_limit_bytes=VMEM_LIMIT),
        name="hgrn2",
    )(proj, proj, proj, proj, lb_logits, norm_g,
      jnp.asarray(np.tile(_hgrn_level_matrices(), (1, 2)), BF16), jnp.asarray(_hgrn_pair_masks(), F32))


COL_NQ = 2048 // LANES
COL_KC, COL_VC, COL_KS, COL_VS, COL_KW, COL_VW, COL_GATE = (COL_NQ + 4 + k for k in range(7))
LOG2E = 1.4426950408889634
MASK_BIAS = -1e30
V_ROWS = HEAD_DIM + 2 * SUBLANES


def _rope_tables(pos):
    half = ROT_DIM // 2
    inv = np.float32(ROPE_THETA) ** (-np.arange(half, dtype=np.float32) / np.float32(half))
    ang = pos.astype(np.float32)[:, None] * inv[None, :]
    n = pos.shape[0]
    pad = HEAD_DIM - ROT_DIM
    cos = np.concatenate([np.cos(ang), np.cos(ang), np.ones((n, pad), np.float32)], axis=1)
    sin = np.concatenate([np.sin(ang), np.sin(ang), np.zeros((n, pad), np.float32)], axis=1)
    return cos.astype(np.float32), sin.astype(np.float32)


def _rot_matrix(heads):
    half = ROT_DIM // 2
    r = np.zeros((HEAD_DIM, HEAD_DIM), np.float32)
    for j in range(half):
        r[j + half, j] = -1.0
        r[j, j + half] = 1.0
    return np.kron(np.eye(heads, dtype=np.float32), r)


def _norm_rope_t(xt, gain, cos, sin):
    half = ROT_DIM // 2
    xn = xt * lax.rsqrt(jnp.mean(xt * xt, axis=0, keepdims=True) + EPS) * gain
    x1, x2 = xn[:half], xn[half:ROT_DIM]
    return jnp.concatenate([x1 * cos - x2 * sin, x2 * cos + x1 * sin, xn[ROT_DIM:]], axis=0)


def _in_proj_kernel(x_ref, g_ref, w_ref, cos_ref, sin_ref, gq_ref, gk_ref, perm_ref,
                    o_ref, qo_ref, kso_ref, vso_ref, kwo_ref, vwo_ref, xk_ref, xv_ref, *, tiles_per_seq):
    tb = x_ref.shape[0]
    xb = _rms(x_ref[...], g_ref[...]).astype(BF16)
    hg_cols = COL_NQ * LANES
    qp = _dot(xb, w_ref[:, hg_cols:hg_cols + NSA_WIDTH])
    o_ref[:, hg_cols:hg_cols + NSA_WIDTH] = qp.astype(o_ref.dtype)
    kv = _dot(xb, w_ref[:, COL_KC * LANES:])
    o_ref[:, COL_KC * LANES:] = kv.astype(o_ref.dtype)
    for c in range(0, hg_cols, hg_cols // 2):
        o_ref[:, c:c + hg_cols // 2] = _dot(xb, w_ref[:, c:c + hg_cols // 2]).astype(o_ref.dtype)
    part_t = lambda col: kv[:, (col - COL_KC) * LANES:(col - COL_KC + 1) * LANES].T

    groups = tb // CMP_STRIDE
    banded = _dot(perm_ref[...], kv[:, :2 * LANES].astype(BF16))
    bands = [banded[l * groups:(l + 1) * groups] for l in range(CMP_STRIDE)]
    xk_ref[0] = jnp.concatenate([r[:, :LANES] for r in bands], axis=1).astype(xk_ref.dtype)
    xv_ref[0] = jnp.concatenate([r[:, LANES:] for r in bands], axis=1).astype(xv_ref.dtype)

    cos, sin = cos_ref[...], sin_ref[...]
    qt = qp.T
    for h in range(NSA_HEADS):
        qo_ref[0, h] = _norm_rope_t(qt[h * HEAD_DIM:(h + 1) * HEAD_DIM], gq_ref[...], cos, sin).astype(qo_ref.dtype)

    t0 = (pl.program_id(0) % tiles_per_seq) * tb
    block = (t0 + lax.broadcasted_iota(jnp.int32, (HEAD_DIM, tb), 1)) // SEL_BLOCK
    row_i = lax.broadcasted_iota(jnp.int32, (HEAD_DIM, tb), 0)
    tails = (jnp.where(row_i == block, 1.0, 0.0), jnp.where(row_i == 0, 1.0, 0.0))
    row_ones = jnp.ones((V_ROWS - HEAD_DIM, tb), F32)
    for j, (kcol, vcol, ko_ref, vo_ref) in enumerate(((COL_KS, COL_VS, kso_ref, vso_ref),
                                                      (COL_KW, COL_VW, kwo_ref, vwo_ref))):
        kt, vt = part_t(kcol), part_t(vcol)
        for g in range(NSA_KV):
            kh = _norm_rope_t(kt[g * HEAD_DIM:(g + 1) * HEAD_DIM], gk_ref[j], cos, sin)
            ko_ref[0, g] = jnp.concatenate([kh, tails[j]], axis=0).T.astype(ko_ref.dtype)
        for g in range(NSA_KV):
            vo_ref[0, g] = jnp.concatenate([vt[g * HEAD_DIM:(g + 1) * HEAD_DIM], row_ones], axis=0).astype(vo_ref.dtype)


def _in_proj(x2, g, w_pad, q_norm_g, k_norm_g, batch, seq):
    n = x2.shape[0]
    nt = seq // ROW_TILE
    assert seq // SEL_BLOCK <= HEAD_DIM
    cos, sin = _rope_tables(np.arange(seq))
    half = ROT_DIM // 2
    cos_t, sin_t = jnp.asarray(cos[:, :half].T.copy()), jnp.asarray(sin[:, :half].T.copy())
    lanes = lambda gain: jnp.tile(gain[..., None], (1,) * gain.ndim + (ROW_TILE,))
    gq = lanes(q_norm_g * (HEAD_DIM ** -0.5 * LOG2E))
    gk = lanes(k_norm_g[1:3])
    row = np.arange(ROW_TILE)
    perm = (np.arange(ROW_TILE)[None, :] == ((row % (ROW_TILE // CMP_STRIDE)) * CMP_STRIDE
                                             + row // (ROW_TILE // CMP_STRIDE))[:, None]).astype(np.float32)
    const = lambda i: (0, 0)
    rows_major = lambda d: pl.BlockSpec((1, NSA_KV, ROW_TILE, d), lambda i: (i // nt, 0, i % nt, 0))
    cols_major = lambda heads, d: pl.BlockSpec((1, heads, d, ROW_TILE), lambda i: (i // nt, 0, 0, i % nt))
    sds = lambda *shape: jax.ShapeDtypeStruct((batch,) + shape, BF16)
    grouped = pl.BlockSpec((1, ROW_TILE // CMP_STRIDE, CMP_STRIDE * LANES), lambda i: (i // nt, i % nt, 0))
    return pl.pallas_call(
        functools.partial(_in_proj_kernel, tiles_per_seq=nt),
        grid=(n // ROW_TILE,),
        in_specs=[pl.BlockSpec((ROW_TILE, D_MODEL), lambda i: (i, 0)),
                  pl.BlockSpec((1, D_MODEL), const),
                  pl.BlockSpec((D_MODEL, IN_PAD), const),
                  pl.BlockSpec((half, ROW_TILE), lambda i: (0, i % nt)),
                  pl.BlockSpec((half, ROW_TILE), lambda i: (0, i % nt)),
                  pl.BlockSpec((HEAD_DIM, ROW_TILE), const),
                  pl.BlockSpec((2, HEAD_DIM, ROW_TILE), lambda i: (0, 0, 0)),
                  pl.BlockSpec((ROW_TILE, ROW_TILE), const)],
        out_specs=[pl.BlockSpec((ROW_TILE, IN_PAD), lambda i: (i, 0)),
                   cols_major(NSA_HEADS, HEAD_DIM), rows_major(2 * HEAD_DIM), cols_major(NSA_KV, V_ROWS),
                   rows_major(2 * HEAD_DIM), cols_major(NSA_KV, V_ROWS), grouped, grouped],
        out_shape=[jax.ShapeDtypeStruct((n, IN_PAD), BF16),
                   sds(NSA_HEADS, HEAD_DIM, seq), sds(NSA_KV, seq, 2 * HEAD_DIM), sds(NSA_KV, V_ROWS, seq),
                   sds(NSA_KV, seq, 2 * HEAD_DIM), sds(NSA_KV, V_ROWS, seq),
                   sds(seq // CMP_STRIDE, CMP_STRIDE * LANES), sds(seq // CMP_STRIDE, CMP_STRIDE * LANES)],
        compiler_params=pltpu.CompilerParams(dimension_semantics=("parallel",),
                                             vmem_limit_bytes=VMEM_LIMIT),
        name="in_proj",
    )(x2, g, w_pad, cos_t, sin_t, gq, gk, jnp.asarray(perm, BF16))


def _compress_kernel(xk_ref, xv_ref, pea_ref, peb_ref, w1a_ref, w1b_ref, w2_ref, gk_ref, cos_ref, sin_ref,
                     rot_ref, kc_ref, vc_ref, shift_scr):
    nc = xk_ref.shape[1]
    for j, x_ref in enumerate((xk_ref, xv_ref)):
        x = x_ref[0].astype(F32)
        u = _dot((x + pea_ref[j:j + 1, :]).astype(BF16), w1a_ref[j])
        v = _dot((x + peb_ref[j:j + 1, :]).astype(BF16), w1b_ref[j])
        shift_scr[0:nc, :] = v
        shift_scr[nc:nc + HALO, :] = jnp.zeros((HALO, v.shape[1]), F32)
        hid = _silu(u + shift_scr[1:nc + 1, :])
        out = _dot(hid.astype(BF16), w2_ref[j])
        if j == 0:
            for g in range(NSA_KV):
                og = (out if g == 0 else pltpu.roll(out, HEAD_DIM, 1))[:, :HEAD_DIM]
                xn = _rms(og, gk_ref[...])
                kc_ref[0, g] = (xn * cos_ref[...] + _dot(xn.astype(BF16), rot_ref[...]) * sin_ref[...]).astype(kc_ref.dtype)
        else:
            vc_ref[0] = out.T.reshape(NSA_KV, HEAD_DIM, nc).astype(vc_ref.dtype)


def _compress(xk, xv, cmp_pe, cmp_w1, cmp_w2, gk0, batch, seq):
    nc = seq // CMP_STRIDE
    hidden = cmp_w1.shape[2]
    cos, sin = (jnp.asarray(t) for t in _rope_tables(np.arange(nc) * CMP_STRIDE + CMP_LEN - 1))
    pe = jnp.tile(cmp_pe[:, :, None, :], (1, 1, NSA_KV, 1))
    pea = pe[:, :CMP_STRIDE].reshape(2, -1)
    peb = pe[:, CMP_STRIDE:].reshape(2, -1)
    eye = jnp.eye(NSA_KV, dtype=cmp_w1.dtype)
    w1 = jnp.einsum('jldh,gk->jlgdkh', cmp_w1.reshape(2, CMP_LEN, HEAD_DIM, hidden), eye)
    w1 = w1.reshape(2, CMP_LEN, NSA_KV * HEAD_DIM, NSA_KV * hidden).astype(BF16)
    w1a = w1[:, :CMP_STRIDE].reshape(2, -1, NSA_KV * hidden)
    w1b = w1[:, CMP_STRIDE:].reshape(2, -1, NSA_KV * hidden)
    w2 = jnp.einsum('jhd,gk->jghkd', cmp_w2, eye).reshape(2, NSA_KV * hidden, NSA_KV * HEAD_DIM).astype(BF16)
    blk = pl.BlockSpec((1, nc, CMP_STRIDE * LANES), lambda b: (b, 0, 0))
    const2 = lambda b: (0, 0)
    const3 = lambda b: (0, 0, 0)
    return pl.pallas_call(
        _compress_kernel,
        grid=(batch,),
        in_specs=[blk, blk,
                  pl.BlockSpec(pea.shape, const2), pl.BlockSpec(peb.shape, const2),
                  pl.BlockSpec(w1a.shape, const3), pl.BlockSpec(w1b.shape, const3), pl.BlockSpec(w2.shape, const3),
                  pl.BlockSpec((1, HEAD_DIM), const2),
                  pl.BlockSpec((nc, HEAD_DIM), const2), pl.BlockSpec((nc, HEAD_DIM), const2),
                  pl.BlockSpec((HEAD_DIM, HEAD_DIM), const2)],
        out_specs=[pl.BlockSpec((1, NSA_KV, nc, HEAD_DIM), lambda b: (b, 0, 0, 0)),
                   pl.BlockSpec((1, NSA_KV, HEAD_DIM, nc), lambda b: (b, 0, 0, 0))],
        out_shape=[jax.ShapeDtypeStruct((batch, NSA_KV, nc, HEAD_DIM), BF16),
                   jax.ShapeDtypeStruct((batch, NSA_KV, HEAD_DIM, nc), BF16)],
        scratch_shapes=[pltpu.VMEM((nc + HALO, NSA_KV * hidden), F32)],
        compiler_params=pltpu.CompilerParams(dimension_semantics=("parallel",),
                                             vmem_limit_bytes=VMEM_LIMIT),
        name="nsa_compress",
    )(xk, xv, pea, peb, w1a, w1b, w2, gk0[None, :], cos, sin, jnp.asarray(_rot_matrix(1), BF16))


NSA_TQ = 256
CMP_TQ = 1024


def _overlap_t(seq):
    ns, nc = seq // SEL_BLOCK, seq // CMP_STRIDE
    cs = np.arange(nc)[None, :] * CMP_STRIDE
    ss = np.arange(ns)[:, None] * SEL_BLOCK
    ov = np.clip(np.minimum(cs + CMP_LEN, ss + SEL_BLOCK) - np.maximum(cs, ss), 0, None)
    return (ov / CMP_LEN).astype(np.float32)


def _cmpsel_kernel(offset_ref, q_ref, kc_ref, vc_ref, ovt_ref, ocmp_ref, bias_ref, *, top):
    tq, nc = q_ref.shape[3], kc_ref.shape[2]
    ns = ovt_ref.shape[0]
    t0 = pl.program_id(1) * tq
    n_i = lax.broadcasted_iota(jnp.int32, (ns, tq), 0)
    t_i = t0 + lax.broadcasted_iota(jnp.int32, (ns, tq), 1)
    cur = t_i // SEL_BLOCK
    forced = jnp.where(n_i == 0, FORCE, jnp.where(n_i == cur, FORCE, jnp.where(n_i == cur - 1, FORCE, 0.0)))
    visible = n_i * SEL_BLOCK <= t_i
    sub = lax.broadcasted_iota(jnp.int32, (SUBLANES, tq), 0)

    def attend(g, quarters):
        ncv, nsv = quarters * nc // 4, quarters * ns // 4
        safe = max((quarters - 1) * nc // 4 - SUBLANES, 0)
        c_i = safe + lax.broadcasted_iota(jnp.int32, (ncv - safe, NSA_GROUP * tq), 0)
        q_i = lax.broadcasted_iota(jnp.int32, (ncv - safe, NSA_GROUP * tq), 1) & (tq - 1)
        cmask = c_i * CMP_STRIDE + (CMP_LEN - 1) <= t0 + q_i
        qg = jnp.concatenate([q_ref[0, g * NSA_GROUP + h] for h in range(NSA_GROUP)], axis=1)
        s = _dot(kc_ref[0, g, 0:ncv, :], qg)
        s_hi = jnp.where(cmask, s[safe:], NEG)
        m = jnp.max(s_hi, axis=0, keepdims=True)
        if safe:
            m = jnp.maximum(m, jnp.max(s[:safe], axis=0, keepdims=True))
        e_hi = jnp.exp2(s_hi - m)
        total = jnp.sum(e_hi, axis=0, keepdims=True)
        if safe:
            e_lo = jnp.exp2(s[:safe] - m)
            total += jnp.sum(e_lo, axis=0, keepdims=True)
        inv = 1.0 / total
        p = jnp.where(cmask, e_hi * inv, 0.0)
        if safe:
            p = jnp.concatenate([e_lo * inv, p], axis=0)
        o = _dot(vc_ref[0, g, :, 0:ncv], p.astype(BF16)).astype(ocmp_ref.dtype)
        psum = p[:, :tq]
        for h in range(NSA_GROUP):
            ocmp_ref[0, g * NSA_GROUP + h] = o[:, h * tq:(h + 1) * tq]
            if h:
                psum += p[:, h * tq:(h + 1) * tq]
        ovt = ovt_ref[:, 0:ncv]
        hi, lo = _split(psum)
        val = jnp.where(visible, _dot(ovt, hi) + _dot(ovt, lo) + forced, NEG)
        groups = []
        for r in range(ns // SUBLANES):
            if r * SUBLANES >= nsv:
                groups.append(jnp.zeros((SUBLANES, tq), F32))
                continue
            v_r = val[r * SUBLANES:(r + 1) * SUBLANES]
            c = jnp.zeros((SUBLANES, tq), F32)
            for m in range(nsv):
                row = val[m:m + 1, :]
                if m // SUBLANES > r:
                    hit = row > v_r
                elif m // SUBLANES < r:
                    hit = row >= v_r
                else:
                    hit = jnp.where(sub > m % SUBLANES, jnp.where(row >= v_r, 1.0, 0.0),
                                    jnp.where(row > v_r, 1.0, 0.0)) > 0.5
                c = jnp.where(hit, c + 1.0, c)
            groups.append(c)
        bias = jnp.where(jnp.concatenate(groups, axis=0) < top, offset_ref[0], MASK_BIAS)
        if ns < HEAD_DIM:
            bias = jnp.concatenate([bias, jnp.zeros((HEAD_DIM - ns, tq), F32)], axis=0)
        bias_ref[0, g] = bias.astype(bias_ref.dtype)

    assert ns % (4 * SUBLANES) == 0 and nc % 4 == 0
    last_quarter = (t0 + tq - 1) // (nc // 4 * CMP_STRIDE)
    for g in range(NSA_KV):
        for v in range(4):
            @pl.when(last_quarter == v)
            def _():
                attend(g, v + 1)


def _cmpsel(offset, qt, kc, vct, batch, seq):
    nt = seq // CMP_TQ
    assert seq % (4 * CMP_TQ) == 0
    nc, ns = seq // CMP_STRIDE, seq // SEL_BLOCK
    qblk = pl.BlockSpec((1, NSA_HEADS, HEAD_DIM, CMP_TQ), lambda b, t: (b, 0, 0, t))
    return pl.pallas_call(
        functools.partial(_cmpsel_kernel, top=min(SEL_TOPK, ns)),
        grid=(batch, nt),
        in_specs=[pl.BlockSpec(memory_space=pltpu.SMEM), qblk,
                  pl.BlockSpec((1, NSA_KV, nc, HEAD_DIM), lambda b, t: (b, 0, 0, 0)),
                  pl.BlockSpec((1, NSA_KV, HEAD_DIM, nc), lambda b, t: (b, 0, 0, 0)),
                  pl.BlockSpec((ns, nc), lambda b, t: (0, 0))],
        out_specs=[qblk, pl.BlockSpec((1, NSA_KV, HEAD_DIM, CMP_TQ), lambda b, t: (b, 0, 0, t))],
        out_shape=[jax.ShapeDtypeStruct((batch, NSA_HEADS, HEAD_DIM, seq), BF16),
                   jax.ShapeDtypeStruct((batch, NSA_KV, HEAD_DIM, seq), BF16)],
        compiler_params=pltpu.CompilerParams(dimension_semantics=("parallel", "parallel"),
                                             vmem_limit_bytes=VMEM_LIMIT),
        name="nsa_cmpsel",
    )(offset, qt, kc, vct, jnp.asarray(_overlap_t(seq), BF16))


SAFE_OFFSET = 60.0


def _attn_kernel(q_ref, ks_ref, vs_ref, kw_ref, vw_ref, bias_ref, wrow_ref, ocmp_ref, gate_ref, gout_ref,
                 o_ref, acc_scr, *scratch, bounded):
    tq = q_ref.shape[3]
    i = pl.program_id(1)
    t0 = i * tq
    q_sel, q_win = [], []
    for g in range(NSA_KV):
        q_grp = jnp.concatenate([q_ref[0, g * NSA_GROUP + h] for h in range(NSA_GROUP)], axis=1)
        q_sel.append(jnp.concatenate([q_grp, jnp.concatenate([bias_ref[0, g]] * NSA_GROUP, axis=1)], axis=0))
        q_win.append(jnp.concatenate([q_grp, wrow_ref[...]], axis=0))

    def per_head(s, keep):
        return jnp.concatenate([jnp.where(keep, s[:, h * tq:(h + 1) * tq], NEG) for h in range(NSA_GROUP)], axis=1)

    def causal(s):
        key_i = lax.broadcasted_iota(jnp.int32, (tq, tq), 0)
        return per_head(s, key_i <= lax.broadcasted_iota(jnp.int32, (tq, tq), 1))

    def tile(j):
        return pl.ds(pl.multiple_of(j * tq, tq), tq)

    acc_scr[...] = jnp.zeros_like(acc_scr)
    span = WINDOW + tq
    w0 = pl.multiple_of(jnp.maximum(t0 + tq - span, 0), tq)
    dist = (t0 - w0) + (lax.broadcasted_iota(jnp.int32, (span, tq), 1)
                        - lax.broadcasted_iota(jnp.int32, (span, tq), 0))
    in_window = jnp.abs(2 * dist - (WINDOW - 1)) < WINDOW
    win_scores = lambda g: per_head(_dot(kw_ref[0, g, pl.ds(w0, span), :], q_win[g]), in_window)

    def finish(win):
        gate = jax.nn.sigmoid(gate_ref[...].astype(F32).T)
        gout = gout_ref[...]
        rows = []
        for hh in range(NSA_HEADS):
            g, lanes = hh // NSA_GROUP, slice((hh % NSA_GROUP) * tq, (hh % NSA_GROUP + 1) * tq)
            a_s, a_w = acc_scr[g][:, lanes], win[g][:, lanes]
            o = (gate[hh:hh + 1] * ocmp_ref[0, hh].astype(F32)
                 + gate[NSA_HEADS + hh:NSA_HEADS + hh + 1] * (a_s[:HEAD_DIM] * (1.0 / a_s[HEAD_DIM:HEAD_DIM + 1]))
                 + gate[2 * NSA_HEADS + hh:2 * NSA_HEADS + hh + 1] * (a_w[:HEAD_DIM] * (1.0 / a_w[HEAD_DIM:HEAD_DIM + 1])))
            rows.append(o * lax.rsqrt(jnp.mean(o * o, axis=0, keepdims=True) + EPS) * gout)
        o_ref[...] = jnp.concatenate(rows, axis=0).T.astype(o_ref.dtype)

    if bounded:
        p0_scr, p1_scr = scratch

        def probs(j, p_scr, diag):
            for g in range(NSA_KV):
                s = _dot(ks_ref[0, g, tile(j), :], q_sel[g])
                p_scr[g] = jnp.exp2(causal(s) if diag else s).astype(BF16)

        def accumulate(j, p_scr):
            for g in range(NSA_KV):
                acc_scr[g] += _dot(vs_ref[0, g, :, tile(j)], p_scr[g])

        probs(0, p0_scr, False)

        def pair(j0):
            probs(j0 + 1, p1_scr, False)
            accumulate(j0, p0_scr)
            probs(j0 + 2, p0_scr, False)
            accumulate(j0 + 1, p1_scr)

        def body4(jj, carry):
            pair(4 * jj)
            pair(4 * jj + 2)
            return carry

        ahead = jnp.maximum(i - 1, 0)
        done = 4 * (ahead // 4)
        lax.fori_loop(0, ahead // 4, body4, 0)

        def body2(jj, carry):
            pair(done + 2 * jj)
            return carry

        lax.fori_loop(0, (ahead - done) // 2, body2, 0)

        def last(p_scr):
            probs(i, p_scr, True)
            s_win = [win_scores(g) for g in range(NSA_KV)]
            accumulate(i, p_scr)
            finish([_dot(vw_ref[0, g, :, pl.ds(w0, span)], jnp.exp2(s_win[g]).astype(BF16)) for g in range(NSA_KV)])

        @pl.when(i == 0)
        def _():
            last(p0_scr)

        @pl.when(i % 2 == 1)
        def _():
            accumulate(i - 1, p0_scr)
            last(p1_scr)

        @pl.when((i > 0) & (i % 2 == 0))
        def _():
            probs(i - 1, p1_scr, False)
            accumulate(i - 2, p0_scr)
            accumulate(i - 1, p1_scr)
            last(p0_scr)
    else:
        m_scr, s0_scr, s1_scr = scratch
        m_scr[...] = jnp.full(m_scr.shape, NEG, F32)

        def scores(j, s_scr):
            for g in range(NSA_KV):
                s_scr[g] = _dot(ks_ref[0, g, tile(j), :], q_sel[g])

        def consume(j, s_scr, diag):
            for g in range(NSA_KV):
                s = causal(s_scr[g]) if diag else s_scr[g]
                m_old = m_scr[g]
                m_new = jnp.maximum(m_old, jnp.max(s, axis=0, keepdims=True))
                p = jnp.exp2(s - m_new).astype(BF16)
                acc_scr[g] = jnp.exp2(m_old - m_new) * acc_scr[g] + _dot(vs_ref[0, g, :, tile(j)], p)
                m_scr[g] = m_new

        scores(0, s0_scr)

        def body(jj, carry):
            scores(2 * jj + 1, s1_scr)
            consume(2 * jj, s0_scr, False)
            scores(2 * jj + 2, s0_scr)
            consume(2 * jj + 1, s1_scr, False)
            return carry

        lax.fori_loop(0, i // 2, body, 0)

        @pl.when(i % 2 == 0)
        def _():
            consume(i, s0_scr, True)

        @pl.when(i % 2 == 1)
        def _():
            scores(i, s1_scr)
            consume(i - 1, s0_scr, False)
            consume(i, s1_scr, True)

        win = []
        for g in range(NSA_KV):
            s = win_scores(g)
            p = jnp.exp2(s - jnp.max(s, axis=0, keepdims=True)).astype(BF16)
            win.append(_dot(vw_ref[0, g, :, pl.ds(w0, span)], p))
        finish(win)


def _attn(qt, ks, vst, kw, vwt, bias, wrow, ocmp, proj, gout, *, bounded):
    batch, _, _, seq = qt.shape
    nt = seq // NSA_TQ
    assert WINDOW % NSA_TQ == 0 and seq >= WINDOW + NSA_TQ
    wide = NSA_GROUP * NSA_TQ
    rows_full = pl.BlockSpec((1, NSA_KV, seq, 2 * HEAD_DIM), lambda b, t: (b, 0, 0, 0))
    cols_full = pl.BlockSpec((1, NSA_KV, V_ROWS, seq), lambda b, t: (b, 0, 0, 0))
    qtile = lambda heads: pl.BlockSpec((1, heads, HEAD_DIM, NSA_TQ), lambda b, t: (b, 0, 0, t))
    if bounded:
        scratch = [pltpu.VMEM((NSA_KV, NSA_TQ, wide), BF16), pltpu.VMEM((NSA_KV, NSA_TQ, wide), BF16)]
    else:
        scratch = [pltpu.VMEM((NSA_KV, 1, wide), F32),
                   pltpu.VMEM((NSA_KV, NSA_TQ, wide), F32), pltpu.VMEM((NSA_KV, NSA_TQ, wide), F32)]
    return pl.pallas_call(
        functools.partial(_attn_kernel, bounded=bounded),
        grid=(batch, nt),
        in_specs=[qtile(NSA_HEADS), rows_full, cols_full, rows_full, cols_full, qtile(NSA_KV),
                  pl.BlockSpec((HEAD_DIM, wide), lambda b, t: (0, 0)),
                  qtile(NSA_HEADS),
                  pl.BlockSpec((NSA_TQ, LANES), lambda b, t: (b * nt + t, COL_GATE)),
                  pl.BlockSpec((HEAD_DIM, NSA_TQ), lambda b, t: (0, 0))],
        out_specs=pl.BlockSpec((NSA_TQ, NSA_WIDTH), lambda b, t: (b * nt + t, 0)),
        out_shape=jax.ShapeDtypeStruct((batch * seq, NSA_WIDTH), BF16),
        scratch_shapes=[pltpu.VMEM((NSA_KV, V_ROWS, wide), F32)] + scratch,
        compiler_params=pltpu.CompilerParams(dimension_semantics=("parallel", "parallel"),
                                             vmem_limit_bytes=VMEM_LIMIT),
        name="nsa_attn" if bounded else "nsa_attn_online",
    )(qt, ks, vst, kw, vwt, bias, wrow, ocmp, proj, gout)


def _nsa(proj, qt, ks, vst, kw, vwt, xk, xv, q_norm_g, k_norm_g, cmp_pe, cmp_w1, cmp_w2, out_norm_g, batch, seq):
    def offset(gk):
        bound = HEAD_DIM * (HEAD_DIM ** -0.5 * LOG2E) * jnp.max(jnp.abs(q_norm_g)) * jnp.max(jnp.abs(gk))
        return (1.02 * bound).astype(BF16).astype(F32)

    off_sel, off_win = offset(k_norm_g[1]), offset(k_norm_g[2])
    kc, vct = _compress(xk, xv, cmp_pe, cmp_w1, cmp_w2, k_norm_g[0], batch, seq)
    ocmp, bias = _cmpsel(-off_sel[None], qt, kc, vct, batch, seq)
    wrow = jnp.zeros((HEAD_DIM, NSA_GROUP * NSA_TQ), F32).at[0].set(-off_win).astype(BF16)
    gout = jnp.tile(out_norm_g[:, None], (1, NSA_TQ))
    return lax.cond(jnp.maximum(off_sel, off_win) <= SAFE_OFFSET,
                    functools.partial(_attn, bounded=True), functools.partial(_attn, bounded=False),
                    qt, ks, vst, kw, vwt, bias, wrow, ocmp, proj, gout)


def kernel(x, p, attn_norm_g, w_in, hg_lb_logits, hg_norm_g, nsa_q_norm_g, nsa_k_norm_g, cmp_pe,
           cmp_w1, cmp_w2, nsa_out_norm_g, w_out, ffn_norm_g, w_up, conv_w, conv_b, w_down,
           ple_gate_norm_g, w_ple_gate, w_ple, ple_norm_g):
    B, T, _ = x.shape
    n = B * T
    h = x.reshape(n, D_MODEL)
    for i in range(w_in.shape[0]):
        w_pad = jnp.pad(w_in[i], ((0, 0), (0, IN_PAD - IN_TOTAL))).astype(BF16)
        proj, *qkv = _in_proj(h, attn_norm_g[i][None, :], w_pad, nsa_q_norm_g[i], nsa_k_norm_g[i], B, T)
        o_hg = _hgrn(proj, hg_lb_logits.astype(F32), hg_norm_g[i][None, :], i, B, T)
        o_nsa = _nsa(proj, *qkv, nsa_q_norm_g[i], nsa_k_norm_g[i], cmp_pe[i], cmp_w1[i], cmp_w2[i],
                     nsa_out_norm_g[i], B, T)
        h = _post_mixer(h, o_hg, o_nsa, p[i].reshape(n, PLE_DIM), w_out[i].astype(BF16), ffn_norm_g[i][None, :],
                        w_up[i].astype(BF16), conv_w[i], conv_b[i][None, :], w_down[i].astype(BF16),
                        ple_gate_norm_g[i][None, :], w_ple_gate[i].astype(BF16), w_ple[i].astype(BF16),
                        ple_norm_g[i][None, :], T)
    return h.reshape(B, T, D_MODEL)
```
